```python
import math
import jax
import jax.numpy as jnp
from jax import lax
import numpy as np

D_MODEL = 2048
BATCH = 2
SEQ = 4096
DEPTH = 1
DEC_BATCH = 8
DEC_SEQ = 1
PAST_LEN = 16384
PAGE_SIZE = 128

RET_HEADS = 4
RET_DK = 256
RET_DV = 256
RET_WIDTH = RET_HEADS * RET_DV
RET_CHUNK = 128
RET_ROPE_THETA = 10000.0
MOBA_HEADS = 8
MOBA_HEAD_DIM = 128
MOBA_WIDTH = MOBA_HEADS * MOBA_HEAD_DIM
MOBA_BLOCK = 256
MOBA_TOPK = 3
MOBA_QCHUNK = 16
ROPE_THETA = 500000.0
ROPE_DIM = MOBA_HEAD_DIM // 4
MIX_WIDTH = RET_WIDTH + MOBA_WIDTH
IN_SPLITS = (RET_HEADS * RET_DK, RET_HEADS * RET_DK, RET_WIDTH, RET_WIDTH, MOBA_WIDTH, MOBA_WIDTH, MOBA_WIDTH)
IN_WIDTH = sum(IN_SPLITS)
N_EXPERTS = 32
TOP_K = 4
D_FF = D_MODEL
SWIGLU_LIMIT = 7.0
SWIGLU_ALPHA = 1.702
MOE_MAX_BLOCK = 128
MOE_MIN_BLOCK = 8
PLE_DIM = 256
EPS = 1e-6

kernel_name = 'hymba_retention_moba_moe_step'


def rmsnorm(x, g):
    xf = x.astype(jnp.float32)
    y = xf * lax.rsqrt(jnp.mean(xf * xf, axis=-1, keepdims=True) + EPS)
    return (y * g.astype(jnp.float32)).astype(x.dtype)


def head_rms(x):
    xf = x.astype(jnp.float32)
    return xf * lax.rsqrt(jnp.mean(xf * xf, axis=-1, keepdims=True) + EPS)


def rope(x, pos, rot_dim, theta):
    half = rot_dim // 2
    inv = theta ** (-2.0 * jnp.arange(half, dtype=jnp.float32) / rot_dim)
    ang = pos.astype(jnp.float32)[:, None] * inv[None, :]
    cos = jnp.cos(ang)[None, :, None, :].astype(x.dtype)
    sin = jnp.sin(ang)[None, :, None, :].astype(x.dtype)
    x1 = x[..., :half]
    x2 = x[..., half:rot_dim]
    parts = [x1 * cos - x2 * sin, x2 * cos + x1 * sin]
    if rot_dim < x.shape[-1]:
        parts.append(x[..., rot_dim:])
    return jnp.concatenate(parts, axis=-1)


def project(x, pos, norm_mix, w_in, q_norm, k_norm):
    B, L, _ = x.shape
    z = rmsnorm(x, norm_mix) @ w_in
    offs = [int(o) for o in np.cumsum(IN_SPLITS)[:-1]]
    rq, rk, rv, rg, mq, mk, mv = jnp.split(z, offs, axis=-1)
    rq = rope(rq.reshape(B, L, RET_HEADS, RET_DK), pos, RET_DK, RET_ROPE_THETA)
    rk = rope(rk.reshape(B, L, RET_HEADS, RET_DK), pos, RET_DK, RET_ROPE_THETA) * (RET_DK ** -0.5)
    rv = rv.reshape(B, L, RET_HEADS, RET_DV)
    mq = rope(rmsnorm(mq.reshape(B, L, MOBA_HEADS, MOBA_HEAD_DIM), q_norm), pos, ROPE_DIM, ROPE_THETA)
    mk = rope(rmsnorm(mk.reshape(B, L, MOBA_HEADS, MOBA_HEAD_DIM), k_norm), pos, ROPE_DIM, ROPE_THETA)
    mv = mv.reshape(B, L, MOBA_HEADS, MOBA_HEAD_DIM)
    return rq, rk, rv, rg, mq, mk, mv


def retention(q, k, v, state0):
    B, L, H, _ = q.shape
    Dv = v.shape[-1]
    C = math.gcd(L, RET_CHUNK)
    n = L // C
    log_g = jnp.log1p(-jnp.exp2(-5.0 - jnp.arange(H, dtype=jnp.float32)))
    i = jnp.arange(C, dtype=jnp.float32)
    diff = i[:, None] - i[None, :]
    inner = jnp.where(diff >= 0, jnp.exp(log_g[:, None, None] * jnp.maximum(diff, 0.0)), 0.0)
    q_dec = jnp.exp(log_g[None, :] * (i[:, None] + 1.0))
    k_dec = jnp.exp(log_g[None, :] * (C - 1.0 - i)[:, None])
    chunk_dec = jnp.exp(log_g * C)

    def split(t):
        return t.astype(jnp.float32).reshape(B, n, C, H, t.shape[-1]).transpose(1, 0, 2, 3, 4)

    def step(R, inp):
        qc, kc, vc = inp
        s = jnp.einsum('bihd,bjhd->bhij', qc, kc) * inner[None]
        o = jnp.einsum('bhij,bjhe->bihe', s, vc) + jnp.einsum('bihd,bhde->bihe', qc * q_dec[:, :, None], R)
        R = R * chunk_dec[:, None, None] + jnp.einsum('bjhd,bjhe->bhde', kc * k_dec[:, :, None], vc)
        return R, o

    R, o = lax.scan(step, state0.astype(jnp.float32), (split(q), split(k), split(v)))
    return o.transpose(1, 0, 2, 3, 4).reshape(B, L, H, Dv), R


def moba_attend(q, k_parts, v_parts, q_pos):
    Bt, Q, H, Dh = q.shape
    L = sum(t.shape[1] for t in k_parts)
    nblk = -(-L // MOBA_BLOCK)
    pad = nblk * MOBA_BLOCK - L
    if pad:
        k_parts = list(k_parts) + [jnp.zeros((Bt, pad, H, Dh), k_parts[0].dtype)]
        v_parts = list(v_parts) + [jnp.zeros((Bt, pad, H, Dh), v_parts[0].dtype)]
    k_all = jnp.concatenate(k_parts, axis=1) if len(k_parts) > 1 else k_parts[0]
    v_all = jnp.concatenate(v_parts, axis=1) if len(v_parts) > 1 else v_parts[0]
    kb = k_all.reshape(Bt, nblk, MOBA_BLOCK, H, Dh)
    vb = v_all.reshape(Bt, nblk, MOBA_BLOCK, H, Dh)
    k_mean = kb.astype(jnp.float32).mean(axis=2)
    n_sel = min(MOBA_TOPK, nblk)
    qc_size = math.gcd(Q, MOBA_QCHUNK)
    n_chunks = Q // qc_size
    qs = q.reshape(Bt, n_chunks, qc_size, H, Dh).transpose(1, 0, 2, 3, 4)
    ps = q_pos.reshape(n_chunks, qc_size)
    b_idx = jnp.arange(Bt)[:, None, None, None]
    h_idx = jnp.arange(H)[None, None, :, None]
    blk_range = jnp.arange(nblk)
    tok_range = jnp.arange(MOBA_BLOCK)
    is_own = (jnp.arange(n_sel + 1) == n_sel)[None, None, None, :, None]
    scale = Dh ** -0.5

    def chunk(args):
        qc, pc = args
        qf = qc.astype(jnp.float32)
        own = pc // MOBA_BLOCK
        gate = jnp.einsum('bqhd,bnhd->bqhn', qf, k_mean)
        fully_past = blk_range[None, :] < own[:, None]
        gate = jnp.where(fully_past[None, :, None, :], gate, -jnp.inf)
        _, top = lax.top_k(gate, n_sel)
        own_b = jnp.broadcast_to(own[None, :, None, None], (Bt, qc_size, H, 1))
        sel = jnp.concatenate([top.astype(jnp.int32), own_b.astype(jnp.int32)], axis=-1)
        kg = kb[b_idx, sel, :, h_idx]
        vg = vb[b_idx, sel, :, h_idx]
        s = jnp.einsum('bqhd,bqhskd->bqhsk', qf, kg.astype(jnp.float32)) * scale
        key_pos = sel[..., None] * MOBA_BLOCK + tok_range
        past_ok = (sel < own[None, :, None, None])[..., None]
        causal_ok = key_pos <= pc[None, :, None, None, None]
        allowed = jnp.where(is_own, causal_ok, past_ok)
        s = jnp.where(allowed, s, -jnp.inf)
        prob = jax.nn.softmax(s.reshape(Bt, qc_size, H, -1), axis=-1).reshape(s.shape)
        o = jnp.einsum('bqhsk,bqhskd->bqhd', prob, vg.astype(jnp.float32))
        return o.astype(q.dtype)

    out = lax.map(chunk, (qs, ps))
    return out.transpose(1, 0, 2, 3, 4).reshape(Bt, Q, H, Dh)


def moe_ffn(x2d, w_router, b_router, w_gu, b_gu, w_down, b_down):
    T, D = x2d.shape
    logits = (x2d @ w_router).astype(jnp.float32) + b_router.astype(jnp.float32)
    top_val, top_idx = lax.top_k(logits, TOP_K)
    gates = jax.nn.softmax(top_val, axis=-1)
    tk = T * TOP_K
    blk = int(min(MOE_MAX_BLOCK, max(MOE_MIN_BLOCK, tk // N_EXPERTS)))
    n_blocks = -(-tk // blk) + N_EXPERTS
    flat_e = top_idx.reshape(-1)
    order = jnp.argsort(flat_e)
    sorted_e = flat_e[order]
    counts = jnp.bincount(flat_e, length=N_EXPERTS)
    padded = (counts + blk - 1) // blk * blk
    pad_end = jnp.cumsum(padded)
    pad_start = pad_end - padded
    start = jnp.cumsum(counts) - counts
    dest = pad_start[sorted_e] + jnp.arange(tk) - start[sorted_e]
    row_tok = jnp.full((n_blocks * blk,), T, jnp.int32).at[dest].set((order // TOP_K).astype(jnp.int32))
    row_gate = jnp.zeros((n_blocks * blk,), jnp.float32).at[dest].set(gates.reshape(-1)[order])
    block_e = jnp.minimum(jnp.searchsorted(pad_end, jnp.arange(n_blocks) * blk, side='right'), N_EXPERTS - 1)
    x_pad = jnp.concatenate([x2d, jnp.zeros((1, D), x2d.dtype)], axis=0)

    def expert_block(args):
        tok, e = args
        gu = x_pad[tok] @ w_gu[e] + b_gu[e]
        g = jnp.minimum(gu[:, :D_FF], SWIGLU_LIMIT)
        u = jnp.clip(gu[:, D_FF:], -SWIGLU_LIMIT, SWIGLU_LIMIT)
        act = (u + 1.0) * g * jax.nn.sigmoid(SWIGLU_ALPHA * g)
        return act @ w_down[e] + b_down[e]

    out = lax.map(expert_block, (row_tok.reshape(n_blocks, blk), block_e))
    out = out.reshape(n_blocks * blk, D).astype(jnp.float32) * row_gate[:, None]
    y = jax.ops.segment_sum(out, row_tok, num_segments=T + 1)[:T]
    return y.astype(x2d.dtype)


def decoder_layer(x, p, pos, ret_state, k_past, v_past, norm_mix, w_in, q_norm, k_norm, w_o, norm_ffn,
                  w_router, b_router, w_gu, b_gu, w_down, b_down, norm_ple, w_ple_gate, w_ple_proj):
    B, L, D = x.shape
    rq, rk, rv, rg, mq, mk, mv = project(x, pos, norm_mix, w_in, q_norm, k_norm)
    ret_o, ret_new = retention(rq, rk, rv, ret_state)
    if k_past is None:
        k_parts, v_parts = [mk], [mv]
    else:
        k_parts = [k_past, mk.astype(k_past.dtype)]
        v_parts = [v_past, mv.astype(v_past.dtype)]
    moba_o = moba_attend(mq, k_parts, v_parts, pos)
    ret = head_rms(ret_o).reshape(B, L, RET_WIDTH) * jax.nn.silu(rg.astype(jnp.float32))
    mixed = jnp.concatenate([ret.astype(x.dtype), moba_o.reshape(B, L, MOBA_WIDTH).astype(x.dtype)], axis=-1)
    h = x + mixed @ w_o
    f = moe_ffn(rmsnorm(h, norm_ffn).reshape(B * L, D), w_router, b_router, w_gu, b_gu, w_down, b_down)
    h = h + f.reshape(B, L, D)
    gate = jax.nn.sigmoid(rmsnorm(h, norm_ple) @ w_ple_gate)
    y = h + gate * (p @ w_ple_proj)
    return y, mk, mv, ret_new


def setup_inputs(seed: int = 0) -> dict:
    key = jax.random.key(seed)
    ks = jax.random.split(key, 24)
    f32 = jnp.float32
    n_pages = PAST_LEN // PAGE_SIZE
    n_used = DEC_BATCH * n_pages
    n_pool = n_used + n_used // 4
    nrm = lambda k, shape, s: jax.random.normal(k, shape, f32) * s
    page_table = jax.random.permutation(ks[5], n_pool)[:n_used].reshape(DEC_BATCH, n_pages).astype(jnp.int32)
    return {
        'x_prompt': nrm(ks[0], (BATCH, SEQ, D_MODEL), 1.0),
        'x_sample': nrm(ks[1], (DEC_BATCH, DEC_SEQ, D_MODEL), 1.0),
        'cache_k': nrm(ks[2], (DEPTH, n_pool, PAGE_SIZE, MOBA_HEADS, MOBA_HEAD_DIM), 1.0),
        'cache_v': nrm(ks[3], (DEPTH, n_pool, PAGE_SIZE, MOBA_HEADS, MOBA_HEAD_DIM), 1.0),
        'state_ret': nrm(ks[4], (DEPTH, DEC_BATCH, RET_HEADS, RET_DK, RET_DV), 0.05),
        'page_table': page_table,
        'p_prompt': nrm(ks[6], (DEPTH, BATCH, SEQ, PLE_DIM), 1.0),
        'p_sample': nrm(ks[7], (DEPTH, DEC_BATCH, DEC_SEQ, PLE_DIM), 1.0),
        'norm_mix': 1.0 + nrm(ks[8], (DEPTH, D_MODEL), 0.02),
        'w_in': nrm(ks[9], (DEPTH, D_MODEL, IN_WIDTH), D_MODEL ** -0.5),
        'q_norm': 1.0 + nrm(ks[10], (DEPTH, MOBA_HEAD_DIM), 0.02),
        'k_norm': 1.0 + nrm(ks[11], (DEPTH, MOBA_HEAD_DIM), 0.02),
        'w_o': nrm(ks[12], (DEPTH, MIX_WIDTH, D_MODEL), MIX_WIDTH ** -0.5),
        'norm_ffn': 1.0 + nrm(ks[13], (DEPTH, D_MODEL), 0.02),
        'w_router': nrm(ks[14], (DEPTH, D_MODEL, N_EXPERTS), D_MODEL ** -0.5),
        'b_router': nrm(ks[15], (DEPTH, N_EXPERTS), 0.01),
        'w_gu': nrm(ks[16], (DEPTH, N_EXPERTS, D_MODEL, 2 * D_FF), D_MODEL ** -0.5),
        'b_gu': nrm(ks[17], (DEPTH, N_EXPERTS, 2 * D_FF), 0.01),
        'w_down': nrm(ks[18], (DEPTH, N_EXPERTS, D_FF, D_MODEL), D_FF ** -0.5),
        'b_down': nrm(ks[19], (DEPTH, N_EXPERTS, D_MODEL), 0.01),
        'norm_ple': 1.0 + nrm(ks[20], (DEPTH, D_MODEL), 0.02),
        'w_ple_gate': nrm(ks[21], (DEPTH, D_MODEL, D_MODEL), D_MODEL ** -0.5),
        'w_ple_proj': nrm(ks[22], (DEPTH, PLE_DIM, D_MODEL), PLE_DIM ** -0.5),
    }


def reference(x_prompt, x_sample, cache_k, cache_v, state_ret, page_table, p_prompt, p_sample,
              norm_mix, w_in, q_norm, k_norm, w_o, norm_ffn, w_router, b_router, w_gu, b_gu,
              w_down, b_down, norm_ple, w_ple_gate, w_ple_proj):
    n_dec, n_pages = page_table.shape
    past_len = n_pages * cache_k.shape[2]
    pos_p = jnp.arange(x_prompt.shape[1], dtype=jnp.int32)
    pos_s = past_len + jnp.arange(x_sample.shape[1], dtype=jnp.int32)
    hp, hs = x_prompt, x_sample
    kp_l, vp_l, rp_l, ks_l, vs_l, rs_l = [], [], [], [], [], []
    for i in range(DEPTH):
        w = (norm_mix[i], w_in[i], q_norm[i], k_norm[i], w_o[i], norm_ffn[i], w_router[i], b_router[i],
             w_gu[i], b_gu[i], w_down[i], b_down[i], norm_ple[i], w_ple_gate[i], w_ple_proj[i])
        r0 = jnp.zeros((hp.shape[0], RET_HEADS, RET_DK, RET_DV), jnp.float32)
        hp, mk, mv, r_new = decoder_layer(hp, p_prompt[i], pos_p, r0, None, None, *w)
        kp_l.append(mk)
        vp_l.append(mv)
        rp_l.append(r_new.astype(state_ret.dtype))
        k_past = cache_k[i][page_table].reshape(n_dec, past_len, MOBA_HEADS, MOBA_HEAD_DIM)
        v_past = cache_v[i][page_table].reshape(n_dec, past_len, MOBA_HEADS, MOBA_HEAD_DIM)
        hs, mk, mv, r_new = decoder_layer(hs, p_sample[i], pos_s, state_ret[i], k_past, v_past, *w)
        ks_l.append(mk)
        vs_l.append(mv)
        rs_l.append(r_new.astype(state_ret.dtype))
    k_prompt = jnp.stack(kp_l)
    v_prompt = jnp.stack(vp_l)
    state_ret_prompt = jnp.stack(rp_l)
    k_sample = jnp.stack(ks_l)
    v_sample = jnp.stack(vs_l)
    state_ret_sample = jnp.stack(rs_l)
    return (hp, hs, k_prompt, v_prompt, state_ret_prompt, k_sample, v_sample, state_ret_sample)
```

```python
import functools
import math

import jax
import jax.numpy as jnp
from jax import lax
from jax.experimental import pallas as pl
from jax.experimental.pallas import tpu as pltpu

F32 = jnp.float32
BF16 = jnp.bfloat16

RET_HEADS = 4
RET_DK = 256
RET_DV = 256
RET_ROPE_THETA = 10000.0
MOBA_HEADS = 8
MOBA_HEAD_DIM = 128
MOBA_BLOCK = 256
MOBA_TOPK = 3
ROPE_THETA = 500000.0
ROPE_DIM = MOBA_HEAD_DIM // 4
N_EXPERTS = 32
TOP_K = 4
SWIGLU_LIMIT = 7.0
SWIGLU_ALPHA = 1.702
EPS = 1e-6

LANES = 128
SUBLANES = 8
VMEM_LIMIT_BYTES = 56 * 1024 * 1024

SAMPLE_ROWS = 16
RET_CHUNK = 256
MOE_ROW_BLOCK = 256
MOE_ITEM_ROWS = 2048
MOE_FF_TILE = 256
ROW_PIECES = 16
COMBINE_TOKENS = 128


def _cparams(semantics, vmem=VMEM_LIMIT_BYTES):
    return pltpu.CompilerParams(dimension_semantics=semantics, vmem_limit_bytes=vmem)


def _nt_dot(a, b, **kw):
    return lax.dot_general(a, b, (((1,), (1,)), ((), ())), preferred_element_type=F32, **kw)


def _rms(x):
    return x * lax.rsqrt(jnp.mean(x * x, axis=-1, keepdims=True) + EPS)


_IN_TN = 256


def _inproj_kernel(x_ref, g_ref, w_ref, cr_ref, sr_ref, cm_ref, sa_ref, sb_ref, qn_ref, kn_ref,
                   z_ref, xn_ref):
    j = pl.program_id(1)

    @pl.when(j == 0)
    def _():
        xn_ref[...] = (_rms(x_ref[...]) * g_ref[...]).astype(BF16)

    acc = jnp.dot(xn_ref[...], w_ref[...], preferred_element_type=F32)
    @pl.when(j < 8)
    def _():
        half = RET_DK // 2
        x1 = acc[:, :half]
        x2 = acc[:, half:]
        c = cr_ref[...]
        s = sr_ref[...]
        scale = jnp.where(j < 4, 1.0, RET_DK ** -0.5).astype(F32)
        z_ref[:, :half] = (x1 * c - x2 * s) * scale
        z_ref[:, half:] = (x2 * c + x1 * s) * scale

    @pl.when(((j >= 8) & (j < 16)) | (j >= 24))
    def _():
        z_ref[...] = acc

    @pl.when((j >= 16) & (j < 24))
    def _():
        gain = jnp.where(j < 20, qn_ref[...], kn_ref[...])
        for hh in range(_IN_TN // MOBA_HEAD_DIM):
            t = _rms(acc[:, hh * MOBA_HEAD_DIM:(hh + 1) * MOBA_HEAD_DIM]) * gain
            up = pltpu.roll(t, MOBA_HEAD_DIM - ROPE_DIM // 2, 1)
            dn = pltpu.roll(t, ROPE_DIM // 2, 1)
            z_ref[:, hh * MOBA_HEAD_DIM:(hh + 1) * MOBA_HEAD_DIM] = (
                t * cm_ref[...] + up * sa_ref[...] + dn * sb_ref[...])


def _rope_tables(pos):
    posf = pos.astype(F32)[:, None]
    half = RET_DK // 2
    inv = RET_ROPE_THETA ** (-2.0 * jnp.arange(half, dtype=F32) / RET_DK)
    ang = posf * inv[None, :]
    cr, sr = jnp.cos(ang), jnp.sin(ang)
    mh = ROPE_DIM // 2
    inv_m = ROPE_THETA ** (-2.0 * jnp.arange(mh, dtype=F32) / ROPE_DIM)
    ang_m = posf * inv_m[None, :]
    cos_m, sin_m = jnp.cos(ang_m), jnp.sin(ang_m)
    n = pos.shape[0]
    rest = MOBA_HEAD_DIM - ROPE_DIM
    cm = jnp.concatenate([cos_m, cos_m, jnp.ones((n, rest), F32)], axis=1)
    sa = jnp.concatenate([-sin_m, jnp.zeros((n, MOBA_HEAD_DIM - mh), F32)], axis=1)
    sb = jnp.concatenate([jnp.zeros((n, mh), F32), sin_m, jnp.zeros((n, rest), F32)], axis=1)
    return cr, sr, cm, sa, sb


def _inproj(x2d, norm, w_bf, tables, qn, kn, seq_len, tm):
    t, d = x2d.shape
    n_out = w_bf.shape[1]
    tpos = seq_len // tm
    cr, sr, cm, sa, sb = tables
    tab = pl.BlockSpec((tm, LANES), lambda i, j: (i % tpos, 0))
    vec = pl.BlockSpec((1, LANES), lambda i, j: (0, 0))
    return pl.pallas_call(
        _inproj_kernel,
        grid=(t // tm, n_out // _IN_TN),
        in_specs=[
            pl.BlockSpec((tm, d), lambda i, j: (i, 0)),
            pl.BlockSpec((1, d), lambda i, j: (0, 0)),
            pl.BlockSpec((d, _IN_TN), lambda i, j: (0, j)),
            tab, tab, tab, tab, tab, vec, vec,
        ],
        out_specs=pl.BlockSpec((tm, _IN_TN), lambda i, j: (i, j)),
        out_shape=jax.ShapeDtypeStruct((t, n_out), F32),
        scratch_shapes=[pltpu.VMEM((tm, d), BF16)],
        compiler_params=_cparams(("parallel", "arbitrary")),
        name="inproj",
    )(x2d, norm, w_bf, cr, sr, cm, sa, sb, qn, kn)


def _ret_gate(o, g):
    return _rms(o) * (g * jax.nn.sigmoid(g))


def _ret_prompt_kernel(logg_ref, q_ref, k_ref, v_ref, g_ref, o_ref, st_ref, r_ref):
    h = pl.program_id(1)
    c = pl.program_id(2)
    n = q_ref.shape[0]

    @pl.when(c == 0)
    def _():
        r_ref[...] = jnp.zeros_like(r_ref)

    lg = logg_ref[h]
    i = lax.broadcasted_iota(jnp.int32, (n, 1), 0).astype(F32)
    jj = lax.broadcasted_iota(jnp.int32, (1, n), 1).astype(F32)
    diff = i - jj
    inner = jnp.where(diff >= 0, jnp.exp(lg * jnp.maximum(diff, 0.0)), 0.0)
    q_dec = jnp.exp(lg * (i + 1.0))
    k_dec = jnp.exp(lg * (n - 1.0 - i))
    chunk_dec = jnp.exp(lg * jnp.full((1, 1), float(n), F32))

    q = q_ref[...]
    k = k_ref[...]
    vb = v_ref[...].astype(BF16)
    r = r_ref[...]
    s = _nt_dot(q.astype(BF16), k.astype(BF16)) * inner
    o = (jnp.dot(s.astype(BF16), vb, preferred_element_type=F32)
         + jnp.dot((q * q_dec).astype(BF16), r.astype(BF16), preferred_element_type=F32))
    kd_t = (k * k_dec).T.astype(BF16)
    r_new = r * chunk_dec + jnp.dot(kd_t, vb, preferred_element_type=F32)
    r_ref[...] = r_new
    o_ref[...] = _ret_gate(o, g_ref[...])

    @pl.when(c == pl.num_programs(2) - 1)
    def _():
        st_ref[0, 0] = r_new


def _ret_log_decay():
    return jnp.log1p(-jnp.exp2(-5.0 - jnp.arange(RET_HEADS, dtype=F32)))


def _ret_prompt(z, batch, seq_len):
    nc = seq_len // RET_CHUNK
    w = RET_DK

    def col(off):
        return pl.BlockSpec((RET_CHUNK, w), lambda b, h, c: (b * nc + c, off + h))

    return pl.pallas_call(
        _ret_prompt_kernel,
        grid=(batch, RET_HEADS, nc),
        in_specs=[pl.BlockSpec(memory_space=pltpu.SMEM),
                  col(0), col(RET_HEADS), col(2 * RET_HEADS), col(3 * RET_HEADS)],
        out_specs=[
            pl.BlockSpec((RET_CHUNK, RET_DV), lambda b, h, c: (b * nc + c, h)),
            pl.BlockSpec((1, 1, RET_DK, RET_DV), lambda b, h, c: (b, h, 0, 0)),
        ],
        scratch_shapes=[pltpu.VMEM((RET_DK, RET_DV), F32)],
        out_shape=[
            jax.ShapeDtypeStruct((batch * seq_len, RET_HEADS * RET_DV), F32),
            jax.ShapeDtypeStruct((batch, RET_HEADS, RET_DK, RET_DV), F32),
        ],
        compiler_params=_cparams(("parallel", "parallel", "arbitrary")),
        name="ret_prompt",
    )(_ret_log_decay(), z, z, z, z)


def _ret_sample_kernel(logg_ref, q_ref, kc_ref, kr_ref, v_ref, g_ref, s0_ref, o_ref, st_ref):
    h = pl.program_id(1)
    dec = jnp.exp(logg_ref[h] * jnp.ones((1, 1), F32))
    rnd = lambda a: a.astype(BF16).astype(F32)
    q = q_ref[0, 0]
    v = rnd(v_ref[0, 0])
    r0 = s0_ref[0, 0]
    qk = jnp.sum(rnd(q) * rnd(kr_ref[0, 0]), axis=-1, keepdims=True)
    q8 = jnp.broadcast_to(q * dec, (SUBLANES, RET_DK)).astype(BF16)
    qr = jnp.dot(q8, r0.astype(BF16), preferred_element_type=F32)[0:1]
    o = rnd(qk) * v + qr
    st_ref[0, 0] = r0 * dec + rnd(kc_ref[0, 0]) * v
    o_ref[0, 0] = _ret_gate(o, g_ref[0, 0])


def _ret_sample(q, k, v, g, state):
    n = q.shape[0]
    row = lambda a: a.reshape(n, RET_HEADS, 1, RET_DK)
    rspec = pl.BlockSpec((1, 1, 1, RET_DK), lambda b, h: (b, h, 0, 0))
    mspec = pl.BlockSpec((1, 1, RET_DK, RET_DV), lambda b, h: (b, h, 0, 0))
    o, st = pl.pallas_call(
        _ret_sample_kernel,
        grid=(n, RET_HEADS),
        in_specs=[pl.BlockSpec(memory_space=pltpu.SMEM),
                  rspec, pl.BlockSpec((1, 1, RET_DK, 1), lambda b, h: (b, h, 0, 0)),
                  rspec, rspec, rspec, mspec],
        out_specs=[rspec, mspec],
        out_shape=[jax.ShapeDtypeStruct((n, RET_HEADS, 1, RET_DV), F32),
                   jax.ShapeDtypeStruct(state.shape, F32)],
        compiler_params=_cparams(("parallel", "parallel")),
        name="ret_sample",
    )(_ret_log_decay(), row(q), k.reshape(n, RET_HEADS, RET_DK, 1), row(k), row(v), row(g), state)
    return o.reshape(n, RET_HEADS * RET_DV), st


def _moba_prompt_kernel(q_ref, k_ref, v_ref, o_ref, kb_ref, vb_ref, km_ref):
    qi = pl.program_id(2)
    blk = MOBA_BLOCK
    nb = k_ref.shape[0] // blk
    scale = MOBA_HEAD_DIM ** -0.5

    @pl.when(qi == 0)
    def _():
        kb_ref[...] = k_ref[...].astype(BF16)
        vb_ref[...] = v_ref[...].astype(BF16)
        km_ref[...] = jnp.zeros_like(km_ref)
        for n in range(nb):
            km_ref[n:n + 1, :] = jnp.mean(k_ref[n * blk:(n + 1) * blk, :], axis=0, keepdims=True)

    q = q_ref[...]
    qb = q.astype(BF16)
    gate = _nt_dot(qb, km_ref[...].astype(BF16))
    lane = lax.broadcasted_iota(jnp.int32, (blk, LANES), 1)
    past = lane < qi
    g = jnp.where(past, gate, -jnp.inf)
    cnt = jnp.zeros((blk, LANES), F32)
    for m in range(nb):
        gm = g[:, m:m + 1]
        beats = jnp.where(gm > g, 1.0, jnp.where((gm == g) & (lane > m), 1.0, 0.0))
        cnt = cnt + beats
    sel = jnp.where(past & (cnt < MOBA_TOPK), 1.0, 0.0)

    own = pl.ds(pl.multiple_of(qi * blk, blk), blk)
    s = _nt_dot(qb, kb_ref[own, :]) * scale
    row = lax.broadcasted_iota(jnp.int32, (blk, blk), 0)
    colk = lax.broadcasted_iota(jnp.int32, (blk, blk), 1)
    s = jnp.where(row >= colk, s, -jnp.inf)
    m0 = jnp.max(s, axis=1, keepdims=True)
    p = jnp.exp(s - m0)
    l0 = jnp.sum(p, axis=1, keepdims=True)
    a0 = jnp.dot(p.astype(BF16), vb_ref[own, :], preferred_element_type=F32)

    def body(n, carry):
        m_i, l_i, a_i = carry
        rows = pl.ds(pl.multiple_of(n * blk, blk), blk)
        sn = _nt_dot(qb, kb_ref[rows, :]) * scale
        allowed = jnp.sum(jnp.where(lane == n, sel, 0.0), axis=1, keepdims=True) > 0.0
        sn = jnp.where(allowed, sn, -jnp.inf)
        m_new = jnp.maximum(m_i, jnp.max(sn, axis=1, keepdims=True))
        alpha = jnp.exp(m_i - m_new)
        pn = jnp.exp(sn - m_new)
        l_new = alpha * l_i + jnp.sum(pn, axis=1, keepdims=True)
        a_new = alpha * a_i + jnp.dot(pn.astype(BF16), vb_ref[rows, :], preferred_element_type=F32)
        return m_new, l_new, a_new

    _, l_f, a_f = lax.fori_loop(0, qi, body, (m0, l0, a0))
    o_ref[...] = a_f / l_f


def _moba_prompt(z, batch, seq_len):
    nq = seq_len // MOBA_BLOCK
    hd = MOBA_HEAD_DIM
    qoff = (2 * RET_HEADS * RET_DK + 2 * RET_HEADS * RET_DV) // hd
    koff = qoff + MOBA_HEADS
    voff = koff + MOBA_HEADS
    full = lambda off: pl.BlockSpec((seq_len, hd), lambda b, h, i: (b, off + h))
    return pl.pallas_call(
        _moba_prompt_kernel,
        grid=(batch, MOBA_HEADS, nq),
        in_specs=[pl.BlockSpec((MOBA_BLOCK, hd), lambda b, h, i: (b * nq + i, qoff + h)),
                  full(koff), full(voff)],
        out_specs=pl.BlockSpec((MOBA_BLOCK, hd), lambda b, h, i: (b * nq + i, h)),
        out_shape=jax.ShapeDtypeStruct((batch * seq_len, MOBA_HEADS * hd), F32),
        scratch_shapes=[pltpu.VMEM((seq_len, hd), BF16), pltpu.VMEM((seq_len, hd), BF16),
                        pltpu.VMEM((LANES, hd), F32)],
        compiler_params=_cparams(("parallel", "parallel", "arbitrary")),
        name="moba_prompt",
    )(z, z, z)


def _moba_scan_kernel(pt_ref, q_ref, k_ref, s_ref, sel_ref, gate_ref, ksum_ref):
    p = pl.program_id(1)
    npages = pl.num_programs(1)
    hd = MOBA_HEAD_DIM
    ppb = MOBA_BLOCK // k_ref.shape[1]

    @pl.when(p == 0)
    def _():
        gate_ref[...] = jnp.zeros_like(gate_ref)

    q = q_ref[0]
    hrow = lax.broadcasted_iota(jnp.int32, (MOBA_HEADS, hd), 0)
    qbd = jnp.concatenate([jnp.where(hrow == h, q, 0.0) for h in range(MOBA_HEADS)],
                          axis=1).astype(BF16)
    kp = k_ref[0]
    s_ref[0] = _nt_dot(qbd, kp.astype(BF16))
    lane = lax.broadcasted_iota(jnp.int32, gate_ref.shape, 1)
    page_sum = jnp.sum(kp, axis=0, keepdims=True)

    @pl.when(p % ppb == 0)
    def _():
        ksum_ref[...] = page_sum

    @pl.when(p % ppb != 0)
    def _():
        ksum_ref[...] += page_sum

    @pl.when(p % ppb == ppb - 1)
    def _():
        k_mean = ksum_ref[...] * (1.0 / MOBA_BLOCK)
        km8 = jnp.broadcast_to(k_mean, (SUBLANES, k_mean.shape[1])).astype(BF16)
        g = _nt_dot(qbd, km8)[:, 0:1]
        gate_ref[...] = jnp.where(lane == p // ppb, g, gate_ref[...])

    @pl.when(p == npages - 1)
    def _():
        g = jnp.where(lane < npages // ppb, gate_ref[...], -jnp.inf)
        out = jnp.zeros(gate_ref.shape, jnp.int32)
        for t in range(MOBA_TOPK):
            mx = jnp.max(g, axis=1, keepdims=True)
            idx = jnp.min(jnp.where(g == mx, lane, LANES), axis=1, keepdims=True)
            out = jnp.where(lane == t, idx, out)
            g = jnp.where(lane == idx, -jnp.inf, g)
        sel_ref[0] = out


def _moba_attend_kernel(pt_ref, selp_ref, q_ref, kn_ref, vn_ref, *refs):
    nsrc = MOBA_TOPK * 2
    s_refs = refs[:nsrc]
    v_refs = refs[nsrc:2 * nsrc]
    o_ref = refs[2 * nsrc]
    h = pl.program_id(1)
    scale = MOBA_HEAD_DIM ** -0.5
    hrow = lax.broadcasted_iota(jnp.int32, (MOBA_HEADS, MOBA_HEAD_DIM), 0)
    mine = hrow == h
    pick = lambda a: jnp.sum(jnp.where(mine, a, 0.0), axis=0, keepdims=True)
    q = pick(q_ref[0])
    rnd = lambda a: a.astype(BF16).astype(F32)
    s_new = jnp.sum(rnd(q) * rnd(pick(kn_ref[0])), axis=1, keepdims=True) * scale
    ss = [pick(r[0]) * scale for r in s_refs]
    mx = s_new
    for sj in ss:
        mx = jnp.maximum(mx, jnp.max(sj, axis=1, keepdims=True))
    p_new = jnp.exp(s_new - mx)
    ps = [jnp.exp(sj - mx) for sj in ss]
    den = p_new
    for pj in ps:
        den = den + jnp.sum(pj, axis=1, keepdims=True)
    inv = 1.0 / den
    acc = rnd(p_new * inv) * rnd(pick(vn_ref[0]))
    for pj, vr in zip(ps, v_refs):
        p8 = jnp.broadcast_to(pj * inv, (SUBLANES, pj.shape[1])).astype(BF16)
        acc = acc + jnp.dot(p8, vr[0].astype(BF16), preferred_element_type=F32)[0:1]
    o_ref[0, 0] = acc


def _moba_sample(q, k_new, v_new, cache_k, cache_v, page_table):
    n, npages = page_table.shape
    pool, page = cache_k.shape[0], cache_k.shape[1]
    hd = MOBA_HEAD_DIM
    width = MOBA_HEADS * hd
    ppb = MOBA_BLOCK // page
    ck = cache_k.reshape(pool, page, width)
    cv = cache_v.reshape(pool, page, width)
    scores, sel = pl.pallas_call(
        _moba_scan_kernel,
        grid_spec=pltpu.PrefetchScalarGridSpec(
            num_scalar_prefetch=1,
            grid=(n, npages),
            in_specs=[pl.BlockSpec((1, MOBA_HEADS, hd), lambda b, p, pt: (b, 0, 0)),
                      pl.BlockSpec((1, page, width), lambda b, p, pt: (pt[b, p], 0, 0))],
            out_specs=[pl.BlockSpec((1, MOBA_HEADS, page), lambda b, p, pt: (b, 0, p)),
                       pl.BlockSpec((1, MOBA_HEADS, LANES), lambda b, p, pt: (b, 0, 0))],
            scratch_shapes=[pltpu.VMEM((MOBA_HEADS, LANES), F32), pltpu.VMEM((1, width), F32)],
        ),
        out_shape=[jax.ShapeDtypeStruct((n, MOBA_HEADS, npages * page), F32),
                   jax.ShapeDtypeStruct((n, MOBA_HEADS, LANES), jnp.int32)],
        compiler_params=_cparams(("parallel", "arbitrary")),
        name="moba_scan",
    )(page_table, q, ck)
    selp = (sel[:, :, :MOBA_TOPK, None] * ppb + jnp.arange(ppb, dtype=jnp.int32)).reshape(-1)
    nsrc = MOBA_TOPK * ppb

    def src(j):
        return lambda b, h, pt, sp: sp[(b * MOBA_HEADS + h) * nsrc + j]

    s_specs = [pl.BlockSpec((1, MOBA_HEADS, page), (lambda b, h, pt, sp, f=src(j): (b, 0, f(b, h, pt, sp))))
               for j in range(nsrc)]
    v_specs = [pl.BlockSpec((1, page, hd),
                            (lambda b, h, pt, sp, f=src(j): (pt[b, f(b, h, pt, sp)], 0, h)))
               for j in range(nsrc)]
    tok = pl.BlockSpec((1, MOBA_HEADS, hd), lambda b, h, pt, sp: (b, 0, 0))
    out = pl.pallas_call(
        _moba_attend_kernel,
        grid_spec=pltpu.PrefetchScalarGridSpec(
            num_scalar_prefetch=2,
            grid=(n, MOBA_HEADS),
            in_specs=[tok, tok, tok] + s_specs + v_specs,
            out_specs=pl.BlockSpec((1, 1, 1, hd), lambda b, h, pt, sp: (b, h, 0, 0)),
        ),
        out_shape=jax.ShapeDtypeStruct((n, MOBA_HEADS, 1, hd), F32),
        compiler_params=_cparams(("parallel", "parallel")),
        name="moba_attend",
    )(page_table, selp, q, k_new, v_new, *([scores] * nsrc), *([cv] * nsrc))
    return out.reshape(n, width)


def _outproj_kernel(x_ref, ret_ref, moba_ref, wo_ref, nf_ref, wr_ref, br_ref,
                    h_ref, hn_ref, idx_ref, gate_ref):
    rw = ret_ref.shape[1]
    h = (x_ref[...]
         + jnp.dot(ret_ref[...].astype(BF16), wo_ref[:rw, :], preferred_element_type=F32)
         + jnp.dot(moba_ref[...].astype(BF16), wo_ref[rw:, :], preferred_element_type=F32))
    h_ref[...] = h
    hn = _rms(h) * nf_ref[...]
    hn_ref[...] = hn
    logits = jnp.dot(hn.astype(BF16), wr_ref[...], preferred_element_type=F32) + br_ref[...]
    lane = lax.broadcasted_iota(jnp.int32, logits.shape, 1)
    vals, idxs = [], []
    for _ in range(TOP_K):
        mx = jnp.max(logits, axis=1, keepdims=True)
        ix = jnp.min(jnp.where(logits == mx, lane, LANES), axis=1, keepdims=True)
        vals.append(mx)
        idxs.append(ix)
        logits = jnp.where(lane == ix, -jnp.inf, logits)
    es = [jnp.exp(v - vals[0]) for v in vals]
    den = es[0] + es[1] + es[2] + es[3]
    idx_out = jnp.zeros(lane.shape, jnp.int32)
    gate_out = jnp.zeros(lane.shape, F32)
    for t in range(TOP_K):
        idx_out = jnp.where(lane == t, idxs[t], idx_out)
        gate_out = jnp.where(lane == t, es[t] / den, gate_out)
    idx_ref[...] = idx_out
    gate_ref[...] = gate_out


def _outproj(x2d, ret, moba, wo_bf, norm_ffn, wr_pad, br_pad, tm):
    t, d = x2d.shape
    rw, mw = ret.shape[1], moba.shape[1]
    const = lambda shape: pl.BlockSpec(shape, lambda i: (0, 0))
    rows = lambda w: pl.BlockSpec((tm, w), lambda i: (i, 0))
    return pl.pallas_call(
        _outproj_kernel,
        grid=(t // tm,),
        in_specs=[rows(d), rows(rw), rows(mw), const((rw + mw, d)), const((1, d)),
                  const((d, LANES)), const((1, LANES))],
        out_specs=[rows(d), rows(d), rows(LANES), rows(LANES)],
        out_shape=[jax.ShapeDtypeStruct((t, d), F32), jax.ShapeDtypeStruct((t, d), F32),
                   jax.ShapeDtypeStruct((t, LANES), jnp.int32),
                   jax.ShapeDtypeStruct((t, LANES), F32)],
        compiler_params=_cparams(("parallel",)),
        name="outproj",
    )(x2d, ret, moba, wo_bf, norm_ffn, wr_pad, br_pad)


def _dispatch_kernel(dest_ref, pad_lo_ref, pad_hi_ref, hn_hbm, xs_hbm, zero_ref, sem, zsem):
    i = pl.program_id(0)
    tb = COMBINE_TOKENS
    ntok = hn_hbm.shape[0]
    first = i * tb
    count = jnp.minimum(tb, ntok - first)

    def row_copy(t, d):
        return pltpu.make_async_copy(hn_hbm.at[t], xs_hbm.at[d], sem)

    def issue(r, c):
        for k in range(TOP_K):
            row_copy(first + r, dest_ref[(first + r) * TOP_K + k]).start()
        return c

    lax.fori_loop(0, count, issue, 0)

    @pl.when(i == 0)
    def _():
        zero_ref[...] = jnp.zeros_like(zero_ref)

        def pad_copy(s):
            return pltpu.make_async_copy(zero_ref.at[0], xs_hbm.at[s], zsem)

        def per_expert(e, c):
            def pad_row(s, c2):
                pad_copy(s).start()
                return c2
            lax.fori_loop(pad_lo_ref[e], pad_hi_ref[e], pad_row, 0)
            return c

        lax.fori_loop(0, N_EXPERTS, per_expert, 0)

        rb = zero_ref.shape[0]
        tail0 = pad_hi_ref[N_EXPERTS - 1]
        n_tail = (xs_hbm.shape[0] - tail0) // rb

        def tail_copy(c):
            rows = pl.ds(pl.multiple_of(tail0 + c * rb, rb), rb)
            return pltpu.make_async_copy(zero_ref, xs_hbm.at[rows], zsem)

        def tail_start(c, c2):
            tail_copy(c).start()
            return c2

        lax.fori_loop(0, n_tail, tail_start, 0)

        def per_expert_wait(e, c):
            def pad_wait(s, c2):
                pad_copy(s).wait()
                return c2
            lax.fori_loop(pad_lo_ref[e], pad_hi_ref[e], pad_wait, 0)
            return c

        lax.fori_loop(0, N_EXPERTS, per_expert_wait, 0)

        def tail_wait(c, c2):
            tail_copy(c).wait()
            return c2

        lax.fori_loop(0, n_tail, tail_wait, 0)

    def drain(r, c):
        for k in range(TOP_K):
            row_copy(first + r, dest_ref[(first + r) * TOP_K + k]).wait()
        return c

    lax.fori_loop(0, count, drain, 0)


def _dispatch(hn3, dest_flat, pad_lo, pad_hi, n_slots):
    ntok = hn3.shape[0]
    return pl.pallas_call(
        _dispatch_kernel,
        grid_spec=pltpu.PrefetchScalarGridSpec(
            num_scalar_prefetch=3,
            grid=(pl.cdiv(ntok, COMBINE_TOKENS),),
            in_specs=[pl.BlockSpec(memory_space=pl.ANY)],
            out_specs=pl.BlockSpec(memory_space=pl.ANY),
            scratch_shapes=[pltpu.VMEM((MOE_ROW_BLOCK,) + hn3.shape[1:], F32),
                            pltpu.SemaphoreType.DMA(()), pltpu.SemaphoreType.DMA(())],
        ),
        out_shape=jax.ShapeDtypeStruct((n_slots,) + hn3.shape[1:], F32),
        compiler_params=_cparams(("arbitrary",)),
        name="moe_dispatch",
    )(dest_flat, pad_lo, pad_hi, hn3)


def _moe_kernel(item_e, item_row0, item_nch, tail_ref, xs_hbm, wg_ref, wu_ref, bg_ref, bu_ref,
                wd_ref, bd_ref, out_hbm, x_ref, acc_ref, stage_ref, wgb_ref, wub_ref, wdb_ref,
                sem_in, sem_out):
    it = pl.program_id(0)
    f = pl.program_id(1)
    nf = pl.num_programs(1)
    rb = MOE_ROW_BLOCK
    pieces = ROW_PIECES
    nch = item_nch[it]
    row0 = item_row0[it]

    def chunk_rows(r):
        return pl.ds(pl.multiple_of((row0 + r * rb) * pieces, rb * pieces), rb * pieces)

    def in_copy(r, slot):
        return pltpu.make_async_copy(xs_hbm.at[chunk_rows(r)], stage_ref.at[slot], sem_in.at[slot])

    def out_copy(r, slot):
        return pltpu.make_async_copy(stage_ref.at[slot], out_hbm.at[chunk_rows(r)], sem_out.at[slot])

    @pl.when((f == 0) & (nch > 0))
    def _():
        in_copy(0, 0).start()

        def load(r, c):
            slot = r % 2

            @pl.when(r + 1 < nch)
            def _():
                in_copy(r + 1, 1 - slot).start()

            in_copy(r, slot).wait()
            rows = pl.ds(pl.multiple_of(r * rb, rb), rb)
            for p in range(pieces):
                piece = stage_ref[slot, pl.ds(p, rb, stride=pieces), :]
                x_ref[rows, p * LANES:(p + 1) * LANES] = piece.astype(BF16)
            acc_ref[rows, :] = jnp.broadcast_to(bd_ref[0], (rb, acc_ref.shape[1]))
            return c

        lax.fori_loop(0, nch, load, 0)

    @pl.when(nch > 0)
    def _():
        wgb_ref[...] = wg_ref[0].astype(BF16)
        wub_ref[...] = wu_ref[0].astype(BF16)
        wdb_ref[...] = wd_ref[0].astype(BF16)
        bg = bg_ref[0]
        bu = bu_ref[0]

        def compute(r, c):
            rows = pl.ds(pl.multiple_of(r * rb, rb), rb)
            x = x_ref[rows, :]
            g = jnp.dot(x, wgb_ref[...], preferred_element_type=F32) + bg
            u = jnp.dot(x, wub_ref[...], preferred_element_type=F32) + bu
            g = jnp.minimum(g, SWIGLU_LIMIT)
            u = jnp.clip(u, -SWIGLU_LIMIT, SWIGLU_LIMIT)
            act = (u + 1.0) * g * jax.nn.sigmoid(SWIGLU_ALPHA * g)
            acc_ref[rows, :] += jnp.dot(act.astype(BF16), wdb_ref[...], preferred_element_type=F32)
            return c

        lax.fori_loop(0, nch, compute, 0)

    @pl.when((f == nf - 1) & (nch > 0))
    def _():
        def store(r, c):
            slot = r % 2

            @pl.when(r >= 2)
            def _():
                out_copy(r - 2, slot).wait()

            rows = pl.ds(pl.multiple_of(r * rb, rb), rb)
            for p in range(pieces):
                stage_ref[slot, pl.ds(p, rb, stride=pieces), :] = acc_ref[rows, p * LANES:(p + 1) * LANES]
            out_copy(r, slot).start()
            return c

        lax.fori_loop(0, nch, store, 0)

        @pl.when(nch >= 2)
        def _():
            out_copy(nch - 2, nch % 2).wait()

        out_copy(nch - 1, (nch - 1) % 2).wait()

    @pl.when((it == pl.num_programs(0) - 1) & (f == nf - 1))
    def _():
        tail0 = tail_ref[0]
        n_tail = (out_hbm.shape[0] // pieces - tail0) // rb
        stage_ref[0] = jnp.zeros(stage_ref.shape[1:], F32)

        def tail_copy(c):
            rows = pl.ds(pl.multiple_of((tail0 + c * rb) * pieces, rb * pieces), rb * pieces)
            return pltpu.make_async_copy(stage_ref.at[0], out_hbm.at[rows], sem_out.at[0])

        def tail_start(c, c2):
            tail_copy(c).start()
            return c2

        def tail_wait(c, c2):
            tail_copy(c).wait()
            return c2

        lax.fori_loop(0, n_tail, tail_start, 0)
        lax.fori_loop(0, n_tail, tail_wait, 0)


def _moe(xs3, items, w_gu, b_gu, w_down, b_down):
    item_e, item_row0, item_nch, tail0 = items
    n_items = item_e.shape[0]
    n_slots = xs3.shape[0]
    d = w_gu.shape[1]
    ff = w_down.shape[1]
    tf = MOE_FF_TILE
    nf = ff // tf
    xs2 = xs3.reshape(n_slots * ROW_PIECES, LANES)

    def ftile(it, f, nch):
        return jnp.where(nch[it] > 0, f, nf - 1)

    wg_spec = pl.BlockSpec((1, d, tf), lambda it, f, e, r0, nch, t0: (e[it], 0, ftile(it, f, nch)))
    wu_spec = pl.BlockSpec((1, d, tf), lambda it, f, e, r0, nch, t0: (e[it], 0, nf + ftile(it, f, nch)))
    bg_spec = pl.BlockSpec((1, 1, tf), lambda it, f, e, r0, nch, t0: (e[it], 0, ftile(it, f, nch)))
    bu_spec = pl.BlockSpec((1, 1, tf), lambda it, f, e, r0, nch, t0: (e[it], 0, nf + ftile(it, f, nch)))
    wd_spec = pl.BlockSpec((1, tf, d), lambda it, f, e, r0, nch, t0: (e[it], ftile(it, f, nch), 0))
    bd_spec = pl.BlockSpec((1, 1, d), lambda it, f, e, r0, nch, t0: (e[it], 0, 0))
    b_gu3 = b_gu.reshape(N_EXPERTS, 1, 2 * ff)
    b_down3 = b_down.reshape(N_EXPERTS, 1, d)
    out2 = pl.pallas_call(
        _moe_kernel,
        grid_spec=pltpu.PrefetchScalarGridSpec(
            num_scalar_prefetch=4,
            grid=(n_items, nf),
            in_specs=[pl.BlockSpec(memory_space=pl.ANY), wg_spec, wu_spec, bg_spec, bu_spec,
                      wd_spec, bd_spec],
            out_specs=pl.BlockSpec(memory_space=pl.ANY),
            scratch_shapes=[
                pltpu.VMEM((MOE_ITEM_ROWS, d), BF16),
                pltpu.VMEM((MOE_ITEM_ROWS, d), F32),
                pltpu.VMEM((2, MOE_ROW_BLOCK * ROW_PIECES, LANES), F32),
                pltpu.VMEM((d, tf), BF16), pltpu.VMEM((d, tf), BF16), pltpu.VMEM((tf, d), BF16),
                pltpu.SemaphoreType.DMA((2,)), pltpu.SemaphoreType.DMA((2,)),
            ],
        ),
        out_shape=jax.ShapeDtypeStruct(xs2.shape, F32),
        compiler_params=_cparams(("arbitrary", "arbitrary")),
        name="moe_experts",
    )(item_e, item_row0, item_nch, tail0, xs2, w_gu, w_gu, b_gu3, b_gu3, w_down, b_down3)
    return out2


def _combine_kernel(dest_ref, h_ref, gate_ref, out_hbm, o_ref, buf_ref, sem, *, tok_off):
    i = pl.program_id(0)
    tb = h_ref.shape[0]
    pieces = ROW_PIECES
    base = (tok_off + i * tb) * TOP_K

    def row_copy(r, k):
        d = dest_ref[base + r * TOP_K + k]
        src = out_hbm.at[pl.ds(pl.multiple_of(d * pieces, pieces), pieces)]
        dst = buf_ref.at[pl.ds(pl.multiple_of((k * tb + r) * pieces, pieces), pieces)]
        return pltpu.make_async_copy(src, dst, sem)

    def issue(r, c):
        for k in range(TOP_K):
            row_copy(r, k).start()
        return c

    lax.fori_loop(0, tb, issue, 0)

    def drain(r, c):
        for k in range(TOP_K):
            row_copy(r, k).wait()
        return c

    lax.fori_loop(0, tb, drain, 0)

    for p in range(pieces):
        f = jnp.zeros((tb, LANES), F32)
        for k in range(TOP_K):
            rows = buf_ref[pl.ds(k * tb * pieces + p, tb, stride=pieces), :]
            f = f + gate_ref[:, k:k + 1] * rows
        o_ref[:, p * LANES:(p + 1) * LANES] = h_ref[:, p * LANES:(p + 1) * LANES] + f


def _combine(dest_flat, h, gates, out2, tok_off, tb):
    t, d = h.shape
    return pl.pallas_call(
        functools.partial(_combine_kernel, tok_off=tok_off),
        grid_spec=pltpu.PrefetchScalarGridSpec(
            num_scalar_prefetch=1,
            grid=(t // tb,),
            in_specs=[pl.BlockSpec((tb, d), lambda i, ds: (i, 0)),
                      pl.BlockSpec((tb, LANES), lambda i, ds: (i, 0)),
                      pl.BlockSpec(memory_space=pl.ANY)],
            out_specs=pl.BlockSpec((tb, d), lambda i, ds: (i, 0)),
            scratch_shapes=[pltpu.VMEM((TOP_K * tb * ROW_PIECES, LANES), F32),
                            pltpu.SemaphoreType.DMA(())],
        ),
        out_shape=jax.ShapeDtypeStruct((t, d), F32),
        compiler_params=_cparams(("arbitrary",)),
        name="moe_combine",
    )(dest_flat, h, gates, out2)


def _ple_kernel(h_ref, p_ref, n_ref, wg_ref, wp_ref, y_ref):
    h = h_ref[...]
    hn = (_rms(h) * n_ref[...]).astype(BF16)
    gate = jax.nn.sigmoid(jnp.dot(hn, wg_ref[...], preferred_element_type=F32))
    proj = jnp.dot(p_ref[...].astype(BF16), wp_ref[...], preferred_element_type=F32)
    y_ref[...] = h + gate * proj


def _ple(h, p, norm, wg_bf, wp_bf, tm):
    t, d = h.shape
    pd = p.shape[1]
    const = lambda shape: pl.BlockSpec(shape, lambda i: (0, 0))
    return pl.pallas_call(
        _ple_kernel,
        grid=(t // tm,),
        in_specs=[pl.BlockSpec((tm, d), lambda i: (i, 0)), pl.BlockSpec((tm, pd), lambda i: (i, 0)),
                  const((1, d)), const((d, d)), const((pd, d))],
        out_specs=pl.BlockSpec((tm, d), lambda i: (i, 0)),
        out_shape=jax.ShapeDtypeStruct((t, d), F32),
        compiler_params=_cparams(("parallel",)),
        name="ple",
    )(h, p, norm, wg_bf, wp_bf)


def _route(top_idx, n_items):
    rb = MOE_ROW_BLOCK
    flat = top_idx.reshape(-1)
    onehot = (flat[:, None] == jnp.arange(N_EXPERTS, dtype=jnp.int32)[None, :]).astype(jnp.int32)
    csum = jnp.cumsum(onehot, axis=0)
    rank = jnp.sum(onehot * (csum - 1), axis=1)
    counts = csum[-1]
    padded = (counts + rb - 1) // rb * rb
    seg_end = jnp.cumsum(padded)
    seg_start = seg_end - padded
    dest = (jnp.sum(onehot * seg_start[None, :], axis=1) + rank).astype(jnp.int32)
    pad_lo = (seg_start + counts).astype(jnp.int32)
    pad_hi = seg_end.astype(jnp.int32)
    per = (padded + MOE_ITEM_ROWS - 1) // MOE_ITEM_ROWS
    item_end = jnp.cumsum(per)
    item_start = item_end - per
    ids = jnp.arange(n_items, dtype=jnp.int32)
    e = jnp.minimum(jnp.searchsorted(item_end, ids, side="right"), N_EXPERTS - 1).astype(jnp.int32)
    valid = ids < item_end[-1]
    piece = ids - item_start[e]
    row0 = seg_start[e] + piece * MOE_ITEM_ROWS
    rows = jnp.clip(padded[e] - piece * MOE_ITEM_ROWS, 0, MOE_ITEM_ROWS)
    nch = jnp.where(valid, rows // rb, 0).astype(jnp.int32)
    last_e = e[jnp.maximum(item_end[-1] - 1, 0)]
    item_e = jnp.where(valid, e, last_e).astype(jnp.int32)
    item_row0 = jnp.where(valid, row0, 0).astype(jnp.int32)
    return dest, pad_lo, pad_hi, (item_e, item_row0, nch, pad_hi[-1:])


def _pad_rows(a, rows):
    return jnp.concatenate([a, jnp.zeros((rows - a.shape[0],) + a.shape[1:], a.dtype)], axis=0)


def kernel(x_prompt, x_sample, cache_k, cache_v, state_ret, page_table, p_prompt, p_sample,
           norm_mix, w_in, q_norm, k_norm, w_o, norm_ffn, w_router, b_router, w_gu, b_gu,
           w_down, b_down, norm_ple, w_ple_gate, w_ple_proj):
    depth = norm_mix.shape[0]
    assert depth == 1
    batch, seq_len, d = x_prompt.shape
    n_dec, dec_seq, _ = x_sample.shape
    assert dec_seq == 1 and n_dec <= SAMPLE_ROWS
    past_len = page_table.shape[1] * cache_k.shape[2]
    ret_w = RET_HEADS * RET_DK
    moba_w = MOBA_HEADS * MOBA_HEAD_DIM
    off = [0, ret_w, 2 * ret_w, 3 * ret_w, 4 * ret_w, 4 * ret_w + moba_w, 4 * ret_w + 2 * moba_w]

    w_in_bf = w_in[0].astype(BF16)
    w_o_bf = w_o[0].astype(BF16)
    wg_ple_bf = w_ple_gate[0].astype(BF16)
    wp_ple_bf = w_ple_proj[0].astype(BF16)
    wr_pad = jnp.concatenate([w_router[0], jnp.zeros((d, LANES - N_EXPERTS), F32)],
                             axis=1).astype(BF16)
    br_pad = jnp.concatenate([b_router[0], jnp.full((LANES - N_EXPERTS,), -jnp.inf, F32)])[None, :]
    qn, kn = q_norm, k_norm

    t_p = batch * seq_len
    xp = x_prompt.reshape(t_p, d)
    tm = 1024
    tables_p = _rope_tables(jnp.arange(seq_len, dtype=jnp.int32))
    z_p = _inproj(xp, norm_mix, w_in_bf, tables_p, qn, kn, seq_len, tm)
    ret_p, state_p = _ret_prompt(z_p, batch, seq_len)
    moba_p = _moba_prompt(z_p, batch, seq_len)

    xs_rows = _pad_rows(x_sample.reshape(n_dec, d), SAMPLE_ROWS)
    tables_s = _rope_tables(jnp.full((SAMPLE_ROWS,), past_len, jnp.int32))
    z_s = _inproj(xs_rows, norm_mix, w_in_bf, tables_s, qn, kn, SAMPLE_ROWS, SAMPLE_ROWS)
    zs = z_s[:n_dec]
    ret_s, state_s = _ret_sample(zs[:, off[0]:off[1]], zs[:, off[1]:off[2]], zs[:, off[2]:off[3]],
                                 zs[:, off[3]:off[4]], state_ret[0])
    heads = lambda a: a.reshape(n_dec, MOBA_HEADS, MOBA_HEAD_DIM)
    mk_s, mv_s = zs[:, off[5]:off[6]], zs[:, off[6]:]
    moba_s = _moba_sample(heads(zs[:, off[4]:off[5]]), heads(mk_s), heads(mv_s),
                          cache_k[0], cache_v[0], page_table)

    h_p, hn_p, idx_p, gate_p = _outproj(xp, ret_p, moba_p, w_o_bf, norm_ffn, wr_pad, br_pad, 256)
    h_s, hn_s, idx_s, gate_s = _outproj(xs_rows, _pad_rows(ret_s, SAMPLE_ROWS),
                                        _pad_rows(moba_s, SAMPLE_ROWS), w_o_bf, norm_ffn,
                                        wr_pad, br_pad, SAMPLE_ROWS)

    n_tok = t_p + n_dec
    top_idx = jnp.concatenate([idx_p[:, :TOP_K], idx_s[:n_dec, :TOP_K]], axis=0)
    n_assign = n_tok * TOP_K
    n_chunks = n_assign // MOE_ROW_BLOCK + N_EXPERTS
    n_slots = n_chunks * MOE_ROW_BLOCK
    n_items = N_EXPERTS + pl.cdiv(n_assign, MOE_ITEM_ROWS)
    dest, pad_lo, pad_hi, items = _route(top_idx, n_items)
    hn3 = jnp.concatenate([hn_p, hn_s[:n_dec]], axis=0).reshape(n_tok, ROW_PIECES, d // ROW_PIECES)
    xs3 = _dispatch(hn3, dest, pad_lo, pad_hi, n_slots)
    out2 = _moe(xs3, items, w_gu[0], b_gu[0], w_down[0], b_down[0])
    dest_pad = jnp.concatenate([dest, jnp.zeros(((SAMPLE_ROWS - n_dec) * TOP_K,), jnp.int32)])
    h2_p = _combine(dest_pad, h_p, gate_p, out2, 0, COMBINE_TOKENS)
    h2_s = _combine(dest_pad, h_s, gate_s, out2, t_p, SAMPLE_ROWS)

    y_p = _ple(h2_p, p_prompt[0].reshape(t_p, -1), norm_ple, wg_ple_bf, wp_ple_bf, 512)
    y_s = _ple(h2_s, _pad_rows(p_sample[0].reshape(n_dec, -1), SAMPLE_ROWS), norm_ple,
               wg_ple_bf, wp_ple_bf, SAMPLE_ROWS)

    kv = lambda a, n, l: a.reshape(1, n, l, MOBA_HEADS, MOBA_HEAD_DIM)
    return (y_p.reshape(batch, seq_len, d), y_s[:n_dec].reshape(n_dec, 1, d),
            kv(z_p[:, off[5]:off[6]], batch, seq_len), kv(z_p[:, off[6]:], batch, seq_len),
            state_p[None], kv(mk_s, n_dec, 1), kv(mv_s, n_dec, 1), state_s[None])
```

```python
import functools
import math

import jax
import jax.numpy as jnp
from jax import lax
from jax.experimental import pallas as pl
from jax.experimental.pallas import tpu as pltpu

F32 = jnp.float32
BF16 = jnp.bfloat16

RET_HEADS = 4
RET_DK = 256
RET_DV = 256
RET_ROPE_THETA = 10000.0
MOBA_HEADS = 8
MOBA_HEAD_DIM = 128
MOBA_BLOCK = 256
MOBA_TOPK = 3
ROPE_THETA = 500000.0
ROPE_DIM = MOBA_HEAD_DIM // 4
N_EXPERTS = 32
TOP_K = 4
SWIGLU_LIMIT = 7.0
SWIGLU_ALPHA = 1.702
EPS = 1e-6

LANES = 128
SUBLANES = 8
VMEM_LIMIT_BYTES = 56 * 1024 * 1024

SAMPLE_ROWS = 16
RET_CHUNK = 256
MOE_ROW_BLOCK = 256
MOE_ITEM_ROWS = 2048
MOE_FF_TILE = 256
ROW_PIECES = 16
COMBINE_TOKENS = 128
MOBA_KV_GROUP = 4


def _cparams(semantics, vmem=VMEM_LIMIT_BYTES):
    return pltpu.CompilerParams(dimension_semantics=semantics, vmem_limit_bytes=vmem)


def _nt_dot(a, b, **kw):
    return lax.dot_general(a, b, (((1,), (1,)), ((), ())), preferred_element_type=F32, **kw)


def _rms(x):
    return x * lax.rsqrt(jnp.mean(x * x, axis=-1, keepdims=True) + EPS)


def _rnd(a):
    return a.astype(BF16).astype(F32)


_IN_TN = 256


def _inproj_kernel(x_ref, g_ref, w_ref, cr_ref, sr_ref, cm_ref, sa_ref, sb_ref, qn_ref, kn_ref,
                   z_ref, xn_ref):
    j = pl.program_id(1)

    @pl.when(j == 0)
    def _():
        xn_ref[...] = (_rms(x_ref[...]) * g_ref[...]).astype(BF16)

    acc = jnp.dot(xn_ref[...], w_ref[...], preferred_element_type=F32)
    @pl.when(j < 8)
    def _():
        half = RET_DK // 2
        x1 = acc[:, :half]
        x2 = acc[:, half:]
        c = cr_ref[...]
        s = sr_ref[...]
        scale = jnp.where(j < 4, 1.0, RET_DK ** -0.5).astype(F32)
        z_ref[:, :half] = (x1 * c - x2 * s) * scale
        z_ref[:, half:] = (x2 * c + x1 * s) * scale

    @pl.when(((j >= 8) & (j < 16)) | (j >= 24))
    def _():
        z_ref[...] = acc

    @pl.when((j >= 16) & (j < 24))
    def _():
        gain = jnp.where(j < 20, qn_ref[...], kn_ref[...])
        for hh in range(_IN_TN // MOBA_HEAD_DIM):
            t = _rms(acc[:, hh * MOBA_HEAD_DIM:(hh + 1) * MOBA_HEAD_DIM]) * gain
            up = pltpu.roll(t, MOBA_HEAD_DIM - ROPE_DIM // 2, 1)
            dn = pltpu.roll(t, ROPE_DIM // 2, 1)
            z_ref[:, hh * MOBA_HEAD_DIM:(hh + 1) * MOBA_HEAD_DIM] = (
                t * cm_ref[...] + up * sa_ref[...] + dn * sb_ref[...])


def _rope_tables(pos):
    posf = pos.astype(F32)[:, None]
    half = RET_DK // 2
    inv = RET_ROPE_THETA ** (-2.0 * jnp.arange(half, dtype=F32) / RET_DK)
    ang = posf * inv[None, :]
    cr, sr = jnp.cos(ang), jnp.sin(ang)
    mh = ROPE_DIM // 2
    inv_m = ROPE_THETA ** (-2.0 * jnp.arange(mh, dtype=F32) / ROPE_DIM)
    ang_m = posf * inv_m[None, :]
    cos_m, sin_m = jnp.cos(ang_m), jnp.sin(ang_m)
    n = pos.shape[0]
    rest = MOBA_HEAD_DIM - ROPE_DIM
    cm = jnp.concatenate([cos_m, cos_m, jnp.ones((n, rest), F32)], axis=1)
    sa = jnp.concatenate([-sin_m, jnp.zeros((n, MOBA_HEAD_DIM - mh), F32)], axis=1)
    sb = jnp.concatenate([jnp.zeros((n, mh), F32), sin_m, jnp.zeros((n, rest), F32)], axis=1)
    return cr, sr, cm, sa, sb


def _inproj(x2d, norm, w_bf, tables, qn, kn, seq_len, tm):
    t, d = x2d.shape
    n_out = w_bf.shape[1]
    tpos = seq_len // tm
    cr, sr, cm, sa, sb = tables
    tab = pl.BlockSpec((tm, LANES), lambda i, j: (i % tpos, 0))
    vec = pl.BlockSpec((1, LANES), lambda i, j: (0, 0))
    return pl.pallas_call(
        _inproj_kernel,
        grid=(t // tm, n_out // _IN_TN),
        in_specs=[
            pl.BlockSpec((tm, d), lambda i, j: (i, 0)),
            pl.BlockSpec((1, d), lambda i, j: (0, 0)),
            pl.BlockSpec((d, _IN_TN), lambda i, j: (0, j)),
            tab, tab, tab, tab, tab, vec, vec,
        ],
        out_specs=pl.BlockSpec((tm, _IN_TN), lambda i, j: (i, j)),
        out_shape=jax.ShapeDtypeStruct((t, n_out), F32),
        scratch_shapes=[pltpu.VMEM((tm, d), BF16)],
        compiler_params=_cparams(("parallel", "arbitrary")),
        name="inproj",
    )(x2d, norm, w_bf, cr, sr, cm, sa, sb, qn, kn)


def _ret_gate(o, g):
    return _rms(o) * (g * jax.nn.sigmoid(g))


def _ret_prompt_kernel(logg_ref, q_ref, k_ref, v_ref, g_ref, o_ref, st_ref, r_ref):
    h = pl.program_id(1)
    c = pl.program_id(2)
    n = q_ref.shape[0]

    @pl.when(c == 0)
    def _():
        r_ref[...] = jnp.zeros_like(r_ref)

    lg = logg_ref[h]
    i = lax.broadcasted_iota(jnp.int32, (n, 1), 0).astype(F32)
    jj = lax.broadcasted_iota(jnp.int32, (1, n), 1).astype(F32)
    diff = i - jj
    inner = jnp.where(diff >= 0, jnp.exp(lg * jnp.maximum(diff, 0.0)), 0.0)
    q_dec = jnp.exp(lg * (i + 1.0))
    k_dec = jnp.exp(lg * (n - 1.0 - i))
    chunk_dec = jnp.exp(lg * jnp.full((1, 1), float(n), F32))

    q = q_ref[...]
    k = k_ref[...]
    vb = v_ref[...].astype(BF16)
    r = r_ref[...]
    s = _nt_dot(q.astype(BF16), k.astype(BF16)) * inner
    o = (jnp.dot(s.astype(BF16), vb, preferred_element_type=F32)
         + jnp.dot((q * q_dec).astype(BF16), r.astype(BF16), preferred_element_type=F32))
    kd_t = (k * k_dec).T.astype(BF16)
    r_new = r * chunk_dec + jnp.dot(kd_t, vb, preferred_element_type=F32)
    r_ref[...] = r_new
    o_ref[...] = _ret_gate(o, g_ref[...])

    @pl.when(c == pl.num_programs(2) - 1)
    def _():
        st_ref[0, 0] = r_new


def _ret_log_decay():
    return jnp.log1p(-jnp.exp2(-5.0 - jnp.arange(RET_HEADS, dtype=F32)))


def _ret_prompt(z, batch, seq_len):
    nc = seq_len // RET_CHUNK
    w = RET_DK

    def col(off):
        return pl.BlockSpec((RET_CHUNK, w), lambda b, h, c: (b * nc + c, off + h))

    return pl.pallas_call(
        _ret_prompt_kernel,
        grid=(batch, RET_HEADS, nc),
        in_specs=[pl.BlockSpec(memory_space=pltpu.SMEM),
                  col(0), col(RET_HEADS), col(2 * RET_HEADS), col(3 * RET_HEADS)],
        out_specs=[
            pl.BlockSpec((RET_CHUNK, RET_DV), lambda b, h, c: (b * nc + c, h)),
            pl.BlockSpec((1, 1, RET_DK, RET_DV), lambda b, h, c: (b, h, 0, 0)),
        ],
        scratch_shapes=[pltpu.VMEM((RET_DK, RET_DV), F32)],
        out_shape=[
            jax.ShapeDtypeStruct((batch * seq_len, RET_HEADS * RET_DV), F32),
            jax.ShapeDtypeStruct((batch, RET_HEADS, RET_DK, RET_DV), F32),
        ],
        compiler_params=_cparams(("parallel", "parallel", "arbitrary")),
        name="ret_prompt",
    )(_ret_log_decay(), z, z, z, z)


def _ret_sample_kernel(logg_ref, q_ref, kc_ref, kr_ref, v_ref, g_ref, s0_ref, o_ref, st_ref):
    h = pl.program_id(1)
    dec = jnp.exp(logg_ref[h] * jnp.ones((1, 1), F32))
    q = q_ref[0, 0]
    v = _rnd(v_ref[0, 0])
    r0 = s0_ref[0, 0]
    qk = jnp.sum(_rnd(q) * _rnd(kr_ref[0, 0]), axis=-1, keepdims=True)
    q8 = jnp.broadcast_to(q * dec, (SUBLANES, RET_DK)).astype(BF16)
    qr = jnp.dot(q8, r0.astype(BF16), preferred_element_type=F32)[0:1]
    o = _rnd(qk) * v + qr
    st_ref[0, 0] = r0 * dec + _rnd(kc_ref[0, 0]) * v
    o_ref[0, 0] = _ret_gate(o, g_ref[0, 0])


def _ret_sample(q, k, v, g, state):
    n = q.shape[0]
    row = lambda a: a.reshape(n, RET_HEADS, 1, RET_DK)
    rspec = pl.BlockSpec((1, 1, 1, RET_DK), lambda b, h: (b, h, 0, 0))
    mspec = pl.BlockSpec((1, 1, RET_DK, RET_DV), lambda b, h: (b, h, 0, 0))
    o, st = pl.pallas_call(
        _ret_sample_kernel,
        grid=(n, RET_HEADS),
        in_specs=[pl.BlockSpec(memory_space=pltpu.SMEM),
                  rspec, pl.BlockSpec((1, 1, RET_DK, 1), lambda b, h: (b, h, 0, 0)),
                  rspec, rspec, rspec, mspec],
        out_specs=[rspec, mspec],
        out_shape=[jax.ShapeDtypeStruct((n, RET_HEADS, 1, RET_DV), F32),
                   jax.ShapeDtypeStruct(state.shape, F32)],
        compiler_params=_cparams(("parallel", "parallel")),
        name="ret_sample",
    )(_ret_log_decay(), row(q), k.reshape(n, RET_HEADS, RET_DK, 1), row(k), row(v), row(g), state)
    return o.reshape(n, RET_HEADS * RET_DV), st


def _moba_prompt_kernel(q_ref, k_ref, v_ref, o_ref, kb_ref, vt_ref, km_ref, sel_ref):
    qi = pl.program_id(2)
    blk = MOBA_BLOCK
    nb = k_ref.shape[0] // blk
    scale = MOBA_HEAD_DIM ** -0.5

    @pl.when(qi == 0)
    def _():
        kb_ref[...] = k_ref[...].astype(BF16)
        for n in range(nb):
            rows = slice(n * blk, (n + 1) * blk)
            vt_ref[:, rows] = v_ref[rows, :].T.astype(BF16)
            km_ref[n:n + 1, :] = jnp.mean(k_ref[rows, :], axis=0, keepdims=True)

    qb = q_ref[...].astype(BF16)
    gate = _nt_dot(km_ref[...].astype(BF16), qb)
    kblock = lax.broadcasted_iota(jnp.int32, (nb, blk), 0)
    past = kblock < qi
    g = jnp.where(past, gate, -jnp.inf)
    cnt = jnp.zeros((nb, blk), F32)
    for m in range(nb):
        gm = g[m:m + 1, :]
        cnt = cnt + jnp.where(gm > g, 1.0, jnp.where((gm == g) & (kblock > m), 1.0, 0.0))
    sel_ref[...] = jnp.where(past & (cnt < MOBA_TOPK), 1.0, 0.0)

    own = pl.ds(pl.multiple_of(qi * blk, blk), blk)
    s = _nt_dot(kb_ref[own, :], qb) * scale
    key = lax.broadcasted_iota(jnp.int32, (blk, blk), 0)
    qry = lax.broadcasted_iota(jnp.int32, (blk, blk), 1)
    s = jnp.where(key <= qry, s, -jnp.inf)
    m0 = jnp.max(s, axis=0, keepdims=True)
    p = jnp.exp(s - m0)
    l0 = jnp.sum(p, axis=0, keepdims=True)
    a0 = jnp.dot(vt_ref[:, own], p.astype(BF16), preferred_element_type=F32)

    grp = MOBA_KV_GROUP
    span = grp * blk

    def body(j, carry):
        m_i, l_i, a_i = carry
        rows = pl.ds(pl.multiple_of(j * span, span), span)
        sn = _nt_dot(kb_ref[rows, :], qb) * scale
        allowed = jnp.concatenate(
            [jnp.broadcast_to(sel_ref[pl.ds(j * grp + t, 1), :], (blk, blk)) for t in range(grp)],
            axis=0)
        sn = jnp.where(allowed > 0.0, sn, -jnp.inf)
        m_new = jnp.maximum(m_i, jnp.max(sn, axis=0, keepdims=True))
        alpha = jnp.exp(m_i - m_new)
        pn = jnp.exp(sn - m_new)
        l_new = alpha * l_i + jnp.sum(pn, axis=0, keepdims=True)
        a_new = alpha * a_i + jnp.dot(vt_ref[:, rows], pn.astype(BF16), preferred_element_type=F32)
        return m_new, l_new, a_new

    _, l_f, a_f = lax.fori_loop(0, (qi + grp - 1) // grp, body, (m0, l0, a0))
    o_ref[...] = (a_f / l_f).T


def _moba_prompt(z, batch, seq_len):
    nq = seq_len // MOBA_BLOCK
    hd = MOBA_HEAD_DIM
    qoff = (2 * RET_HEADS * RET_DK + 2 * RET_HEADS * RET_DV) // hd
    koff = qoff + MOBA_HEADS
    voff = koff + MOBA_HEADS
    full = lambda off: pl.BlockSpec((seq_len, hd), lambda b, h, i: (b, off + h))
    return pl.pallas_call(
        _moba_prompt_kernel,
        grid=(batch, MOBA_HEADS, nq),
        in_specs=[pl.BlockSpec((MOBA_BLOCK, hd), lambda b, h, i: (b * nq + i, qoff + h)),
                  full(koff), full(voff)],
        out_specs=pl.BlockSpec((MOBA_BLOCK, hd), lambda b, h, i: (b * nq + i, h)),
        out_shape=jax.ShapeDtypeStruct((batch * seq_len, MOBA_HEADS * hd), F32),
        scratch_shapes=[pltpu.VMEM((seq_len, hd), BF16), pltpu.VMEM((hd, seq_len), BF16),
                        pltpu.VMEM((nq, hd), F32), pltpu.VMEM((nq, MOBA_BLOCK), F32)],
        compiler_params=_cparams(("parallel", "parallel", "arbitrary")),
        name="moba_prompt",
    )(z, z, z)


MOBA_SCAN_PAGES = 8


def _moba_scan_kernel(pt_ref, q_ref, *refs):
    k_refs = refs[:MOBA_SCAN_PAGES]
    sel_ref, gate_ref = refs[MOBA_SCAN_PAGES:]
    p = pl.program_id(1)
    ppb = MOBA_BLOCK // k_refs[0].shape[1]
    blocks_per_step = MOBA_SCAN_PAGES // ppb

    @pl.when(p == 0)
    def _():
        gate_ref[...] = jnp.zeros_like(gate_ref)

    q = _rnd(q_ref[0])
    lane = lax.broadcasted_iota(jnp.int32, gate_ref.shape, 1)
    gate = gate_ref[...]
    for jb in range(blocks_per_step):
        k_sum = sum(jnp.sum(k_refs[jb * ppb + j][0], axis=0) for j in range(ppb))
        g = jnp.sum(q * _rnd(k_sum * (1.0 / MOBA_BLOCK)), axis=1, keepdims=True)
        gate = jnp.where(lane == p * blocks_per_step + jb, g, gate)
    gate_ref[...] = gate

    @pl.when(p == pl.num_programs(1) - 1)
    def _():
        g = jnp.where(lane < pl.num_programs(1) * blocks_per_step, gate, -jnp.inf)
        out = jnp.zeros(gate_ref.shape, jnp.int32)
        for t in range(MOBA_TOPK):
            mx = jnp.max(g, axis=1, keepdims=True)
            idx = jnp.min(jnp.where(g == mx, lane, LANES), axis=1, keepdims=True)
            out = jnp.where(lane == t, idx, out)
            g = jnp.where(lane == idx, -jnp.inf, g)
        sel_ref[0] = out


def _moba_attend_kernel(pt_ref, selp_ref, q_ref, kn_ref, vn_ref, *refs):
    nsrc = (len(refs) - 1) // 2
    k_refs = refs[:nsrc]
    v_refs = refs[nsrc:2 * nsrc]
    o_ref = refs[2 * nsrc]
    h = pl.program_id(1)
    scale = MOBA_HEAD_DIM ** -0.5
    mine2 = lax.broadcasted_iota(jnp.int32, (MOBA_HEADS, MOBA_HEAD_DIM), 0) == h
    mine3 = lax.broadcasted_iota(jnp.int32, (1, MOBA_HEADS, 1), 1) == h
    all2 = lambda a, op: op(op(a, axis=1, keepdims=True), axis=0, keepdims=True)
    all3 = lambda a, op: op(op(a, axis=0, keepdims=True), axis=1, keepdims=True)
    q = _rnd(q_ref[0])
    s_new = all2(jnp.where(mine2, q * _rnd(kn_ref[0]), 0.0), jnp.sum) * scale
    ss = [jnp.where(mine3, jnp.sum(_rnd(kr[0]) * q[None], axis=-1, keepdims=True) * scale, -jnp.inf)
          for kr in k_refs]
    mx = s_new.reshape(1, 1, 1)
    for sj in ss:
        mx = jnp.maximum(mx, all3(sj, jnp.max))
    p_new = jnp.exp(s_new.reshape(1, 1, 1) - mx)
    ps = [jnp.exp(sj - mx) for sj in ss]
    den = p_new
    for pj in ps:
        den = den + all3(pj, jnp.sum)
    inv = 1.0 / den
    acc = jnp.where(mine2, _rnd(p_new * inv).reshape(1, 1) * _rnd(vn_ref[0]), 0.0)
    for pj, vr in zip(ps, v_refs):
        acc = acc + jnp.sum(_rnd(pj * inv) * _rnd(vr[0]), axis=0)
    o_ref[0, 0] = jnp.sum(acc, axis=0, keepdims=True)


def _moba_sample(q, k_new, v_new, cache_k, cache_v, page_table):
    n, npages = page_table.shape
    page = cache_k.shape[1]
    hd = MOBA_HEAD_DIM
    ppb = MOBA_BLOCK // page
    pp = MOBA_SCAN_PAGES
    assert npages % pp == 0 and npages // ppb <= LANES
    page_block = (1, page, MOBA_HEADS, hd)
    tok2 = pl.BlockSpec((1, MOBA_HEADS, hd), lambda b, p, pt: (b, 0, 0))
    scan_specs = [pl.BlockSpec(page_block, (lambda b, p, pt, j=j: (pt[b, p * pp + j], 0, 0, 0)))
                  for j in range(pp)]
    sel = pl.pallas_call(
        _moba_scan_kernel,
        grid_spec=pltpu.PrefetchScalarGridSpec(
            num_scalar_prefetch=1,
            grid=(n, npages // pp),
            in_specs=[tok2] + scan_specs,
            out_specs=pl.BlockSpec((1, MOBA_HEADS, LANES), lambda b, p, pt: (b, 0, 0)),
            scratch_shapes=[pltpu.VMEM((MOBA_HEADS, LANES), F32)],
        ),
        out_shape=jax.ShapeDtypeStruct((n, MOBA_HEADS, LANES), jnp.int32),
        compiler_params=_cparams(("parallel", "arbitrary")),
        name="moba_scan",
    )(page_table, q, *([cache_k] * pp))
    selp = (sel[:, :, :MOBA_TOPK, None] * ppb + jnp.arange(ppb, dtype=jnp.int32)).reshape(-1)
    nsrc = MOBA_TOPK * ppb

    def page_spec(j):
        def index(b, h, pt, sp):
            return (pt[b, sp[(b * MOBA_HEADS + h) * nsrc + j]], 0, 0, 0)
        return pl.BlockSpec(page_block, index)

    tok = pl.BlockSpec((1, MOBA_HEADS, hd), lambda b, h, pt, sp: (b, 0, 0))
    pages = [page_spec(j) for j in range(nsrc)]
    out = pl.pallas_call(
        _moba_attend_kernel,
        grid_spec=pltpu.PrefetchScalarGridSpec(
            num_scalar_prefetch=2,
            grid=(n, MOBA_HEADS),
            in_specs=[tok, tok, tok] + pages + pages,
            out_specs=pl.BlockSpec((1, 1, 1, hd), lambda b, h, pt, sp: (b, h, 0, 0)),
        ),
        out_shape=jax.ShapeDtypeStruct((n, MOBA_HEADS, 1, hd), F32),
        compiler_params=_cparams(("parallel", "parallel")),
        name="moba_attend",
    )(page_table, selp, q, k_new, v_new, *([cache_k] * nsrc), *([cache_v] * nsrc))
    return out.reshape(n, MOBA_HEADS * hd)


def _outproj_kernel(x_ref, ret_ref, moba_ref, wo_ref, nf_ref, wr_ref, br_ref,
                    h_ref, hn_ref, idx_ref, gate_ref):
    rw = ret_ref.shape[1]
    h = (x_ref[...]
         + jnp.dot(ret_ref[...].astype(BF16), wo_ref[:rw, :], preferred_element_type=F32)
         + jnp.dot(moba_ref[...].astype(BF16), wo_ref[rw:, :], preferred_element_type=F32))
    h_ref[...] = h
    hn = _rms(h) * nf_ref[...]
    hn_ref[...] = hn
    logits = jnp.dot(hn.astype(BF16), wr_ref[...], preferred_element_type=F32) + br_ref[...]
    lane = lax.broadcasted_iota(jnp.int32, logits.shape, 1)
    vals, idxs = [], []
    for _ in range(TOP_K):
        mx = jnp.max(logits, axis=1, keepdims=True)
        ix = jnp.min(jnp.where(logits == mx, lane, LANES), axis=1, keepdims=True)
        vals.append(mx)
        idxs.append(ix)
        logits = jnp.where(lane == ix, -jnp.inf, logits)
    es = [jnp.exp(v - vals[0]) for v in vals]
    den = es[0] + es[1] + es[2] + es[3]
    idx_out = jnp.zeros(lane.shape, jnp.int32)
    gate_out = jnp.zeros(lane.shape, F32)
    for t in range(TOP_K):
        idx_out = jnp.where(lane == t, idxs[t], idx_out)
        gate_out = jnp.where(lane == t, es[t] / den, gate_out)
    idx_ref[...] = idx_out
    gate_ref[...] = gate_out


def _outproj(x2d, ret, moba, wo_bf, norm_ffn, wr_pad, br_pad, tm):
    t, d = x2d.shape
    rw, mw = ret.shape[1], moba.shape[1]
    const = lambda shape: pl.BlockSpec(shape, lambda i: (0, 0))
    rows = lambda w: pl.BlockSpec((tm, w), lambda i: (i, 0))
    return pl.pallas_call(
        _outproj_kernel,
        grid=(t // tm,),
        in_specs=[rows(d), rows(rw), rows(mw), const((rw + mw, d)), const((1, d)),
                  const((d, LANES)), const((1, LANES))],
        out_specs=[rows(d), rows(d), rows(LANES), rows(LANES)],
        out_shape=[jax.ShapeDtypeStruct((t, d), F32), jax.ShapeDtypeStruct((t, d), F32),
                   jax.ShapeDtypeStruct((t, LANES), jnp.int32),
                   jax.ShapeDtypeStruct((t, LANES), F32)],
        compiler_params=_cparams(("parallel",)),
        name="outproj",
    )(x2d, ret, moba, wo_bf, norm_ffn, wr_pad, br_pad)


def _dispatch_kernel(dest_ref, pad_lo_ref, pad_hi_ref, hn_ref, xs_hbm, zero_ref, sem, zsem, *,
                     ntok):
    i = pl.program_id(0)
    tb = hn_ref.shape[0]
    first = i * tb
    count = jnp.minimum(tb, ntok - first)

    def row_copy(r, k):
        return pltpu.make_async_copy(hn_ref.at[r], xs_hbm.at[dest_ref[(first + r) * TOP_K + k]], sem)

    def issue(r, c):
        for k in range(TOP_K):
            row_copy(r, k).start()
        return c

    lax.fori_loop(0, count, issue, 0)

    @pl.when(i == 0)
    def _():
        zero_ref[...] = jnp.zeros_like(zero_ref)

        def pad_copy(s):
            return pltpu.make_async_copy(zero_ref.at[0], xs_hbm.at[s], zsem)

        def per_expert(e, c):
            def pad_row(s, c2):
                pad_copy(s).start()
                return c2
            lax.fori_loop(pad_lo_ref[e], pad_hi_ref[e], pad_row, 0)
            return c

        lax.fori_loop(0, N_EXPERTS, per_expert, 0)

        rb = zero_ref.shape[0]
        tail0 = pad_hi_ref[N_EXPERTS - 1]
        n_tail = (xs_hbm.shape[0] - tail0) // rb

        def tail_copy(c):
            rows = pl.ds(pl.multiple_of(tail0 + c * rb, rb), rb)
            return pltpu.make_async_copy(zero_ref, xs_hbm.at[rows], zsem)

        def tail_start(c, c2):
            tail_copy(c).start()
            return c2

        lax.fori_loop(0, n_tail, tail_start, 0)

        def per_expert_wait(e, c):
            def pad_wait(s, c2):
                pad_copy(s).wait()
                return c2
            lax.fori_loop(pad_lo_ref[e], pad_hi_ref[e], pad_wait, 0)
            return c

        lax.fori_loop(0, N_EXPERTS, per_expert_wait, 0)

        def tail_wait(c, c2):
            tail_copy(c).wait()
            return c2

        lax.fori_loop(0, n_tail, tail_wait, 0)

    def drain(r, c):
        for k in range(TOP_K):
            row_copy(r, k).wait()
        return c

    lax.fori_loop(0, count, drain, 0)


def _dispatch(hn3, dest_flat, pad_lo, pad_hi, n_slots):
    ntok = hn3.shape[0]
    tb = COMBINE_TOKENS
    return pl.pallas_call(
        functools.partial(_dispatch_kernel, ntok=ntok),
        grid_spec=pltpu.PrefetchScalarGridSpec(
            num_scalar_prefetch=3,
            grid=(pl.cdiv(ntok, tb),),
            in_specs=[pl.BlockSpec((tb,) + hn3.shape[1:], lambda i, d, lo, hi: (i, 0, 0))],
            out_specs=pl.BlockSpec(memory_space=pl.ANY),
            scratch_shapes=[pltpu.VMEM((MOE_ROW_BLOCK,) + hn3.shape[1:], F32),
                            pltpu.SemaphoreType.DMA(()), pltpu.SemaphoreType.DMA(())],
        ),
        out_shape=jax.ShapeDtypeStruct((n_slots,) + hn3.shape[1:], F32),
        compiler_params=_cparams(("arbitrary",)),
        name="moe_dispatch",
    )(dest_flat, pad_lo, pad_hi, hn3)


def _moe_kernel(item_e, item_row0, item_nch, tail_ref, xs_hbm, wg_ref, wu_ref, bg_ref, bu_ref,
                wd_ref, bd_ref, out_hbm, x_ref, acc_ref, stage_ref, wgb_ref, wub_ref, wdb_ref,
                sem_in, sem_out):
    it = pl.program_id(0)
    f = pl.program_id(1)
    nf = pl.num_programs(1)
    rb = MOE_ROW_BLOCK
    pieces = ROW_PIECES
    nch = item_nch[it]
    row0 = item_row0[it]

    def chunk_rows(r):
        return pl.ds(pl.multiple_of((row0 + r * rb) * pieces, rb * pieces), rb * pieces)

    def in_copy(r, slot):
        return pltpu.make_async_copy(xs_hbm.at[chunk_rows(r)], stage_ref.at[slot], sem_in.at[slot])

    def out_copy(r, slot):
        return pltpu.make_async_copy(stage_ref.at[slot], out_hbm.at[chunk_rows(r)], sem_out.at[slot])

    @pl.when((f == 0) & (nch > 0))
    def _():
        in_copy(0, 0).start()

        def load(r, c):
            slot = r % 2

            @pl.when(r + 1 < nch)
            def _():
                in_copy(r + 1, 1 - slot).start()

            in_copy(r, slot).wait()
            rows = pl.ds(pl.multiple_of(r * rb, rb), rb)
            for p in range(pieces):
                piece = stage_ref[slot, pl.ds(p, rb, stride=pieces), :]
                x_ref[rows, p * LANES:(p + 1) * LANES] = piece.astype(BF16)
            acc_ref[rows, :] = jnp.broadcast_to(bd_ref[0], (rb, acc_ref.shape[1]))
            return c

        lax.fori_loop(0, nch, load, 0)

    @pl.when(nch > 0)
    def _():
        wgb_ref[...] = wg_ref[0].astype(BF16)
        wub_ref[...] = wu_ref[0].astype(BF16)
        wdb_ref[...] = wd_ref[0].astype(BF16)
        bg = bg_ref[0]
        bu = bu_ref[0]

        def compute(r, c):
            rows = pl.ds(pl.multiple_of(r * rb, rb), rb)
            x = x_ref[rows, :]
            g = jnp.dot(x, wgb_ref[...], preferred_element_type=F32) + bg
            u = jnp.dot(x, wub_ref[...], preferred_element_type=F32) + bu
            g = jnp.minimum(g, SWIGLU_LIMIT)
            u = jnp.clip(u, -SWIGLU_LIMIT, SWIGLU_LIMIT)
            act = (u + 1.0) * g * jax.nn.sigmoid(SWIGLU_ALPHA * g)
            acc_ref[rows, :] += jnp.dot(act.astype(BF16), wdb_ref[...], preferred_element_type=F32)
            return c

        lax.fori_loop(0, nch, compute, 0)

    @pl.when((f == nf - 1) & (nch > 0))
    def _():
        def store(r, c):
            slot = r % 2

            @pl.when(r >= 2)
            def _():
                out_copy(r - 2, slot).wait()

            rows = pl.ds(pl.multiple_of(r * rb, rb), rb)
            for p in range(pieces):
                stage_ref[slot, pl.ds(p, rb, stride=pieces), :] = acc_ref[rows, p * LANES:(p + 1) * LANES]
            out_copy(r, slot).start()
            return c

        lax.fori_loop(0, nch, store, 0)

        @pl.when(nch >= 2)
        def _():
            out_copy(nch - 2, nch % 2).wait()

        out_copy(nch - 1, (nch - 1) % 2).wait()

    @pl.when((it == pl.num_programs(0) - 1) & (f == nf - 1))
    def _():
        tail0 = tail_ref[0]
        n_tail = (out_hbm.shape[0] // pieces - tail0) // rb
        stage_ref[0] = jnp.zeros(stage_ref.shape[1:], F32)

        def tail_copy(c):
            rows = pl.ds(pl.multiple_of((tail0 + c * rb) * pieces, rb * pieces), rb * pieces)
            return pltpu.make_async_copy(stage_ref.at[0], out_hbm.at[rows], sem_out.at[0])

        def tail_start(c, c2):
            tail_copy(c).start()
            return c2

        def tail_wait(c, c2):
            tail_copy(c).wait()
            return c2

        lax.fori_loop(0, n_tail, tail_start, 0)
        lax.fori_loop(0, n_tail, tail_wait, 0)


def _moe(xs3, items, w_gu, b_gu, w_down, b_down):
    item_e, item_row0, item_nch, tail0 = items
    n_items = item_e.shape[0]
    n_slots = xs3.shape[0]
    d = w_gu.shape[1]
    ff = w_down.shape[1]
    tf = MOE_FF_TILE
    nf = ff // tf
    xs2 = xs3.reshape(n_slots * ROW_PIECES, LANES)

    def ftile(it, f, nch):
        return jnp.where(nch[it] > 0, f, nf - 1)

    wg_spec = pl.BlockSpec((1, d, tf), lambda it, f, e, r0, nch, t0: (e[it], 0, ftile(it, f, nch)))
    wu_spec = pl.BlockSpec((1, d, tf), lambda it, f, e, r0, nch, t0: (e[it], 0, nf + ftile(it, f, nch)))
    bg_spec = pl.BlockSpec((1, 1, tf), lambda it, f, e, r0, nch, t0: (e[it], 0, ftile(it, f, nch)))
    bu_spec = pl.BlockSpec((1, 1, tf), lambda it, f, e, r0, nch, t0: (e[it], 0, nf + ftile(it, f, nch)))
    wd_spec = pl.BlockSpec((1, tf, d), lambda it, f, e, r0, nch, t0: (e[it], ftile(it, f, nch), 0))
    bd_spec = pl.BlockSpec((1, 1, d), lambda it, f, e, r0, nch, t0: (e[it], 0, 0))
    b_gu3 = b_gu.reshape(N_EXPERTS, 1, 2 * ff)
    b_down3 = b_down.reshape(N_EXPERTS, 1, d)
    out2 = pl.pallas_call(
        _moe_kernel,
        grid_spec=pltpu.PrefetchScalarGridSpec(
            num_scalar_prefetch=4,
            grid=(n_items, nf),
            in_specs=[pl.BlockSpec(memory_space=pl.ANY), wg_spec, wu_spec, bg_spec, bu_spec,
                      wd_spec, bd_spec],
            out_specs=pl.BlockSpec(memory_space=pl.ANY),
            scratch_shapes=[
                pltpu.VMEM((MOE_ITEM_ROWS, d), BF16),
                pltpu.VMEM((MOE_ITEM_ROWS, d), F32),
                pltpu.VMEM((2, MOE_ROW_BLOCK * ROW_PIECES, LANES), F32),
                pltpu.VMEM((d, tf), BF16), pltpu.VMEM((d, tf), BF16), pltpu.VMEM((tf, d), BF16),
                pltpu.SemaphoreType.DMA((2,)), pltpu.SemaphoreType.DMA((2,)),
            ],
        ),
        out_shape=jax.ShapeDtypeStruct(xs2.shape, F32),
        compiler_params=_cparams(("arbitrary", "arbitrary")),
        name="moe_experts",
    )(item_e, item_row0, item_nch, tail0, xs2, w_gu, w_gu, b_gu3, b_gu3, w_down, b_down3)
    return out2


def _combine_kernel(dest_ref, h_ref, gate_ref, out_hbm, o_ref, buf_ref, sem, *, tok_off):
    i = pl.program_id(0)
    tb = h_ref.shape[0]
    pieces = ROW_PIECES
    base = (tok_off + i * tb) * TOP_K

    def row_copy(r, k):
        d = dest_ref[base + r * TOP_K + k]
        src = out_hbm.at[pl.ds(pl.multiple_of(d * pieces, pieces), pieces)]
        dst = buf_ref.at[pl.ds(pl.multiple_of((k * tb + r) * pieces, pieces), pieces)]
        return pltpu.make_async_copy(src, dst, sem)

    def issue(r, c):
        for k in range(TOP_K):
            row_copy(r, k).start()
        return c

    lax.fori_loop(0, tb, issue, 0)

    def drain(r, c):
        for k in range(TOP_K):
            row_copy(r, k).wait()
        return c

    lax.fori_loop(0, tb, drain, 0)

    for p in range(pieces):
        f = jnp.zeros((tb, LANES), F32)
        for k in range(TOP_K):
            rows = buf_ref[pl.ds(k * tb * pieces + p, tb, stride=pieces), :]
            f = f + gate_ref[:, k:k + 1] * rows
        o_ref[:, p * LANES:(p + 1) * LANES] = h_ref[:, p * LANES:(p + 1) * LANES] + f


def _combine(dest_flat, h, gates, out2, tok_off, tb):
    t, d = h.shape
    return pl.pallas_call(
        functools.partial(_combine_kernel, tok_off=tok_off),
        grid_spec=pltpu.PrefetchScalarGridSpec(
            num_scalar_prefetch=1,
            grid=(t // tb,),
            in_specs=[pl.BlockSpec((tb, d), lambda i, ds: (i, 0)),
                      pl.BlockSpec((tb, LANES), lambda i, ds: (i, 0)),
                      pl.BlockSpec(memory_space=pl.ANY)],
            out_specs=pl.BlockSpec((tb, d), lambda i, ds: (i, 0)),
            scratch_shapes=[pltpu.VMEM((TOP_K * tb * ROW_PIECES, LANES), F32),
                            pltpu.SemaphoreType.DMA(())],
        ),
        out_shape=jax.ShapeDtypeStruct((t, d), F32),
        compiler_params=_cparams(("arbitrary",)),
        name="moe_combine",
    )(dest_flat, h, gates, out2)


def _ple_kernel(h_ref, p_ref, n_ref, wg_ref, wp_ref, y_ref):
    h = h_ref[...]
    hn = (_rms(h) * n_ref[...]).astype(BF16)
    gate = jax.nn.sigmoid(jnp.dot(hn, wg_ref[...], preferred_element_type=F32))
    proj = jnp.dot(p_ref[...].astype(BF16), wp_ref[...], preferred_element_type=F32)
    y_ref[...] = h + gate * proj


def _ple(h, p, norm, wg_bf, wp_bf, tm):
    t, d = h.shape
    pd = p.shape[1]
    const = lambda shape: pl.BlockSpec(shape, lambda i: (0, 0))
    return pl.pallas_call(
        _ple_kernel,
        grid=(t // tm,),
        in_specs=[pl.BlockSpec((tm, d), lambda i: (i, 0)), pl.BlockSpec((tm, pd), lambda i: (i, 0)),
                  const((1, d)), const((d, d)), const((pd, d))],
        out_specs=pl.BlockSpec((tm, d), lambda i: (i, 0)),
        out_shape=jax.ShapeDtypeStruct((t, d), F32),
        compiler_params=_cparams(("parallel",)),
        name="ple",
    )(h, p, norm, wg_bf, wp_bf)


def _route(top_idx, n_items):
    rb = MOE_ROW_BLOCK
    flat = top_idx.reshape(-1)
    onehot = (flat[:, None] == jnp.arange(N_EXPERTS, dtype=jnp.int32)[None, :]).astype(jnp.int32)
    csum = jnp.cumsum(onehot, axis=0)
    rank = jnp.sum(onehot * (csum - 1), axis=1)
    counts = csum[-1]
    padded = (counts + rb - 1) // rb * rb
    seg_end = jnp.cumsum(padded)
    seg_start = seg_end - padded
    dest = (jnp.sum(onehot * seg_start[None, :], axis=1) + rank).astype(jnp.int32)
    pad_lo = (seg_start + counts).astype(jnp.int32)
    pad_hi = seg_end.astype(jnp.int32)
    per = (padded + MOE_ITEM_ROWS - 1) // MOE_ITEM_ROWS
    item_end = jnp.cumsum(per)
    item_start = item_end - per
    ids = jnp.arange(n_items, dtype=jnp.int32)
    e = jnp.minimum(jnp.searchsorted(item_end, ids, side="right"), N_EXPERTS - 1).astype(jnp.int32)
    valid = ids < item_end[-1]
    piece = ids - item_start[e]
    row0 = seg_start[e] + piece * MOE_ITEM_ROWS
    rows = jnp.clip(padded[e] - piece * MOE_ITEM_ROWS, 0, MOE_ITEM_ROWS)
    nch = jnp.where(valid, rows // rb, 0).astype(jnp.int32)
    last_e = e[jnp.maximum(item_end[-1] - 1, 0)]
    item_e = jnp.where(valid, e, last_e).astype(jnp.int32)
    item_row0 = jnp.where(valid, row0, 0).astype(jnp.int32)
    return dest, pad_lo, pad_hi, (item_e, item_row0, nch, pad_hi[-1:])


def _pad_rows(a, rows):
    return jnp.concatenate([a, jnp.zeros((rows - a.shape[0],) + a.shape[1:], a.dtype)], axis=0)


def kernel(x_prompt, x_sample, cache_k, cache_v, state_ret, page_table, p_prompt, p_sample,
           norm_mix, w_in, q_norm, k_norm, w_o, norm_ffn, w_router, b_router, w_gu, b_gu,
           w_down, b_down, norm_ple, w_ple_gate, w_ple_proj):
    depth = norm_mix.shape[0]
    assert depth == 1
    batch, seq_len, d = x_prompt.shape
    n_dec, dec_seq, _ = x_sample.shape
    assert dec_seq == 1 and n_dec <= SAMPLE_ROWS
    past_len = page_table.shape[1] * cache_k.shape[2]
    ret_w = RET_HEADS * RET_DK
    moba_w = MOBA_HEADS * MOBA_HEAD_DIM
    off = [0, ret_w, 2 * ret_w, 3 * ret_w, 4 * ret_w, 4 * ret_w + moba_w, 4 * ret_w + 2 * moba_w]

    w_in_bf = w_in[0].astype(BF16)
    w_o_bf = w_o[0].astype(BF16)
    wg_ple_bf = w_ple_gate[0].astype(BF16)
    wp_ple_bf = w_ple_proj[0].astype(BF16)
    wr_pad = jnp.concatenate([w_router[0], jnp.zeros((d, LANES - N_EXPERTS), F32)],
                             axis=1).astype(BF16)
    br_pad = jnp.concatenate([b_router[0], jnp.full((LANES - N_EXPERTS,), -jnp.inf, F32)])[None, :]
    qn, kn = q_norm, k_norm

    t_p = batch * seq_len
    xp = x_prompt.reshape(t_p, d)
    tm = 1024
    tables_p = _rope_tables(jnp.arange(seq_len, dtype=jnp.int32))
    z_p = _inproj(xp, norm_mix, w_in_bf, tables_p, qn, kn, seq_len, tm)
    ret_p, state_p = _ret_prompt(z_p, batch, seq_len)
    moba_p = _moba_prompt(z_p, batch, seq_len)

    xs_rows = _pad_rows(x_sample.reshape(n_dec, d), SAMPLE_ROWS)
    tables_s = _rope_tables(jnp.full((SAMPLE_ROWS,), past_len, jnp.int32))
    z_s = _inproj(xs_rows, norm_mix, w_in_bf, tables_s, qn, kn, SAMPLE_ROWS, SAMPLE_ROWS)
    zs = z_s[:n_dec]
    ret_s, state_s = _ret_sample(zs[:, off[0]:off[1]], zs[:, off[1]:off[2]], zs[:, off[2]:off[3]],
                                 zs[:, off[3]:off[4]], state_ret[0])
    heads = lambda a: a.reshape(n_dec, MOBA_HEADS, MOBA_HEAD_DIM)
    mk_s, mv_s = zs[:, off[5]:off[6]], zs[:, off[6]:]
    pool_shape = cache_k.shape[1:]
    moba_s = _moba_sample(heads(zs[:, off[4]:off[5]]), heads(mk_s), heads(mv_s),
                          cache_k.reshape(pool_shape), cache_v.reshape(pool_shape), page_table)

    h_p, hn_p, idx_p, gate_p = _outproj(xp, ret_p, moba_p, w_o_bf, norm_ffn, wr_pad, br_pad, 256)
    h_s, hn_s, idx_s, gate_s = _outproj(xs_rows, _pad_rows(ret_s, SAMPLE_ROWS),
                                        _pad_rows(moba_s, SAMPLE_ROWS), w_o_bf, norm_ffn,
                                        wr_pad, br_pad, SAMPLE_ROWS)

    n_tok = t_p + n_dec
    top_idx = jnp.concatenate([idx_p[:, :TOP_K], idx_s[:n_dec, :TOP_K]], axis=0)
    n_assign = n_tok * TOP_K
    n_chunks = n_assign // MOE_ROW_BLOCK + N_EXPERTS
    n_slots = n_chunks * MOE_ROW_BLOCK
    n_items = N_EXPERTS + pl.cdiv(n_assign, MOE_ITEM_ROWS)
    dest, pad_lo, pad_hi, items = _route(top_idx, n_items)
    hn3 = jnp.concatenate([hn_p, hn_s[:n_dec]], axis=0).reshape(n_tok, ROW_PIECES, d // ROW_PIECES)
    xs3 = _dispatch(hn3, dest, pad_lo, pad_hi, n_slots)
    out2 = _moe(xs3, items, w_gu[0], b_gu[0], w_down[0], b_down[0])
    dest_pad = jnp.concatenate([dest, jnp.zeros(((SAMPLE_ROWS - n_dec) * TOP_K,), jnp.int32)])
    h2_p = _combine(dest_pad, h_p, gate_p, out2, 0, COMBINE_TOKENS)
    h2_s = _combine(dest_pad, h_s, gate_s, out2, t_p, SAMPLE_ROWS)

    y_p = _ple(h2_p, p_prompt[0].reshape(t_p, -1), norm_ple, wg_ple_bf, wp_ple_bf, 512)
    y_s = _ple(h2_s, _pad_rows(p_sample[0].reshape(n_dec, -1), SAMPLE_ROWS), norm_ple,
               wg_ple_bf, wp_ple_bf, SAMPLE_ROWS)

    kv = lambda a, n, l: a.reshape(1, n, l, MOBA_HEADS, MOBA_HEAD_DIM)
    return (y_p.reshape(batch, seq_len, d), y_s[:n_dec].reshape(n_dec, 1, d),
            kv(z_p[:, off[5]:off[6]], batch, seq_len), kv(z_p[:, off[6]:], batch, seq_len),
            state_p[None], kv(mk_s, n_dec, 1), kv(mv_s, n_dec, 1), state_s[None])
```

```python
import functools
import math

import jax
import jax.numpy as jnp
from jax import lax
from jax.experimental import pallas as pl
from jax.experimental.pallas import tpu as pltpu

F32 = jnp.float32
BF16 = jnp.bfloat16

RET_HEADS = 4
RET_DK = 256
RET_DV = 256
RET_ROPE_THETA = 10000.0
MOBA_HEADS = 8
MOBA_HEAD_DIM = 128
MOBA_BLOCK = 256
MOBA_TOPK = 3
ROPE_THETA = 500000.0
ROPE_DIM = MOBA_HEAD_DIM // 4
N_EXPERTS = 32
TOP_K = 4
SWIGLU_LIMIT = 7.0
SWIGLU_ALPHA = 1.702
EPS = 1e-6

LANES = 128
SUBLANES = 8
VMEM_LIMIT_BYTES = 56 * 1024 * 1024

SAMPLE_ROWS = 16
RET_CHUNK = 256
MOE_ROW_BLOCK = 128
MOE_MATMUL_ROWS = 256
MOE_ITEM_ROWS = 1536
MOE_FF_TILE = 512
MOE_VMEM_LIMIT_BYTES = 60 * 1024 * 1024
ROW_PIECES = 16
COMBINE_TOKENS = 128
MOBA_KV_GROUP = 4


def _cparams(semantics, vmem=VMEM_LIMIT_BYTES):
    return pltpu.CompilerParams(dimension_semantics=semantics, vmem_limit_bytes=vmem)


def _nt_dot(a, b, **kw):
    return lax.dot_general(a, b, (((1,), (1,)), ((), ())), preferred_element_type=F32, **kw)


def _rms(x):
    return x * lax.rsqrt(jnp.mean(x * x, axis=-1, keepdims=True) + EPS)


def _rnd(a):
    return a.astype(BF16).astype(F32)


_IN_TN = 512


def _inproj_kernel(x_ref, g_ref, w_ref, cr_ref, sr_ref, cm_ref, sa_ref, sb_ref, qn_ref, kn_ref,
                   z_ref, xn_ref):
    j = pl.program_id(1)
    n_ret = RET_HEADS * RET_DK // _IN_TN
    n_moba = MOBA_HEADS * MOBA_HEAD_DIM // _IN_TN
    moba0 = 4 * n_ret

    @pl.when(j == 0)
    def _():
        xn_ref[...] = (_rms(x_ref[...]) * g_ref[...]).astype(BF16)

    acc = jnp.dot(xn_ref[...], w_ref[...], preferred_element_type=F32)

    @pl.when(j < 2 * n_ret)
    def _():
        half = RET_DK // 2
        c = cr_ref[...]
        s = sr_ref[...]
        scale = jnp.where(j < n_ret, 1.0, RET_DK ** -0.5).astype(F32)
        for hh in range(_IN_TN // RET_DK):
            x1 = acc[:, hh * RET_DK:hh * RET_DK + half]
            x2 = acc[:, hh * RET_DK + half:(hh + 1) * RET_DK]
            z_ref[:, hh * RET_DK:hh * RET_DK + half] = (x1 * c - x2 * s) * scale
            z_ref[:, hh * RET_DK + half:(hh + 1) * RET_DK] = (x2 * c + x1 * s) * scale

    @pl.when(((j >= 2 * n_ret) & (j < moba0)) | (j >= moba0 + 2 * n_moba))
    def _():
        z_ref[...] = acc

    @pl.when((j >= moba0) & (j < moba0 + 2 * n_moba))
    def _():
        gain = jnp.where(j < moba0 + n_moba, qn_ref[...], kn_ref[...])
        for hh in range(_IN_TN // MOBA_HEAD_DIM):
            t = _rms(acc[:, hh * MOBA_HEAD_DIM:(hh + 1) * MOBA_HEAD_DIM]) * gain
            up = pltpu.roll(t, MOBA_HEAD_DIM - ROPE_DIM // 2, 1)
            dn = pltpu.roll(t, ROPE_DIM // 2, 1)
            z_ref[:, hh * MOBA_HEAD_DIM:(hh + 1) * MOBA_HEAD_DIM] = (
                t * cm_ref[...] + up * sa_ref[...] + dn * sb_ref[...])


def _rope_tables(pos):
    posf = pos.astype(F32)[:, None]
    half = RET_DK // 2
    inv = RET_ROPE_THETA ** (-2.0 * jnp.arange(half, dtype=F32) / RET_DK)
    ang = posf * inv[None, :]
    cr, sr = jnp.cos(ang), jnp.sin(ang)
    mh = ROPE_DIM // 2
    inv_m = ROPE_THETA ** (-2.0 * jnp.arange(mh, dtype=F32) / ROPE_DIM)
    ang_m = posf * inv_m[None, :]
    cos_m, sin_m = jnp.cos(ang_m), jnp.sin(ang_m)
    n = pos.shape[0]
    rest = MOBA_HEAD_DIM - ROPE_DIM
    cm = jnp.concatenate([cos_m, cos_m, jnp.ones((n, rest), F32)], axis=1)
    sa = jnp.concatenate([-sin_m, jnp.zeros((n, MOBA_HEAD_DIM - mh), F32)], axis=1)
    sb = jnp.concatenate([jnp.zeros((n, mh), F32), sin_m, jnp.zeros((n, rest), F32)], axis=1)
    return cr, sr, cm, sa, sb


def _inproj(x2d, norm, w_bf, tables, qn, kn, seq_len, tm):
    t, d = x2d.shape
    n_out = w_bf.shape[1]
    tpos = seq_len // tm
    cr, sr, cm, sa, sb = tables
    tab = pl.BlockSpec((tm, LANES), lambda i, j: (i % tpos, 0))
    vec = pl.BlockSpec((1, LANES), lambda i, j: (0, 0))
    return pl.pallas_call(
        _inproj_kernel,
        grid=(t // tm, n_out // _IN_TN),
        in_specs=[
            pl.BlockSpec((tm, d), lambda i, j: (i, 0)),
            pl.BlockSpec((1, d), lambda i, j: (0, 0)),
            pl.BlockSpec((d, _IN_TN), lambda i, j: (0, j)),
            tab, tab, tab, tab, tab, vec, vec,
        ],
        out_specs=pl.BlockSpec((tm, _IN_TN), lambda i, j: (i, j)),
        out_shape=jax.ShapeDtypeStruct((t, n_out), F32),
        scratch_shapes=[pltpu.VMEM((tm, d), BF16)],
        compiler_params=_cparams(("parallel", "arbitrary")),
        name="inproj",
    )(x2d, norm, w_bf, cr, sr, cm, sa, sb, qn, kn)


def _ret_gate(o, g):
    return _rms(o) * (g * jax.nn.sigmoid(g))


def _ret_prompt_kernel(logg_ref, q_ref, k_ref, v_ref, g_ref, o_ref, st_ref, r_ref):
    h = pl.program_id(1)
    c = pl.program_id(2)
    n = q_ref.shape[0]

    @pl.when(c == 0)
    def _():
        r_ref[...] = jnp.zeros_like(r_ref)

    lg = logg_ref[h]
    i = lax.broadcasted_iota(jnp.int32, (n, 1), 0).astype(F32)
    jj = lax.broadcasted_iota(jnp.int32, (1, n), 1).astype(F32)
    diff = i - jj
    inner = jnp.where(diff >= 0, jnp.exp(lg * jnp.maximum(diff, 0.0)), 0.0)
    q_dec = jnp.exp(lg * (i + 1.0))
    k_dec = jnp.exp(lg * (n - 1.0 - i))
    chunk_dec = jnp.exp(lg * jnp.full((1, 1), float(n), F32))

    q = q_ref[...]
    k = k_ref[...]
    vb = v_ref[...].astype(BF16)
    r = r_ref[...]
    s = _nt_dot(q.astype(BF16), k.astype(BF16)) * inner
    o = (jnp.dot(s.astype(BF16), vb, preferred_element_type=F32)
         + jnp.dot((q * q_dec).astype(BF16), r.astype(BF16), preferred_element_type=F32))
    kd_t = (k * k_dec).T.astype(BF16)
    r_new = r * chunk_dec + jnp.dot(kd_t, vb, preferred_element_type=F32)
    r_ref[...] = r_new
    o_ref[...] = _ret_gate(o, g_ref[...])

    @pl.when(c == pl.num_programs(2) - 1)
    def _():
        st_ref[0, 0] = r_new


def _ret_log_decay():
    return jnp.log1p(-jnp.exp2(-5.0 - jnp.arange(RET_HEADS, dtype=F32)))


def _ret_prompt(z, batch, seq_len):
    nc = seq_len // RET_CHUNK
    w = RET_DK

    def col(off):
        return pl.BlockSpec((RET_CHUNK, w), lambda b, h, c: (b * nc + c, off + h))

    return pl.pallas_call(
        _ret_prompt_kernel,
        grid=(batch, RET_HEADS, nc),
        in_specs=[pl.BlockSpec(memory_space=pltpu.SMEM),
                  col(0), col(RET_HEADS), col(2 * RET_HEADS), col(3 * RET_HEADS)],
        out_specs=[
            pl.BlockSpec((RET_CHUNK, RET_DV), lambda b, h, c: (b * nc + c, h)),
            pl.BlockSpec((1, 1, RET_DK, RET_DV), lambda b, h, c: (b, h, 0, 0)),
        ],
        scratch_shapes=[pltpu.VMEM((RET_DK, RET_DV), F32)],
        out_shape=[
            jax.ShapeDtypeStruct((batch * seq_len, RET_HEADS * RET_DV), F32),
            jax.ShapeDtypeStruct((batch, RET_HEADS, RET_DK, RET_DV), F32),
        ],
        compiler_params=_cparams(("parallel", "parallel", "arbitrary")),
        name="ret_prompt",
    )(_ret_log_decay(), z, z, z, z)


def _ret_sample_kernel(logg_ref, q_ref, kc_ref, kr_ref, v_ref, g_ref, s0_ref, o_ref, st_ref):
    h = pl.program_id(1)
    dec = jnp.exp(logg_ref[h] * jnp.ones((1, 1), F32))
    q = q_ref[0, 0]
    v = _rnd(v_ref[0, 0])
    r0 = s0_ref[0, 0]
    qk = jnp.sum(_rnd(q) * _rnd(kr_ref[0, 0]), axis=-1, keepdims=True)
    q8 = jnp.broadcast_to(q * dec, (SUBLANES, RET_DK)).astype(BF16)
    qr = jnp.dot(q8, r0.astype(BF16), preferred_element_type=F32)[0:1]
    o = _rnd(qk) * v + qr
    st_ref[0, 0] = r0 * dec + _rnd(kc_ref[0, 0]) * v
    o_ref[0, 0] = _ret_gate(o, g_ref[0, 0])


def _ret_sample(q, k, v, g, state):
    n = q.shape[0]
    row = lambda a: a.reshape(n, RET_HEADS, 1, RET_DK)
    rspec = pl.BlockSpec((1, 1, 1, RET_DK), lambda b, h: (b, h, 0, 0))
    mspec = pl.BlockSpec((1, 1, RET_DK, RET_DV), lambda b, h: (b, h, 0, 0))
    o, st = pl.pallas_call(
        _ret_sample_kernel,
        grid=(n, RET_HEADS),
        in_specs=[pl.BlockSpec(memory_space=pltpu.SMEM),
                  rspec, pl.BlockSpec((1, 1, RET_DK, 1), lambda b, h: (b, h, 0, 0)),
                  rspec, rspec, rspec, mspec],
        out_specs=[rspec, mspec],
        out_shape=[jax.ShapeDtypeStruct((n, RET_HEADS, 1, RET_DV), F32),
                   jax.ShapeDtypeStruct(state.shape, F32)],
        compiler_params=_cparams(("parallel", "parallel")),
        name="ret_sample",
    )(_ret_log_decay(), row(q), k.reshape(n, RET_HEADS, RET_DK, 1), row(k), row(v), row(g), state)
    return o.reshape(n, RET_HEADS * RET_DV), st


def _moba_prompt_kernel(q_ref, k_ref, v_ref, o_ref, kb_ref, vt_ref, km_ref, sel_ref):
    qi = pl.program_id(2)
    blk = MOBA_BLOCK
    nb = k_ref.shape[0] // blk
    scale = MOBA_HEAD_DIM ** -0.5

    @pl.when(qi == 0)
    def _():
        kb_ref[...] = k_ref[...].astype(BF16)
        for n in range(nb):
            rows = slice(n * blk, (n + 1) * blk)
            vt_ref[:, rows] = v_ref[rows, :].T.astype(BF16)
            km_ref[n:n + 1, :] = jnp.mean(k_ref[rows, :], axis=0, keepdims=True)

    qb = q_ref[...].astype(BF16)
    gate = _nt_dot(km_ref[...].astype(BF16), qb)
    kblock = lax.broadcasted_iota(jnp.int32, (nb, blk), 0)
    past = kblock < qi
    g = jnp.where(past, gate, -jnp.inf)
    cnt = jnp.zeros((nb, blk), F32)
    for m in range(nb):
        gm = g[m:m + 1, :]
        cnt = cnt + jnp.where(gm > g, 1.0, jnp.where((gm == g) & (kblock > m), 1.0, 0.0))
    sel_ref[...] = jnp.where(past & (cnt < MOBA_TOPK), 1.0, 0.0)

    own = pl.ds(pl.multiple_of(qi * blk, blk), blk)
    s = _nt_dot(kb_ref[own, :], qb) * scale
    key = lax.broadcasted_iota(jnp.int32, (blk, blk), 0)
    qry = lax.broadcasted_iota(jnp.int32, (blk, blk), 1)
    s = jnp.where(key <= qry, s, -jnp.inf)
    m0 = jnp.max(s, axis=0, keepdims=True)
    p = jnp.exp(s - m0)
    l0 = jnp.sum(p, axis=0, keepdims=True)
    a0 = jnp.dot(vt_ref[:, own], p.astype(BF16), preferred_element_type=F32)

    grp = MOBA_KV_GROUP
    span = grp * blk

    def body(j, carry):
        m_i, l_i, a_i = carry
        rows = pl.ds(pl.multiple_of(j * span, span), span)
        sn = _nt_dot(kb_ref[rows, :], qb) * scale
        allowed = jnp.concatenate(
            [jnp.broadcast_to(sel_ref[pl.ds(j * grp + t, 1), :], (blk, blk)) for t in range(grp)],
            axis=0)
        sn = jnp.where(allowed > 0.0, sn, -jnp.inf)
        m_new = jnp.maximum(m_i, jnp.max(sn, axis=0, keepdims=True))
        alpha = jnp.exp(m_i - m_new)
        pn = jnp.exp(sn - m_new)
        l_new = alpha * l_i + jnp.sum(pn, axis=0, keepdims=True)
        a_new = alpha * a_i + jnp.dot(vt_ref[:, rows], pn.astype(BF16), preferred_element_type=F32)
        return m_new, l_new, a_new

    _, l_f, a_f = lax.fori_loop(0, (qi + grp - 1) // grp, body, (m0, l0, a0))
    o_ref[...] = (a_f / l_f).T


def _moba_prompt(z, batch, seq_len):
    nq = seq_len // MOBA_BLOCK
    hd = MOBA_HEAD_DIM
    qoff = (2 * RET_HEADS * RET_DK + 2 * RET_HEADS * RET_DV) // hd
    koff = qoff + MOBA_HEADS
    voff = koff + MOBA_HEADS
    full = lambda off: pl.BlockSpec((seq_len, hd), lambda b, h, i: (b, off + h))
    return pl.pallas_call(
        _moba_prompt_kernel,
        grid=(batch, MOBA_HEADS, nq),
        in_specs=[pl.BlockSpec((MOBA_BLOCK, hd), lambda b, h, i: (b * nq + i, qoff + h)),
                  full(koff), full(voff)],
        out_specs=pl.BlockSpec((MOBA_BLOCK, hd), lambda b, h, i: (b * nq + i, h)),
        out_shape=jax.ShapeDtypeStruct((batch * seq_len, MOBA_HEADS * hd), F32),
        scratch_shapes=[pltpu.VMEM((seq_len, hd), BF16), pltpu.VMEM((hd, seq_len), BF16),
                        pltpu.VMEM((nq, hd), F32), pltpu.VMEM((nq, MOBA_BLOCK), F32)],
        compiler_params=_cparams(("parallel", "parallel", "arbitrary")),
        name="moba_prompt",
    )(z, z, z)


MOBA_SCAN_PAGES = 8


def _moba_scan_kernel(pt_ref, q_ref, *refs):
    k_refs = refs[:MOBA_SCAN_PAGES]
    sel_ref, gate_ref = refs[MOBA_SCAN_PAGES:]
    p = pl.program_id(1)
    ppb = MOBA_BLOCK // k_refs[0].shape[1]
    blocks_per_step = MOBA_SCAN_PAGES // ppb

    @pl.when(p == 0)
    def _():
        gate_ref[...] = jnp.zeros_like(gate_ref)

    q = _rnd(q_ref[0])
    lane = lax.broadcasted_iota(jnp.int32, gate_ref.shape, 1)
    gate = gate_ref[...]
    for jb in range(blocks_per_step):
        k_sum = sum(jnp.sum(k_refs[jb * ppb + j][0], axis=0) for j in range(ppb))
        g = jnp.sum(q * _rnd(k_sum * (1.0 / MOBA_BLOCK)), axis=1, keepdims=True)
        gate = jnp.where(lane == p * blocks_per_step + jb, g, gate)
    gate_ref[...] = gate

    @pl.when(p == pl.num_programs(1) - 1)
    def _():
        g = jnp.where(lane < pl.num_programs(1) * blocks_per_step, gate, -jnp.inf)
        out = jnp.zeros(gate_ref.shape, jnp.int32)
        for t in range(MOBA_TOPK):
            mx = jnp.max(g, axis=1, keepdims=True)
            idx = jnp.min(jnp.where(g == mx, lane, LANES), axis=1, keepdims=True)
            out = jnp.where(lane == t, idx, out)
            g = jnp.where(lane == idx, -jnp.inf, g)
        sel_ref[0] = out


def _moba_attend_kernel(pt_ref, selp_ref, q_ref, kn_ref, vn_ref, *refs):
    nsrc = (len(refs) - 1) // 2
    k_refs = refs[:nsrc]
    v_refs = refs[nsrc:2 * nsrc]
    o_ref = refs[2 * nsrc]
    h = pl.program_id(1)
    scale = MOBA_HEAD_DIM ** -0.5
    mine2 = lax.broadcasted_iota(jnp.int32, (MOBA_HEADS, MOBA_HEAD_DIM), 0) == h
    mine3 = lax.broadcasted_iota(jnp.int32, (1, MOBA_HEADS, 1), 1) == h
    all2 = lambda a, op: op(op(a, axis=1, keepdims=True), axis=0, keepdims=True)
    all3 = lambda a, op: op(op(a, axis=0, keepdims=True), axis=1, keepdims=True)
    q = _rnd(q_ref[0])
    s_new = all2(jnp.where(mine2, q * _rnd(kn_ref[0]), 0.0), jnp.sum) * scale
    ss = [jnp.where(mine3, jnp.sum(_rnd(kr[0]) * q[None], axis=-1, keepdims=True) * scale, -jnp.inf)
          for kr in k_refs]
    mx = s_new.reshape(1, 1, 1)
    for sj in ss:
        mx = jnp.maximum(mx, all3(sj, jnp.max))
    p_new = jnp.exp(s_new.reshape(1, 1, 1) - mx)
    ps = [jnp.exp(sj - mx) for sj in ss]
    den = p_new
    for pj in ps:
        den = den + all3(pj, jnp.sum)
    inv = 1.0 / den
    acc = jnp.where(mine2, _rnd(p_new * inv).reshape(1, 1) * _rnd(vn_ref[0]), 0.0)
    for pj, vr in zip(ps, v_refs):
        acc = acc + jnp.sum(_rnd(pj * inv) * _rnd(vr[0]), axis=0)
    o_ref[0, 0] = jnp.sum(acc, axis=0, keepdims=True)


def _moba_sample(q, k_new, v_new, cache_k, cache_v, page_table):
    n, npages = page_table.shape
    page = cache_k.shape[1]
    hd = MOBA_HEAD_DIM
    ppb = MOBA_BLOCK // page
    pp = MOBA_SCAN_PAGES
    assert npages % pp == 0 and npages // ppb <= LANES
    page_block = (1, page, MOBA_HEADS, hd)
    tok2 = pl.BlockSpec((1, MOBA_HEADS, hd), lambda b, p, pt: (b, 0, 0))
    scan_specs = [pl.BlockSpec(page_block, (lambda b, p, pt, j=j: (pt[b, p * pp + j], 0, 0, 0)))
                  for j in range(pp)]
    sel = pl.pallas_call(
        _moba_scan_kernel,
        grid_spec=pltpu.PrefetchScalarGridSpec(
            num_scalar_prefetch=1,
            grid=(n, npages // pp),
            in_specs=[tok2] + scan_specs,
            out_specs=pl.BlockSpec((1, MOBA_HEADS, LANES), lambda b, p, pt: (b, 0, 0)),
            scratch_shapes=[pltpu.VMEM((MOBA_HEADS, LANES), F32)],
        ),
        out_shape=jax.ShapeDtypeStruct((n, MOBA_HEADS, LANES), jnp.int32),
        compiler_params=_cparams(("parallel", "arbitrary")),
        name="moba_scan",
    )(page_table, q, *([cache_k] * pp))
    selp = (sel[:, :, :MOBA_TOPK, None] * ppb + jnp.arange(ppb, dtype=jnp.int32)).reshape(-1)
    nsrc = MOBA_TOPK * ppb

    def page_spec(j):
        def index(b, h, pt, sp):
            return (pt[b, sp[(b * MOBA_HEADS + h) * nsrc + j]], 0, 0, 0)
        return pl.BlockSpec(page_block, index)

    tok = pl.BlockSpec((1, MOBA_HEADS, hd), lambda b, h, pt, sp: (b, 0, 0))
    pages = [page_spec(j) for j in range(nsrc)]
    out = pl.pallas_call(
        _moba_attend_kernel,
        grid_spec=pltpu.PrefetchScalarGridSpec(
            num_scalar_prefetch=2,
            grid=(n, MOBA_HEADS),
            in_specs=[tok, tok, tok] + pages + pages,
            out_specs=pl.BlockSpec((1, 1, 1, hd), lambda b, h, pt, sp: (b, h, 0, 0)),
        ),
        out_shape=jax.ShapeDtypeStruct((n, MOBA_HEADS, 1, hd), F32),
        compiler_params=_cparams(("parallel", "parallel")),
        name="moba_attend",
    )(page_table, selp, q, k_new, v_new, *([cache_k] * nsrc), *([cache_v] * nsrc))
    return out.reshape(n, MOBA_HEADS * hd)


def _outproj_kernel(x_ref, ret_ref, moba_ref, wo_ref, nf_ref, wr_ref, br_ref,
                    h_ref, hn_ref, idx_ref, gate_ref):
    rw = ret_ref.shape[1]
    h = (x_ref[...]
         + jnp.dot(ret_ref[...].astype(BF16), wo_ref[:rw, :], preferred_element_type=F32)
         + jnp.dot(moba_ref[...].astype(BF16), wo_ref[rw:, :], preferred_element_type=F32))
    h_ref[...] = h
    hn = _rms(h) * nf_ref[...]
    hn_ref[...] = hn
    logits = jnp.dot(hn.astype(BF16), wr_ref[...], preferred_element_type=F32) + br_ref[...]
    lane = lax.broadcasted_iota(jnp.int32, logits.shape, 1)
    vals, idxs = [], []
    for _ in range(TOP_K):
        mx = jnp.max(logits, axis=1, keepdims=True)
        ix = jnp.min(jnp.where(logits == mx, lane, LANES), axis=1, keepdims=True)
        vals.append(mx)
        idxs.append(ix)
        logits = jnp.where(lane == ix, -jnp.inf, logits)
    es = [jnp.exp(v - vals[0]) for v in vals]
    den = es[0] + es[1] + es[2] + es[3]
    idx_out = jnp.zeros(lane.shape, jnp.int32)
    gate_out = jnp.zeros(lane.shape, F32)
    for t in range(TOP_K):
        idx_out = jnp.where(lane == t, idxs[t], idx_out)
        gate_out = jnp.where(lane == t, es[t] / den, gate_out)
    idx_ref[...] = idx_out
    gate_ref[...] = gate_out


def _outproj(x2d, ret, moba, wo_bf, norm_ffn, wr_pad, br_pad, tm):
    t, d = x2d.shape
    rw, mw = ret.shape[1], moba.shape[1]
    const = lambda shape: pl.BlockSpec(shape, lambda i: (0, 0))
    rows = lambda w: pl.BlockSpec((tm, w), lambda i: (i, 0))
    return pl.pallas_call(
        _outproj_kernel,
        grid=(t // tm,),
        in_specs=[rows(d), rows(rw), rows(mw), const((rw + mw, d)), const((1, d)),
                  const((d, LANES)), const((1, LANES))],
        out_specs=[rows(d), rows(d), rows(LANES), rows(LANES)],
        out_shape=[jax.ShapeDtypeStruct((t, d), F32), jax.ShapeDtypeStruct((t, d), F32),
                   jax.ShapeDtypeStruct((t, LANES), jnp.int32),
                   jax.ShapeDtypeStruct((t, LANES), F32)],
        compiler_params=_cparams(("parallel",)),
        name="outproj",
    )(x2d, ret, moba, wo_bf, norm_ffn, wr_pad, br_pad)


def _dispatch_kernel(dest_ref, pad_lo_ref, pad_hi_ref, hn_ref, xs_hbm, zero_ref, sem, zsem, *,
                     ntok):
    i = pl.program_id(0)
    tb = hn_ref.shape[0]
    first = i * tb
    count = jnp.minimum(tb, ntok - first)

    def row_copy(r, k):
        return pltpu.make_async_copy(hn_ref.at[r], xs_hbm.at[dest_ref[(first + r) * TOP_K + k]], sem)

    def issue(r, c):
        for k in range(TOP_K):
            row_copy(r, k).start()
        return c

    lax.fori_loop(0, count, issue, 0)

    @pl.when(i == 0)
    def _():
        zero_ref[...] = jnp.zeros_like(zero_ref)

        def pad_copy(s):
            return pltpu.make_async_copy(zero_ref.at[0], xs_hbm.at[s], zsem)

        def per_expert(e, c):
            def pad_row(s, c2):
                pad_copy(s).start()
                return c2
            lax.fori_loop(pad_lo_ref[e], pad_hi_ref[e], pad_row, 0)
            return c

        lax.fori_loop(0, N_EXPERTS, per_expert, 0)

        rb = zero_ref.shape[0]
        tail0 = pad_hi_ref[N_EXPERTS - 1]
        n_tail = (xs_hbm.shape[0] - tail0) // rb

        def tail_copy(c):
            rows = pl.ds(pl.multiple_of(tail0 + c * rb, rb), rb)
            return pltpu.make_async_copy(zero_ref, xs_hbm.at[rows], zsem)

        def tail_start(c, c2):
            tail_copy(c).start()
            return c2

        lax.fori_loop(0, n_tail, tail_start, 0)

        def per_expert_wait(e, c):
            def pad_wait(s, c2):
                pad_copy(s).wait()
                return c2
            lax.fori_loop(pad_lo_ref[e], pad_hi_ref[e], pad_wait, 0)
            return c

        lax.fori_loop(0, N_EXPERTS, per_expert_wait, 0)

        def tail_wait(c, c2):
            tail_copy(c).wait()
            return c2

        lax.fori_loop(0, n_tail, tail_wait, 0)

    def drain(r, c):
        for k in range(TOP_K):
            row_copy(r, k).wait()
        return c

    lax.fori_loop(0, count, drain, 0)


def _dispatch(hn3, dest_flat, pad_lo, pad_hi, n_slots):
    ntok = hn3.shape[0]
    tb = COMBINE_TOKENS
    return pl.pallas_call(
        functools.partial(_dispatch_kernel, ntok=ntok),
        grid_spec=pltpu.PrefetchScalarGridSpec(
            num_scalar_prefetch=3,
            grid=(pl.cdiv(ntok, tb),),
            in_specs=[pl.BlockSpec((tb,) + hn3.shape[1:], lambda i, d, lo, hi: (i, 0, 0))],
            out_specs=pl.BlockSpec(memory_space=pl.ANY),
            scratch_shapes=[pltpu.VMEM((MOE_ROW_BLOCK,) + hn3.shape[1:], F32),
                            pltpu.SemaphoreType.DMA(()), pltpu.SemaphoreType.DMA(())],
        ),
        out_shape=jax.ShapeDtypeStruct((n_slots,) + hn3.shape[1:], F32),
        compiler_params=_cparams(("arbitrary",)),
        name="moe_dispatch",
    )(dest_flat, pad_lo, pad_hi, hn3)


def _moe_kernel(item_e, item_row0, item_nch, tail_ref, xs_hbm, wg_ref, wu_ref, bg_ref, bu_ref,
                wd_ref, bd_ref, out_hbm, x_ref, acc_ref, stage_ref, wgb_ref, wub_ref, wdb_ref,
                sem_in, sem_out):
    it = pl.program_id(0)
    f = pl.program_id(1)
    nf = pl.num_programs(1)
    rb = MOE_ROW_BLOCK
    pieces = ROW_PIECES
    nch = item_nch[it]
    row0 = item_row0[it]

    def chunk_rows(r):
        return pl.ds(pl.multiple_of((row0 + r * rb) * pieces, rb * pieces), rb * pieces)

    def in_copy(r, slot):
        return pltpu.make_async_copy(xs_hbm.at[chunk_rows(r)], stage_ref.at[slot], sem_in.at[slot])

    def out_copy(r, slot):
        return pltpu.make_async_copy(stage_ref.at[slot], out_hbm.at[chunk_rows(r)], sem_out.at[slot])

    @pl.when((f == 0) & (nch > 0))
    def _():
        in_copy(0, 0).start()

        def load(r, c):
            slot = r % 2

            @pl.when(r + 1 < nch)
            def _():
                in_copy(r + 1, 1 - slot).start()

            in_copy(r, slot).wait()
            rows = pl.ds(pl.multiple_of(r * rb, rb), rb)
            for p in range(pieces):
                piece = stage_ref[slot, pl.ds(p, rb, stride=pieces), :]
                x_ref[rows, p * LANES:(p + 1) * LANES] = piece.astype(BF16)
            acc_ref[rows, :] = jnp.broadcast_to(bd_ref[0], (rb, acc_ref.shape[1]))
            return c

        lax.fori_loop(0, nch, load, 0)

    @pl.when(nch > 0)
    def _():
        wgb_ref[...] = wg_ref[0].astype(BF16)
        wub_ref[...] = wu_ref[0].astype(BF16)
        wdb_ref[...] = wd_ref[0].astype(BF16)
        bg = bg_ref[0]
        bu = bu_ref[0]

        def compute(start, size):
            rows = pl.ds(start, size)
            x = x_ref[rows, :]
            g = jnp.dot(x, wgb_ref[...], preferred_element_type=F32) + bg
            u = jnp.dot(x, wub_ref[...], preferred_element_type=F32) + bu
            g = jnp.minimum(g, SWIGLU_LIMIT)
            u = jnp.clip(u, -SWIGLU_LIMIT, SWIGLU_LIMIT)
            act = (u + 1.0) * g * jax.nn.sigmoid(SWIGLU_ALPHA * g)
            acc_ref[rows, :] += jnp.dot(act.astype(BF16), wdb_ref[...], preferred_element_type=F32)

        big = MOE_MATMUL_ROWS
        n_big = nch // (big // rb)

        def full(r, c):
            compute(pl.multiple_of(r * big, big), big)
            return c

        lax.fori_loop(0, n_big, full, 0)

        @pl.when(nch % (big // rb) != 0)
        def _():
            compute(pl.multiple_of(n_big * big, rb), rb)

    @pl.when((f == nf - 1) & (nch > 0))
    def _():
        def store(r, c):
            slot = r % 2

            @pl.when(r >= 2)
            def _():
                out_copy(r - 2, slot).wait()

            rows = pl.ds(pl.multiple_of(r * rb, rb), rb)
            for p in range(pieces):
                stage_ref[slot, pl.ds(p, rb, stride=pieces), :] = acc_ref[rows, p * LANES:(p + 1) * LANES]
            out_copy(r, slot).start()
            return c

        lax.fori_loop(0, nch, store, 0)

        @pl.when(nch >= 2)
        def _():
            out_copy(nch - 2, nch % 2).wait()

        out_copy(nch - 1, (nch - 1) % 2).wait()

    @pl.when((it == pl.num_programs(0) - 1) & (f == nf - 1))
    def _():
        tail0 = tail_ref[0]
        n_tail = (out_hbm.shape[0] // pieces - tail0) // rb
        stage_ref[0] = jnp.zeros(stage_ref.shape[1:], F32)

        def tail_copy(c):
            rows = pl.ds(pl.multiple_of((tail0 + c * rb) * pieces, rb * pieces), rb * pieces)
            return pltpu.make_async_copy(stage_ref.at[0], out_hbm.at[rows], sem_out.at[0])

        def tail_start(c, c2):
            tail_copy(c).start()
            return c2

        def tail_wait(c, c2):
            tail_copy(c).wait()
            return c2

        lax.fori_loop(0, n_tail, tail_start, 0)
        lax.fori_loop(0, n_tail, tail_wait, 0)


def _moe(xs3, items, w_gu, b_gu, w_down, b_down):
    item_e, item_row0, item_nch, tail0 = items
    n_items = item_e.shape[0]
    n_slots = xs3.shape[0]
    d = w_gu.shape[1]
    ff = w_down.shape[1]
    tf = MOE_FF_TILE
    nf = ff // tf
    assert MOE_MATMUL_ROWS == 2 * MOE_ROW_BLOCK and MOE_ITEM_ROWS % MOE_MATMUL_ROWS == 0
    xs2 = xs3.reshape(n_slots * ROW_PIECES, LANES)

    def ftile(it, f, nch):
        return jnp.where(nch[it] > 0, f, nf - 1)

    wg_spec = pl.BlockSpec((1, d, tf), lambda it, f, e, r0, nch, t0: (e[it], 0, ftile(it, f, nch)))
    wu_spec = pl.BlockSpec((1, d, tf), lambda it, f, e, r0, nch, t0: (e[it], 0, nf + ftile(it, f, nch)))
    bg_spec = pl.BlockSpec((1, 1, tf), lambda it, f, e, r0, nch, t0: (e[it], 0, ftile(it, f, nch)))
    bu_spec = pl.BlockSpec((1, 1, tf), lambda it, f, e, r0, nch, t0: (e[it], 0, nf + ftile(it, f, nch)))
    wd_spec = pl.BlockSpec((1, tf, d), lambda it, f, e, r0, nch, t0: (e[it], ftile(it, f, nch), 0))
    bd_spec = pl.BlockSpec((1, 1, d), lambda it, f, e, r0, nch, t0: (e[it], 0, 0))
    b_gu3 = b_gu.reshape(N_EXPERTS, 1, 2 * ff)
    b_down3 = b_down.reshape(N_EXPERTS, 1, d)
    out2 = pl.pallas_call(
        _moe_kernel,
        grid_spec=pltpu.PrefetchScalarGridSpec(
            num_scalar_prefetch=4,
            grid=(n_items, nf),
            in_specs=[pl.BlockSpec(memory_space=pl.ANY), wg_spec, wu_spec, bg_spec, bu_spec,
                      wd_spec, bd_spec],
            out_specs=pl.BlockSpec(memory_space=pl.ANY),
            scratch_shapes=[
                pltpu.VMEM((MOE_ITEM_ROWS, d), BF16),
                pltpu.VMEM((MOE_ITEM_ROWS, d), F32),
                pltpu.VMEM((2, MOE_ROW_BLOCK * ROW_PIECES, LANES), F32),
                pltpu.VMEM((d, tf), BF16), pltpu.VMEM((d, tf), BF16), pltpu.VMEM((tf, d), BF16),
                pltpu.SemaphoreType.DMA((2,)), pltpu.SemaphoreType.DMA((2,)),
            ],
        ),
        out_shape=jax.ShapeDtypeStruct(xs2.shape, F32),
        compiler_params=_cparams(("arbitrary", "arbitrary"), MOE_VMEM_LIMIT_BYTES),
        name="moe_experts",
    )(item_e, item_row0, item_nch, tail0, xs2, w_gu, w_gu, b_gu3, b_gu3, w_down, b_down3)
    return out2


def _combine_kernel(dest_ref, h_ref, gate_ref, out_hbm, o_ref, buf_ref, sem, *, tok_off):
    i = pl.program_id(0)
    tb = h_ref.shape[0]
    pieces = ROW_PIECES
    base = (tok_off + i * tb) * TOP_K

    def row_copy(r, k):
        d = dest_ref[base + r * TOP_K + k]
        src = out_hbm.at[pl.ds(pl.multiple_of(d * pieces, pieces), pieces)]
        dst = buf_ref.at[pl.ds(pl.multiple_of((k * tb + r) * pieces, pieces), pieces)]
        return pltpu.make_async_copy(src, dst, sem)

    def issue(r, c):
        for k in range(TOP_K):
            row_copy(r, k).start()
        return c

    lax.fori_loop(0, tb, issue, 0)

    def drain(r, c):
        for k in range(TOP_K):
            row_copy(r, k).wait()
        return c

    lax.fori_loop(0, tb, drain, 0)

    for p in range(pieces):
        f = jnp.zeros((tb, LANES), F32)
        for k in range(TOP_K):
            rows = buf_ref[pl.ds(k * tb * pieces + p, tb, stride=pieces), :]
            f = f + gate_ref[:, k:k + 1] * rows
        o_ref[:, p * LANES:(p + 1) * LANES] = h_ref[:, p * LANES:(p + 1) * LANES] + f


def _combine(dest_flat, h, gates, out2, tok_off, tb):
    t, d = h.shape
    return pl.pallas_call(
        functools.partial(_combine_kernel, tok_off=tok_off),
        grid_spec=pltpu.PrefetchScalarGridSpec(
            num_scalar_prefetch=1,
            grid=(t // tb,),
            in_specs=[pl.BlockSpec((tb, d), lambda i, ds: (i, 0)),
                      pl.BlockSpec((tb, LANES), lambda i, ds: (i, 0)),
                      pl.BlockSpec(memory_space=pl.ANY)],
            out_specs=pl.BlockSpec((tb, d), lambda i, ds: (i, 0)),
            scratch_shapes=[pltpu.VMEM((TOP_K * tb * ROW_PIECES, LANES), F32),
                            pltpu.SemaphoreType.DMA(())],
        ),
        out_shape=jax.ShapeDtypeStruct((t, d), F32),
        compiler_params=_cparams(("arbitrary",)),
        name="moe_combine",
    )(dest_flat, h, gates, out2)


def _ple_kernel(h_ref, p_ref, n_ref, wg_ref, wp_ref, y_ref):
    h = h_ref[...]
    hn = (_rms(h) * n_ref[...]).astype(BF16)
    gate = jax.nn.sigmoid(jnp.dot(hn, wg_ref[...], preferred_element_type=F32))
    proj = jnp.dot(p_ref[...].astype(BF16), wp_ref[...], preferred_element_type=F32)
    y_ref[...] = h + gate * proj


def _ple(h, p, norm, wg_bf, wp_bf, tm):
    t, d = h.shape
    pd = p.shape[1]
    const = lambda shape: pl.BlockSpec(shape, lambda i: (0, 0))
    return pl.pallas_call(
        _ple_kernel,
        grid=(t // tm,),
        in_specs=[pl.BlockSpec((tm, d), lambda i: (i, 0)), pl.BlockSpec((tm, pd), lambda i: (i, 0)),
                  const((1, d)), const((d, d)), const((pd, d))],
        out_specs=pl.BlockSpec((tm, d), lambda i: (i, 0)),
        out_shape=jax.ShapeDtypeStruct((t, d), F32),
        compiler_params=_cparams(("parallel",)),
        name="ple",
    )(h, p, norm, wg_bf, wp_bf)


def _route(top_idx, n_items):
    rb = MOE_ROW_BLOCK
    flat = top_idx.reshape(-1)
    onehot = (flat[:, None] == jnp.arange(N_EXPERTS, dtype=jnp.int32)[None, :]).astype(jnp.int32)
    csum = jnp.cumsum(onehot, axis=0)
    rank = jnp.sum(onehot * (csum - 1), axis=1)
    counts = csum[-1]
    padded = (counts + rb - 1) // rb * rb
    seg_end = jnp.cumsum(padded)
    seg_start = seg_end - padded
    dest = (jnp.sum(onehot * seg_start[None, :], axis=1) + rank).astype(jnp.int32)
    pad_lo = (seg_start + counts).astype(jnp.int32)
    pad_hi = seg_end.astype(jnp.int32)
    per = (padded + MOE_ITEM_ROWS - 1) // MOE_ITEM_ROWS
    item_end = jnp.cumsum(per)
    item_start = item_end - per
    ids = jnp.arange(n_items, dtype=jnp.int32)
    e = jnp.minimum(jnp.searchsorted(item_end, ids, side="right"), N_EXPERTS - 1).astype(jnp.int32)
    valid = ids < item_end[-1]
    piece = ids - item_start[e]
    row0 = seg_start[e] + piece * MOE_ITEM_ROWS
    rows = jnp.clip(padded[e] - piece * MOE_ITEM_ROWS, 0, MOE_ITEM_ROWS)
    nch = jnp.where(valid, rows // rb, 0).astype(jnp.int32)
    last_e = e[jnp.maximum(item_end[-1] - 1, 0)]
    item_e = jnp.where(valid, e, last_e).astype(jnp.int32)
    item_row0 = jnp.where(valid, row0, 0).astype(jnp.int32)
    return dest, pad_lo, pad_hi, (item_e, item_row0, nch, pad_hi[-1:])


def _pad_rows(a, rows):
    return jnp.concatenate([a, jnp.zeros((rows - a.shape[0],) + a.shape[1:], a.dtype)], axis=0)


def kernel(x_prompt, x_sample, cache_k, cache_v, state_ret, page_table, p_prompt, p_sample,
           norm_mix, w_in, q_norm, k_norm, w_o, norm_ffn, w_router, b_router, w_gu, b_gu,
           w_down, b_down, norm_ple, w_ple_gate, w_ple_proj):
    depth = norm_mix.shape[0]
    assert depth == 1
    batch, seq_len, d = x_prompt.shape
    n_dec, dec_seq, _ = x_sample.shape
    assert dec_seq == 1 and n_dec <= SAMPLE_ROWS
    past_len = page_table.shape[1] * cache_k.shape[2]
    ret_w = RET_HEADS * RET_DK
    moba_w = MOBA_HEADS * MOBA_HEAD_DIM
    off = [0, ret_w, 2 * ret_w, 3 * ret_w, 4 * ret_w, 4 * ret_w + moba_w, 4 * ret_w + 2 * moba_w]

    w_in_bf = w_in[0].astype(BF16)
    w_o_bf = w_o[0].astype(BF16)
    wg_ple_bf = w_ple_gate[0].astype(BF16)
    wp_ple_bf = w_ple_proj[0].astype(BF16)
    wr_pad = jnp.concatenate([w_router[0], jnp.zeros((d, LANES - N_EXPERTS), F32)],
                             axis=1).astype(BF16)
    br_pad = jnp.concatenate([b_router[0], jnp.full((LANES - N_EXPERTS,), -jnp.inf, F32)])[None, :]
    qn, kn = q_norm, k_norm

    t_p = batch * seq_len
    xp = x_prompt.reshape(t_p, d)
    tm = 1024
    tables_p = _rope_tables(jnp.arange(seq_len, dtype=jnp.int32))
    z_p = _inproj(xp, norm_mix, w_in_bf, tables_p, qn, kn, seq_len, tm)
    ret_p, state_p = _ret_prompt(z_p, batch, seq_len)
    moba_p = _moba_prompt(z_p, batch, seq_len)

    xs_rows = _pad_rows(x_sample.reshape(n_dec, d), SAMPLE_ROWS)
    tables_s = _rope_tables(jnp.full((SAMPLE_ROWS,), past_len, jnp.int32))
    z_s = _inproj(xs_rows, norm_mix, w_in_bf, tables_s, qn, kn, SAMPLE_ROWS, SAMPLE_ROWS)
    zs = z_s[:n_dec]
    ret_s, state_s = _ret_sample(zs[:, off[0]:off[1]], zs[:, off[1]:off[2]], zs[:, off[2]:off[3]],
                                 zs[:, off[3]:off[4]], state_ret[0])
    heads = lambda a: a.reshape(n_dec, MOBA_HEADS, MOBA_HEAD_DIM)
    mk_s, mv_s = zs[:, off[5]:off[6]], zs[:, off[6]:]
    pool_shape = cache_k.shape[1:]
    moba_s = _moba_sample(heads(zs[:, off[4]:off[5]]), heads(mk_s), heads(mv_s),
                          cache_k.reshape(pool_shape), cache_v.reshape(pool_shape), page_table)

    h_p, hn_p, idx_p, gate_p = _outproj(xp, ret_p, moba_p, w_o_bf, norm_ffn, wr_pad, br_pad, 256)
    h_s, hn_s, idx_s, gate_s = _outproj(xs_rows, _pad_rows(ret_s, SAMPLE_ROWS),
                                        _pad_rows(moba_s, SAMPLE_ROWS), w_o_bf, norm_ffn,
                                        wr_pad, br_pad, SAMPLE_ROWS)

    n_tok = t_p + n_dec
    top_idx = jnp.concatenate([idx_p[:, :TOP_K], idx_s[:n_dec, :TOP_K]], axis=0)
    n_assign = n_tok * TOP_K
    n_chunks = n_assign // MOE_ROW_BLOCK + N_EXPERTS
    n_slots = n_chunks * MOE_ROW_BLOCK
    n_items = N_EXPERTS + pl.cdiv(n_assign, MOE_ITEM_ROWS)
    dest, pad_lo, pad_hi, items = _route(top_idx, n_items)
    hn3 = jnp.concatenate([hn_p, hn_s[:n_dec]], axis=0).reshape(n_tok, ROW_PIECES, d // ROW_PIECES)
    xs3 = _dispatch(hn3, dest, pad_lo, pad_hi, n_slots)
    out2 = _moe(xs3, items, w_gu[0], b_gu[0], w_down[0], b_down[0])
    dest_pad = jnp.concatenate([dest, jnp.zeros(((SAMPLE_ROWS - n_dec) * TOP_K,), jnp.int32)])
    h2_p = _combine(dest_pad, h_p, gate_p, out2, 0, COMBINE_TOKENS)
    h2_s = _combine(dest_pad, h_s, gate_s, out2, t_p, SAMPLE_ROWS)

    y_p = _ple(h2_p, p_prompt[0].reshape(t_p, -1), norm_ple, wg_ple_bf, wp_ple_bf, 512)
    y_s = _ple(h2_s, _pad_rows(p_sample[0].reshape(n_dec, -1), SAMPLE_ROWS), norm_ple,
               wg_ple_bf, wp_ple_bf, SAMPLE_ROWS)

    kv = lambda a, n, l: a.reshape(1, n, l, MOBA_HEADS, MOBA_HEAD_DIM)
    return (y_p.reshape(batch, seq_len, d), y_s[:n_dec].reshape(n_dec, 1, d),
            kv(z_p[:, off[5]:off[6]], batch, seq_len), kv(z_p[:, off[6]:], batch, seq_len),
            state_p[None], kv(mk_s, n_dec, 1), kv(mv_s, n_dec, 1), state_s[None])
```

```python
import functools
import math

import jax
import jax.numpy as jnp
from jax import lax
from jax.experimental import pallas as pl
from jax.experimental.pallas import tpu as pltpu

F32 = jnp.float32
BF16 = jnp.bfloat16

RET_HEADS = 4
RET_DK = 256
RET_DV = 256
RET_ROPE_THETA = 10000.0
MOBA_HEADS = 8
MOBA_HEAD_DIM = 128
MOBA_BLOCK = 256
MOBA_TOPK = 3
ROPE_THETA = 500000.0
ROPE_DIM = MOBA_HEAD_DIM // 4
N_EXPERTS = 32
TOP_K = 4
SWIGLU_LIMIT = 7.0
SWIGLU_ALPHA = 1.702
EPS = 1e-6

LANES = 128
SUBLANES = 8
VMEM_LIMIT_BYTES = 56 * 1024 * 1024

SAMPLE_ROWS = 16
RET_CHUNK = 256
MOE_ROW_BLOCK = 128
MOE_MATMUL_ROWS = 256
MOE_ITEM_ROWS = 1536
MOE_FF_TILE = 512
MOE_VMEM_LIMIT_BYTES = 60 * 1024 * 1024
ROW_PIECES = 16
COMBINE_TOKENS = 128
MOBA_KV_GROUP = 4


def _cparams(semantics, vmem=VMEM_LIMIT_BYTES):
    return pltpu.CompilerParams(dimension_semantics=semantics, vmem_limit_bytes=vmem)


def _nt_dot(a, b, **kw):
    return lax.dot_general(a, b, (((1,), (1,)), ((), ())), preferred_element_type=F32, **kw)


def _rms(x):
    return x * lax.rsqrt(jnp.mean(x * x, axis=-1, keepdims=True) + EPS)


def _rnd(a):
    return a.astype(BF16).astype(F32)


_IN_TN = 512


def _inproj_kernel(x_ref, g_ref, w_ref, cr_ref, sr_ref, cm_ref, sa_ref, sb_ref, qn_ref, kn_ref,
                   z_ref, xn_ref):
    j = pl.program_id(1)
    n_ret = RET_HEADS * RET_DK // _IN_TN
    n_moba = MOBA_HEADS * MOBA_HEAD_DIM // _IN_TN
    moba0 = 4 * n_ret

    @pl.when(j == 0)
    def _():
        xn_ref[...] = (_rms(x_ref[...]) * g_ref[...]).astype(BF16)

    acc = jnp.dot(xn_ref[...], w_ref[...], preferred_element_type=F32)

    @pl.when(j < 2 * n_ret)
    def _():
        half = RET_DK // 2
        c = cr_ref[...]
        s = sr_ref[...]
        scale = jnp.where(j < n_ret, 1.0, RET_DK ** -0.5).astype(F32)
        for hh in range(_IN_TN // RET_DK):
            x1 = acc[:, hh * RET_DK:hh * RET_DK + half]
            x2 = acc[:, hh * RET_DK + half:(hh + 1) * RET_DK]
            z_ref[:, hh * RET_DK:hh * RET_DK + half] = (x1 * c - x2 * s) * scale
            z_ref[:, hh * RET_DK + half:(hh + 1) * RET_DK] = (x2 * c + x1 * s) * scale

    @pl.when(((j >= 2 * n_ret) & (j < moba0)) | (j >= moba0 + 2 * n_moba))
    def _():
        z_ref[...] = acc

    @pl.when((j >= moba0) & (j < moba0 + 2 * n_moba))
    def _():
        gain = jnp.where(j < moba0 + n_moba, qn_ref[...], kn_ref[...])
        for hh in range(_IN_TN // MOBA_HEAD_DIM):
            t = _rms(acc[:, hh * MOBA_HEAD_DIM:(hh + 1) * MOBA_HEAD_DIM]) * gain
            up = pltpu.roll(t, MOBA_HEAD_DIM - ROPE_DIM // 2, 1)
            dn = pltpu.roll(t, ROPE_DIM // 2, 1)
            z_ref[:, hh * MOBA_HEAD_DIM:(hh + 1) * MOBA_HEAD_DIM] = (
                t * cm_ref[...] + up * sa_ref[...] + dn * sb_ref[...])


def _rope_tables(pos):
    posf = pos.astype(F32)[:, None]
    half = RET_DK // 2
    inv = RET_ROPE_THETA ** (-2.0 * jnp.arange(half, dtype=F32) / RET_DK)
    ang = posf * inv[None, :]
    cr, sr = jnp.cos(ang), jnp.sin(ang)
    mh = ROPE_DIM // 2
    inv_m = ROPE_THETA ** (-2.0 * jnp.arange(mh, dtype=F32) / ROPE_DIM)
    ang_m = posf * inv_m[None, :]
    cos_m, sin_m = jnp.cos(ang_m), jnp.sin(ang_m)
    n = pos.shape[0]
    rest = MOBA_HEAD_DIM - ROPE_DIM
    cm = jnp.concatenate([cos_m, cos_m, jnp.ones((n, rest), F32)], axis=1)
    sa = jnp.concatenate([-sin_m, jnp.zeros((n, MOBA_HEAD_DIM - mh), F32)], axis=1)
    sb = jnp.concatenate([jnp.zeros((n, mh), F32), sin_m, jnp.zeros((n, rest), F32)], axis=1)
    return cr, sr, cm, sa, sb


def _inproj(x2d, norm, w_bf, tables, qn, kn, seq_len, tm):
    t, d = x2d.shape
    n_out = w_bf.shape[1]
    tpos = seq_len // tm
    cr, sr, cm, sa, sb = tables
    tab = pl.BlockSpec((tm, LANES), lambda i, j: (i % tpos, 0))
    vec = pl.BlockSpec((1, LANES), lambda i, j: (0, 0))
    return pl.pallas_call(
        _inproj_kernel,
        grid=(t // tm, n_out // _IN_TN),
        in_specs=[
            pl.BlockSpec((tm, d), lambda i, j: (i, 0)),
            pl.BlockSpec((1, d), lambda i, j: (0, 0)),
            pl.BlockSpec((d, _IN_TN), lambda i, j: (0, j)),
            tab, tab, tab, tab, tab, vec, vec,
        ],
        out_specs=pl.BlockSpec((tm, _IN_TN), lambda i, j: (i, j)),
        out_shape=jax.ShapeDtypeStruct((t, n_out), F32),
        scratch_shapes=[pltpu.VMEM((tm, d), BF16)],
        compiler_params=_cparams(("parallel", "arbitrary")),
        name="inproj",
    )(x2d, norm, w_bf, cr, sr, cm, sa, sb, qn, kn)


def _ret_gate(o, g):
    return _rms(o) * (g * jax.nn.sigmoid(g))


def _ret_prompt_kernel(logg_ref, q_ref, k_ref, v_ref, g_ref, o_ref, st_ref, r_ref):
    h = pl.program_id(1)
    c = pl.program_id(2)
    n = q_ref.shape[0]

    @pl.when(c == 0)
    def _():
        r_ref[...] = jnp.zeros_like(r_ref)

    lg = logg_ref[h]
    i = lax.broadcasted_iota(jnp.int32, (n, 1), 0).astype(F32)
    jj = lax.broadcasted_iota(jnp.int32, (1, n), 1).astype(F32)
    diff = i - jj
    inner = jnp.where(diff >= 0, jnp.exp(lg * jnp.maximum(diff, 0.0)), 0.0)
    q_dec = jnp.exp(lg * (i + 1.0))
    k_dec = jnp.exp(lg * (n - 1.0 - i))
    chunk_dec = jnp.exp(lg * jnp.full((1, 1), float(n), F32))

    q = q_ref[...]
    k = k_ref[...]
    vb = v_ref[...].astype(BF16)
    r = r_ref[...]
    s = _nt_dot(q.astype(BF16), k.astype(BF16)) * inner
    o = (jnp.dot(s.astype(BF16), vb, preferred_element_type=F32)
         + jnp.dot((q * q_dec).astype(BF16), r.astype(BF16), preferred_element_type=F32))
    kd_t = (k * k_dec).T.astype(BF16)
    r_new = r * chunk_dec + jnp.dot(kd_t, vb, preferred_element_type=F32)
    r_ref[...] = r_new
    o_ref[...] = _ret_gate(o, g_ref[...])

    @pl.when(c == pl.num_programs(2) - 1)
    def _():
        st_ref[0, 0] = r_new


def _ret_log_decay():
    return jnp.log1p(-jnp.exp2(-5.0 - jnp.arange(RET_HEADS, dtype=F32)))


def _ret_prompt(z, batch, seq_len):
    nc = seq_len // RET_CHUNK
    w = RET_DK

    def col(off):
        return pl.BlockSpec((RET_CHUNK, w), lambda b, h, c: (b * nc + c, off + h))

    return pl.pallas_call(
        _ret_prompt_kernel,
        grid=(batch, RET_HEADS, nc),
        in_specs=[pl.BlockSpec(memory_space=pltpu.SMEM),
                  col(0), col(RET_HEADS), col(2 * RET_HEADS), col(3 * RET_HEADS)],
        out_specs=[
            pl.BlockSpec((RET_CHUNK, RET_DV), lambda b, h, c: (b * nc + c, h)),
            pl.BlockSpec((1, 1, RET_DK, RET_DV), lambda b, h, c: (b, h, 0, 0)),
        ],
        scratch_shapes=[pltpu.VMEM((RET_DK, RET_DV), F32)],
        out_shape=[
            jax.ShapeDtypeStruct((batch * seq_len, RET_HEADS * RET_DV), F32),
            jax.ShapeDtypeStruct((batch, RET_HEADS, RET_DK, RET_DV), F32),
        ],
        compiler_params=_cparams(("parallel", "parallel", "arbitrary")),
        name="ret_prompt",
    )(_ret_log_decay(), z, z, z, z)


def _ret_sample_kernel(logg_ref, q_ref, kc_ref, kr_ref, v_ref, g_ref, s0_ref, o_ref, st_ref):
    h = pl.program_id(1)
    dec = jnp.exp(logg_ref[h] * jnp.ones((1, 1), F32))
    q = q_ref[0, 0]
    v = _rnd(v_ref[0, 0])
    r0 = s0_ref[0, 0]
    qk = jnp.sum(_rnd(q) * _rnd(kr_ref[0, 0]), axis=-1, keepdims=True)
    q8 = jnp.broadcast_to(q * dec, (SUBLANES, RET_DK)).astype(BF16)
    qr = jnp.dot(q8, r0.astype(BF16), preferred_element_type=F32)[0:1]
    o = _rnd(qk) * v + qr
    st_ref[0, 0] = r0 * dec + _rnd(kc_ref[0, 0]) * v
    o_ref[0, 0] = _ret_gate(o, g_ref[0, 0])


def _ret_sample(q, k, v, g, state):
    n = q.shape[0]
    row = lambda a: a.reshape(n, RET_HEADS, 1, RET_DK)
    rspec = pl.BlockSpec((1, 1, 1, RET_DK), lambda b, h: (b, h, 0, 0))
    mspec = pl.BlockSpec((1, 1, RET_DK, RET_DV), lambda b, h: (b, h, 0, 0))
    o, st = pl.pallas_call(
        _ret_sample_kernel,
        grid=(n, RET_HEADS),
        in_specs=[pl.BlockSpec(memory_space=pltpu.SMEM),
                  rspec, pl.BlockSpec((1, 1, RET_DK, 1), lambda b, h: (b, h, 0, 0)),
                  rspec, rspec, rspec, mspec],
        out_specs=[rspec, mspec],
        out_shape=[jax.ShapeDtypeStruct((n, RET_HEADS, 1, RET_DV), F32),
                   jax.ShapeDtypeStruct(state.shape, F32)],
        compiler_params=_cparams(("parallel", "parallel")),
        name="ret_sample",
    )(_ret_log_decay(), row(q), k.reshape(n, RET_HEADS, RET_DK, 1), row(k), row(v), row(g), state)
    return o.reshape(n, RET_HEADS * RET_DV), st


def _moba_prompt_kernel(q_ref, k_ref, v_ref, o_ref, kb_ref, vt_ref, km_ref, sel_ref):
    qi = pl.program_id(2)
    blk = MOBA_BLOCK
    nb = k_ref.shape[0] // blk
    scale = MOBA_HEAD_DIM ** -0.5

    @pl.when(qi == 0)
    def _():
        kb_ref[...] = k_ref[...].astype(BF16)
        for n in range(nb):
            rows = slice(n * blk, (n + 1) * blk)
            vt_ref[:, rows] = v_ref[rows, :].T.astype(BF16)
            km_ref[n:n + 1, :] = jnp.mean(k_ref[rows, :], axis=0, keepdims=True)

    qb = q_ref[...].astype(BF16)
    gate = _nt_dot(km_ref[...].astype(BF16), qb)
    kblock = lax.broadcasted_iota(jnp.int32, (nb, blk), 0)
    past = kblock < qi
    g = jnp.where(past, gate, -jnp.inf)
    cnt = jnp.zeros((nb, blk), F32)
    for m in range(nb):
        gm = g[m:m + 1, :]
        cnt = cnt + jnp.where(gm > g, 1.0, jnp.where((gm == g) & (kblock > m), 1.0, 0.0))
    sel_ref[...] = jnp.where(past & (cnt < MOBA_TOPK), 1.0, 0.0)

    own = pl.ds(pl.multiple_of(qi * blk, blk), blk)
    s = _nt_dot(kb_ref[own, :], qb) * scale
    key = lax.broadcasted_iota(jnp.int32, (blk, blk), 0)
    qry = lax.broadcasted_iota(jnp.int32, (blk, blk), 1)
    s = jnp.where(key <= qry, s, -jnp.inf)
    m0 = jnp.max(s, axis=0, keepdims=True)
    p = jnp.exp(s - m0)
    l0 = jnp.sum(p, axis=0, keepdims=True)
    a0 = jnp.dot(vt_ref[:, own], p.astype(BF16), preferred_element_type=F32)

    grp = MOBA_KV_GROUP
    span = grp * blk
    last = nb // grp - 1

    def scores(j):
        rows = pl.ds(pl.multiple_of(j * span, span), span)
        return _nt_dot(kb_ref[rows, :], qb)

    def body(j, carry):
        m_i, l_i, a_i, s_cur = carry
        s_next = scores(jnp.minimum(j + 1, last))
        rows = pl.ds(pl.multiple_of(j * span, span), span)
        allowed = jnp.concatenate(
            [jnp.broadcast_to(sel_ref[pl.ds(j * grp + t, 1), :], (blk, blk)) for t in range(grp)],
            axis=0)
        sn = jnp.where(allowed > 0.0, s_cur * scale, -jnp.inf)
        m_new = jnp.maximum(m_i, jnp.max(sn, axis=0, keepdims=True))
        alpha = jnp.exp(m_i - m_new)
        pn = jnp.exp(sn - m_new)
        l_new = alpha * l_i + jnp.sum(pn, axis=0, keepdims=True)
        a_new = alpha * a_i + jnp.dot(vt_ref[:, rows], pn.astype(BF16), preferred_element_type=F32)
        return m_new, l_new, a_new, s_next

    _, l_f, a_f, _ = lax.fori_loop(0, (qi + grp - 1) // grp, body, (m0, l0, a0, scores(0)))
    o_ref[...] = (a_f / l_f).T


def _moba_prompt(z, batch, seq_len):
    nq = seq_len // MOBA_BLOCK
    hd = MOBA_HEAD_DIM
    qoff = (2 * RET_HEADS * RET_DK + 2 * RET_HEADS * RET_DV) // hd
    koff = qoff + MOBA_HEADS
    voff = koff + MOBA_HEADS
    full = lambda off: pl.BlockSpec((seq_len, hd), lambda b, h, i: (b, off + h))
    return pl.pallas_call(
        _moba_prompt_kernel,
        grid=(batch, MOBA_HEADS, nq),
        in_specs=[pl.BlockSpec((MOBA_BLOCK, hd), lambda b, h, i: (b * nq + i, qoff + h)),
                  full(koff), full(voff)],
        out_specs=pl.BlockSpec((MOBA_BLOCK, hd), lambda b, h, i: (b * nq + i, h)),
        out_shape=jax.ShapeDtypeStruct((batch * seq_len, MOBA_HEADS * hd), F32),
        scratch_shapes=[pltpu.VMEM((seq_len, hd), BF16), pltpu.VMEM((hd, seq_len), BF16),
                        pltpu.VMEM((nq, hd), F32), pltpu.VMEM((nq, MOBA_BLOCK), F32)],
        compiler_params=_cparams(("parallel", "parallel", "arbitrary")),
        name="moba_prompt",
    )(z, z, z)


MOBA_SCAN_PAGES = 8


def _moba_scan_kernel(pt_ref, q_ref, *refs):
    k_refs = refs[:MOBA_SCAN_PAGES]
    sel_ref, gate_ref = refs[MOBA_SCAN_PAGES:]
    p = pl.program_id(1)
    ppb = MOBA_BLOCK // k_refs[0].shape[1]
    blocks_per_step = MOBA_SCAN_PAGES // ppb

    @pl.when(p == 0)
    def _():
        gate_ref[...] = jnp.zeros_like(gate_ref)

    q = _rnd(q_ref[0])
    lane = lax.broadcasted_iota(jnp.int32, gate_ref.shape, 1)
    gate = gate_ref[...]
    for jb in range(blocks_per_step):
        k_sum = sum(jnp.sum(k_refs[jb * ppb + j][0], axis=0) for j in range(ppb))
        g = jnp.sum(q * _rnd(k_sum * (1.0 / MOBA_BLOCK)), axis=1, keepdims=True)
        gate = jnp.where(lane == p * blocks_per_step + jb, g, gate)
    gate_ref[...] = gate

    @pl.when(p == pl.num_programs(1) - 1)
    def _():
        g = jnp.where(lane < pl.num_programs(1) * blocks_per_step, gate, -jnp.inf)
        out = jnp.zeros(gate_ref.shape, jnp.int32)
        for t in range(MOBA_TOPK):
            mx = jnp.max(g, axis=1, keepdims=True)
            idx = jnp.min(jnp.where(g == mx, lane, LANES), axis=1, keepdims=True)
            out = jnp.where(lane == t, idx, out)
            g = jnp.where(lane == idx, -jnp.inf, g)
        sel_ref[0] = out


def _moba_attend_kernel(pt_ref, selp_ref, q_ref, kn_ref, vn_ref, *refs):
    nsrc = (len(refs) - 1) // 2
    k_refs = refs[:nsrc]
    v_refs = refs[nsrc:2 * nsrc]
    o_ref = refs[2 * nsrc]
    h = pl.program_id(1)
    scale = MOBA_HEAD_DIM ** -0.5
    mine2 = lax.broadcasted_iota(jnp.int32, (MOBA_HEADS, MOBA_HEAD_DIM), 0) == h
    mine3 = lax.broadcasted_iota(jnp.int32, (1, MOBA_HEADS, 1), 1) == h
    all2 = lambda a, op: op(op(a, axis=1, keepdims=True), axis=0, keepdims=True)
    all3 = lambda a, op: op(op(a, axis=0, keepdims=True), axis=1, keepdims=True)
    q = _rnd(q_ref[0])
    s_new = all2(jnp.where(mine2, q * _rnd(kn_ref[0]), 0.0), jnp.sum) * scale
    ss = [jnp.where(mine3, jnp.sum(_rnd(kr[0]) * q[None], axis=-1, keepdims=True) * scale, -jnp.inf)
          for kr in k_refs]
    mx = s_new.reshape(1, 1, 1)
    for sj in ss:
        mx = jnp.maximum(mx, all3(sj, jnp.max))
    p_new = jnp.exp(s_new.reshape(1, 1, 1) - mx)
    ps = [jnp.exp(sj - mx) for sj in ss]
    den = p_new
    for pj in ps:
        den = den + all3(pj, jnp.sum)
    inv = 1.0 / den
    acc = jnp.where(mine2, _rnd(p_new * inv).reshape(1, 1) * _rnd(vn_ref[0]), 0.0)
    for pj, vr in zip(ps, v_refs):
        acc = acc + jnp.sum(_rnd(pj * inv) * _rnd(vr[0]), axis=0)
    o_ref[0, 0] = jnp.sum(acc, axis=0, keepdims=True)


def _moba_sample(q, k_new, v_new, cache_k, cache_v, page_table):
    n, npages = page_table.shape
    page = cache_k.shape[1]
    hd = MOBA_HEAD_DIM
    ppb = MOBA_BLOCK // page
    pp = MOBA_SCAN_PAGES
    assert npages % pp == 0 and npages // ppb <= LANES
    page_block = (1, page, MOBA_HEADS, hd)
    tok2 = pl.BlockSpec((1, MOBA_HEADS, hd), lambda b, p, pt: (b, 0, 0))
    scan_specs = [pl.BlockSpec(page_block, (lambda b, p, pt, j=j: (pt[b, p * pp + j], 0, 0, 0)))
                  for j in range(pp)]
    sel = pl.pallas_call(
        _moba_scan_kernel,
        grid_spec=pltpu.PrefetchScalarGridSpec(
            num_scalar_prefetch=1,
            grid=(n, npages // pp),
            in_specs=[tok2] + scan_specs,
            out_specs=pl.BlockSpec((1, MOBA_HEADS, LANES), lambda b, p, pt: (b, 0, 0)),
            scratch_shapes=[pltpu.VMEM((MOBA_HEADS, LANES), F32)],
        ),
        out_shape=jax.ShapeDtypeStruct((n, MOBA_HEADS, LANES), jnp.int32),
        compiler_params=_cparams(("parallel", "arbitrary")),
        name="moba_scan",
    )(page_table, q, *([cache_k] * pp))
    selp = (sel[:, :, :MOBA_TOPK, None] * ppb + jnp.arange(ppb, dtype=jnp.int32)).reshape(-1)
    nsrc = MOBA_TOPK * ppb

    def page_spec(j):
        def index(b, h, pt, sp):
            return (pt[b, sp[(b * MOBA_HEADS + h) * nsrc + j]], 0, 0, 0)
        return pl.BlockSpec(page_block, index)

    tok = pl.BlockSpec((1, MOBA_HEADS, hd), lambda b, h, pt, sp: (b, 0, 0))
    pages = [page_spec(j) for j in range(nsrc)]
    out = pl.pallas_call(
        _moba_attend_kernel,
        grid_spec=pltpu.PrefetchScalarGridSpec(
            num_scalar_prefetch=2,
            grid=(n, MOBA_HEADS),
            in_specs=[tok, tok, tok] + pages + pages,
            out_specs=pl.BlockSpec((1, 1, 1, hd), lambda b, h, pt, sp: (b, h, 0, 0)),
        ),
        out_shape=jax.ShapeDtypeStruct((n, MOBA_HEADS, 1, hd), F32),
        compiler_params=_cparams(("parallel", "parallel")),
        name="moba_attend",
    )(page_table, selp, q, k_new, v_new, *([cache_k] * nsrc), *([cache_v] * nsrc))
    return out.reshape(n, MOBA_HEADS * hd)


def _outproj_kernel(x_ref, ret_ref, moba_ref, wo_ref, nf_ref, wr_ref, br_ref,
                    h_ref, hn_ref, idx_ref, gate_ref):
    rw = ret_ref.shape[1]
    h = (x_ref[...]
         + jnp.dot(ret_ref[...].astype(BF16), wo_ref[:rw, :], preferred_element_type=F32)
         + jnp.dot(moba_ref[...].astype(BF16), wo_ref[rw:, :], preferred_element_type=F32))
    h_ref[...] = h
    hn = _rms(h) * nf_ref[...]
    hn_ref[...] = hn
    logits = jnp.dot(hn.astype(BF16), wr_ref[...], preferred_element_type=F32) + br_ref[...]
    lane = lax.broadcasted_iota(jnp.int32, logits.shape, 1)
    vals, idxs = [], []
    for _ in range(TOP_K):
        mx = jnp.max(logits, axis=1, keepdims=True)
        ix = jnp.min(jnp.where(logits == mx, lane, LANES), axis=1, keepdims=True)
        vals.append(mx)
        idxs.append(ix)
        logits = jnp.where(lane == ix, -jnp.inf, logits)
    es = [jnp.exp(v - vals[0]) for v in vals]
    den = es[0] + es[1] + es[2] + es[3]
    idx_out = jnp.zeros(lane.shape, jnp.int32)
    gate_out = jnp.zeros(lane.shape, F32)
    for t in range(TOP_K):
        idx_out = jnp.where(lane == t, idxs[t], idx_out)
        gate_out = jnp.where(lane == t, es[t] / den, gate_out)
    idx_ref[...] = idx_out
    gate_ref[...] = gate_out


def _outproj(x2d, ret, moba, wo_bf, norm_ffn, wr_pad, br_pad, tm):
    t, d = x2d.shape
    rw, mw = ret.shape[1], moba.shape[1]
    const = lambda shape: pl.BlockSpec(shape, lambda i: (0, 0))
    rows = lambda w: pl.BlockSpec((tm, w), lambda i: (i, 0))
    return pl.pallas_call(
        _outproj_kernel,
        grid=(t // tm,),
        in_specs=[rows(d), rows(rw), rows(mw), const((rw + mw, d)), const((1, d)),
                  const((d, LANES)), const((1, LANES))],
        out_specs=[rows(d), rows(d), rows(LANES), rows(LANES)],
        out_shape=[jax.ShapeDtypeStruct((t, d), F32), jax.ShapeDtypeStruct((t, d), F32),
                   jax.ShapeDtypeStruct((t, LANES), jnp.int32),
                   jax.ShapeDtypeStruct((t, LANES), F32)],
        compiler_params=_cparams(("parallel",)),
        name="outproj",
    )(x2d, ret, moba, wo_bf, norm_ffn, wr_pad, br_pad)


def _dispatch_kernel(dest_ref, pad_lo_ref, pad_hi_ref, hn_ref, xs_hbm, zero_ref, sem, zsem, *,
                     ntok):
    i = pl.program_id(0)
    tb = hn_ref.shape[0]
    first = i * tb
    count = jnp.minimum(tb, ntok - first)

    def row_copy(r, k):
        return pltpu.make_async_copy(hn_ref.at[r], xs_hbm.at[dest_ref[(first + r) * TOP_K + k]], sem)

    def issue(r, c):
        for k in range(TOP_K):
            row_copy(r, k).start()
        return c

    lax.fori_loop(0, count, issue, 0)

    @pl.when(i == 0)
    def _():
        zero_ref[...] = jnp.zeros_like(zero_ref)

        def pad_copy(s):
            return pltpu.make_async_copy(zero_ref.at[0], xs_hbm.at[s], zsem)

        def per_expert(e, c):
            def pad_row(s, c2):
                pad_copy(s).start()
                return c2
            lax.fori_loop(pad_lo_ref[e], pad_hi_ref[e], pad_row, 0)
            return c

        lax.fori_loop(0, N_EXPERTS, per_expert, 0)

        rb = zero_ref.shape[0]
        tail0 = pad_hi_ref[N_EXPERTS - 1]
        n_tail = (xs_hbm.shape[0] - tail0) // rb

        def tail_copy(c):
            rows = pl.ds(pl.multiple_of(tail0 + c * rb, rb), rb)
            return pltpu.make_async_copy(zero_ref, xs_hbm.at[rows], zsem)

        def tail_start(c, c2):
            tail_copy(c).start()
            return c2

        lax.fori_loop(0, n_tail, tail_start, 0)

        def per_expert_wait(e, c):
            def pad_wait(s, c2):
                pad_copy(s).wait()
                return c2
            lax.fori_loop(pad_lo_ref[e], pad_hi_ref[e], pad_wait, 0)
            return c

        lax.fori_loop(0, N_EXPERTS, per_expert_wait, 0)

        def tail_wait(c, c2):
            tail_copy(c).wait()
            return c2

        lax.fori_loop(0, n_tail, tail_wait, 0)

    @pl.when(count == tb)
    def _():
        for k in range(TOP_K):
            pltpu.make_async_copy(hn_ref, xs_hbm.at[pl.ds(0, tb)], sem).wait()

    @pl.when(count < tb)
    def _():
        def drain(r, c):
            for k in range(TOP_K):
                row_copy(r, k).wait()
            return c

        lax.fori_loop(0, count, drain, 0)


def _dispatch(hn3, dest_flat, pad_lo, pad_hi, n_slots):
    ntok = hn3.shape[0]
    tb = COMBINE_TOKENS
    return pl.pallas_call(
        functools.partial(_dispatch_kernel, ntok=ntok),
        grid_spec=pltpu.PrefetchScalarGridSpec(
            num_scalar_prefetch=3,
            grid=(pl.cdiv(ntok, tb),),
            in_specs=[pl.BlockSpec((tb,) + hn3.shape[1:], lambda i, d, lo, hi: (i, 0, 0))],
            out_specs=pl.BlockSpec(memory_space=pl.ANY),
            scratch_shapes=[pltpu.VMEM((MOE_ROW_BLOCK,) + hn3.shape[1:], F32),
                            pltpu.SemaphoreType.DMA(()), pltpu.SemaphoreType.DMA(())],
        ),
        out_shape=jax.ShapeDtypeStruct((n_slots,) + hn3.shape[1:], F32),
        compiler_params=_cparams(("arbitrary",)),
        name="moe_dispatch",
    )(dest_flat, pad_lo, pad_hi, hn3)


def _moe_kernel(item_e, item_row0, item_nch, tail_ref, xs_hbm, wg_ref, wu_ref, bg_ref, bu_ref,
                wd_ref, bd_ref, out_hbm, x_ref, acc_ref, stage_ref, wgb_ref, wub_ref, wdb_ref,
                sem_in, sem_out):
    it = pl.program_id(0)
    f = pl.program_id(1)
    nf = pl.num_programs(1)
    rb = MOE_ROW_BLOCK
    big = MOE_MATMUL_ROWS
    pieces = ROW_PIECES
    nch = item_nch[it]
    row0 = item_row0[it]
    n_big = nch // (big // rb)
    has_tail = nch % (big // rb) != 0
    tail_row = pl.multiple_of(n_big * big, rb)

    def hbm_rows(start, size):
        return pl.ds(pl.multiple_of((row0 + start) * pieces, rb * pieces), size * pieces)

    def in_copy(start, size, slot):
        return pltpu.make_async_copy(xs_hbm.at[hbm_rows(start, size)],
                                     stage_ref.at[slot, pl.ds(0, size * pieces)], sem_in.at[slot])

    def out_copy(start, size, slot):
        return pltpu.make_async_copy(stage_ref.at[slot, pl.ds(0, size * pieces)],
                                     out_hbm.at[hbm_rows(start, size)], sem_out.at[slot])

    def stage_to_x(start, size, slot):
        rows = pl.ds(start, size)
        for p in range(pieces):
            piece = stage_ref[slot, pl.ds(p, size, stride=pieces), :]
            x_ref[rows, p * LANES:(p + 1) * LANES] = piece.astype(BF16)
        acc_ref[rows, :] = jnp.broadcast_to(bd_ref[0], (size, acc_ref.shape[1]))

    def acc_to_stage(start, size, slot):
        rows = pl.ds(start, size)
        for p in range(pieces):
            stage_ref[slot, pl.ds(p, size, stride=pieces), :] = acc_ref[rows, p * LANES:(p + 1) * LANES]

    @pl.when(nch > 0)
    def _():
        wgb_ref[...] = wg_ref[0].astype(BF16)
        wub_ref[...] = wu_ref[0].astype(BF16)
        wdb_ref[...] = wd_ref[0].astype(BF16)
        bg = bg_ref[0]
        bu = bu_ref[0]

        def compute(start, size):
            rows = pl.ds(start, size)
            x = x_ref[rows, :]
            g = jnp.dot(x, wgb_ref[...], preferred_element_type=F32) + bg
            u = jnp.dot(x, wub_ref[...], preferred_element_type=F32) + bu
            g = jnp.minimum(g, SWIGLU_LIMIT)
            u = jnp.clip(u, -SWIGLU_LIMIT, SWIGLU_LIMIT)
            act = (u + 1.0) * g * jax.nn.sigmoid(SWIGLU_ALPHA * g)
            acc_ref[rows, :] += jnp.dot(act.astype(BF16), wdb_ref[...], preferred_element_type=F32)

        @pl.when(f == 0)
        def _():
            @pl.when(n_big > 0)
            def _():
                in_copy(0, big, 0).start()

            def step(r, c):
                slot = r % 2
                start = pl.multiple_of(r * big, big)

                @pl.when(r + 1 < n_big)
                def _():
                    in_copy(start + big, big, 1 - slot).start()

                in_copy(start, big, slot).wait()
                stage_to_x(start, big, slot)
                compute(start, big)
                return c

            lax.fori_loop(0, n_big, step, 0)

            @pl.when(has_tail)
            def _():
                cp = in_copy(tail_row, rb, 0)
                cp.start()
                cp.wait()
                stage_to_x(tail_row, rb, 0)
                compute(tail_row, rb)

        @pl.when((f > 0) & (f < nf - 1))
        def _():
            def step(r, c):
                compute(pl.multiple_of(r * big, big), big)
                return c

            lax.fori_loop(0, n_big, step, 0)

            @pl.when(has_tail)
            def _():
                compute(tail_row, rb)

        @pl.when(f == nf - 1)
        def _():
            def step(r, c):
                slot = r % 2
                start = pl.multiple_of(r * big, big)
                compute(start, big)

                @pl.when(r >= 2)
                def _():
                    out_copy(start - 2 * big, big, slot).wait()

                acc_to_stage(start, big, slot)
                out_copy(start, big, slot).start()
                return c

            lax.fori_loop(0, n_big, step, 0)

            @pl.when(n_big >= 2)
            def _():
                out_copy(0, big, n_big % 2).wait()

            @pl.when(n_big >= 1)
            def _():
                out_copy(0, big, (n_big - 1) % 2).wait()

            @pl.when(has_tail)
            def _():
                compute(tail_row, rb)
                acc_to_stage(tail_row, rb, 0)
                cp = out_copy(tail_row, rb, 0)
                cp.start()
                cp.wait()

    @pl.when((it == pl.num_programs(0) - 1) & (f == nf - 1))
    def _():
        tail0 = tail_ref[0]
        n_tail = (out_hbm.shape[0] // pieces - tail0) // rb
        stage_ref[0] = jnp.zeros(stage_ref.shape[1:], F32)

        def tail_copy(c):
            rows = pl.ds(pl.multiple_of((tail0 + c * rb) * pieces, rb * pieces), rb * pieces)
            return pltpu.make_async_copy(stage_ref.at[0, pl.ds(0, rb * pieces)], out_hbm.at[rows],
                                         sem_out.at[0])

        def tail_start(c, c2):
            tail_copy(c).start()
            return c2

        def tail_wait(c, c2):
            tail_copy(c).wait()
            return c2

        lax.fori_loop(0, n_tail, tail_start, 0)
        lax.fori_loop(0, n_tail, tail_wait, 0)


def _moe(xs3, items, w_gu, b_gu, w_down, b_down):
    item_e, item_row0, item_nch, tail0 = items
    n_items = item_e.shape[0]
    n_slots = xs3.shape[0]
    d = w_gu.shape[1]
    ff = w_down.shape[1]
    tf = MOE_FF_TILE
    nf = ff // tf
    assert MOE_MATMUL_ROWS == 2 * MOE_ROW_BLOCK and MOE_ITEM_ROWS % MOE_MATMUL_ROWS == 0 and nf >= 2
    xs2 = xs3.reshape(n_slots * ROW_PIECES, LANES)

    def ftile(it, f, nch):
        return jnp.where(nch[it] > 0, f, nf - 1)

    wg_spec = pl.BlockSpec((1, d, tf), lambda it, f, e, r0, nch, t0: (e[it], 0, ftile(it, f, nch)))
    wu_spec = pl.BlockSpec((1, d, tf), lambda it, f, e, r0, nch, t0: (e[it], 0, nf + ftile(it, f, nch)))
    bg_spec = pl.BlockSpec((1, 1, tf), lambda it, f, e, r0, nch, t0: (e[it], 0, ftile(it, f, nch)))
    bu_spec = pl.BlockSpec((1, 1, tf), lambda it, f, e, r0, nch, t0: (e[it], 0, nf + ftile(it, f, nch)))
    wd_spec = pl.BlockSpec((1, tf, d), lambda it, f, e, r0, nch, t0: (e[it], ftile(it, f, nch), 0))
    bd_spec = pl.BlockSpec((1, 1, d), lambda it, f, e, r0, nch, t0: (e[it], 0, 0))
    b_gu3 = b_gu.reshape(N_EXPERTS, 1, 2 * ff)
    b_down3 = b_down.reshape(N_EXPERTS, 1, d)
    out2 = pl.pallas_call(
        _moe_kernel,
        grid_spec=pltpu.PrefetchScalarGridSpec(
            num_scalar_prefetch=4,
            grid=(n_items, nf),
            in_specs=[pl.BlockSpec(memory_space=pl.ANY), wg_spec, wu_spec, bg_spec, bu_spec,
                      wd_spec, bd_spec],
            out_specs=pl.BlockSpec(memory_space=pl.ANY),
            scratch_shapes=[
                pltpu.VMEM((MOE_ITEM_ROWS, d), BF16),
                pltpu.VMEM((MOE_ITEM_ROWS, d), F32),
                pltpu.VMEM((2, MOE_MATMUL_ROWS * ROW_PIECES, LANES), F32),
                pltpu.VMEM((d, tf), BF16), pltpu.VMEM((d, tf), BF16), pltpu.VMEM((tf, d), BF16),
                pltpu.SemaphoreType.DMA((2,)), pltpu.SemaphoreType.DMA((2,)),
            ],
        ),
        out_shape=jax.ShapeDtypeStruct(xs2.shape, F32),
        compiler_params=_cparams(("arbitrary", "arbitrary"), MOE_VMEM_LIMIT_BYTES),
        name="moe_experts",
    )(item_e, item_row0, item_nch, tail0, xs2, w_gu, w_gu, b_gu3, b_gu3, w_down, b_down3)
    return out2


def _combine_kernel(dest_ref, h_ref, gate_ref, out_hbm, o_ref, buf_ref, sem, *, tok_off):
    i = pl.program_id(0)
    tb = h_ref.shape[0]
    pieces = ROW_PIECES

    def row_copy(step, r, k, slot):
        d = dest_ref[(tok_off + step * tb + r) * TOP_K + k]
        src = out_hbm.at[pl.ds(pl.multiple_of(d * pieces, pieces), pieces)]
        dst = buf_ref.at[slot, pl.ds(pl.multiple_of((k * tb + r) * pieces, pieces), pieces)]
        return pltpu.make_async_copy(src, dst, sem.at[slot])

    def issue(step, slot):
        def body(r, c):
            for k in range(TOP_K):
                row_copy(step, r, k, slot).start()
            return c
        lax.fori_loop(0, tb, body, 0)

    @pl.when(i == 0)
    def _():
        issue(0, 0)

    @pl.when(i + 1 < pl.num_programs(0))
    def _():
        issue(i + 1, (i + 1) % 2)

    slot = i % 2
    pltpu.make_async_copy(out_hbm.at[pl.ds(0, TOP_K * tb * pieces)], buf_ref.at[slot],
                          sem.at[slot]).wait()

    for p in range(pieces):
        f = jnp.zeros((tb, LANES), F32)
        for k in range(TOP_K):
            rows = buf_ref[slot, pl.ds(k * tb * pieces + p, tb, stride=pieces), :]
            f = f + gate_ref[:, k:k + 1] * rows
        o_ref[:, p * LANES:(p + 1) * LANES] = h_ref[:, p * LANES:(p + 1) * LANES] + f


def _combine(dest_flat, h, gates, out2, tok_off, tb):
    t, d = h.shape
    return pl.pallas_call(
        functools.partial(_combine_kernel, tok_off=tok_off),
        grid_spec=pltpu.PrefetchScalarGridSpec(
            num_scalar_prefetch=1,
            grid=(t // tb,),
            in_specs=[pl.BlockSpec((tb, d), lambda i, ds: (i, 0)),
                      pl.BlockSpec((tb, LANES), lambda i, ds: (i, 0)),
                      pl.BlockSpec(memory_space=pl.ANY)],
            out_specs=pl.BlockSpec((tb, d), lambda i, ds: (i, 0)),
            scratch_shapes=[pltpu.VMEM((2, TOP_K * tb * ROW_PIECES, LANES), F32),
                            pltpu.SemaphoreType.DMA((2,))],
        ),
        out_shape=jax.ShapeDtypeStruct((t, d), F32),
        compiler_params=_cparams(("arbitrary",)),
        name="moe_combine",
    )(dest_flat, h, gates, out2)


def _ple_kernel(h_ref, p_ref, n_ref, wg_ref, wp_ref, y_ref):
    h = h_ref[...]
    hn = (_rms(h) * n_ref[...]).astype(BF16)
    gate = jax.nn.sigmoid(jnp.dot(hn, wg_ref[...], preferred_element_type=F32))
    proj = jnp.dot(p_ref[...].astype(BF16), wp_ref[...], preferred_element_type=F32)
    y_ref[...] = h + gate * proj


def _ple(h, p, norm, wg_bf, wp_bf, tm):
    t, d = h.shape
    pd = p.shape[1]
    const = lambda shape: pl.BlockSpec(shape, lambda i: (0, 0))
    return pl.pallas_call(
        _ple_kernel,
        grid=(t // tm,),
        in_specs=[pl.BlockSpec((tm, d), lambda i: (i, 0)), pl.BlockSpec((tm, pd), lambda i: (i, 0)),
                  const((1, d)), const((d, d)), const((pd, d))],
        out_specs=pl.BlockSpec((tm, d), lambda i: (i, 0)),
        out_shape=jax.ShapeDtypeStruct((t, d), F32),
        compiler_params=_cparams(("parallel",)),
        name="ple",
    )(h, p, norm, wg_bf, wp_bf)


def _route(top_idx, n_items):
    rb = MOE_ROW_BLOCK
    flat = top_idx.reshape(-1)
    onehot = (flat[:, None] == jnp.arange(N_EXPERTS, dtype=jnp.int32)[None, :]).astype(jnp.int32)
    csum = jnp.cumsum(onehot, axis=0)
    rank = jnp.sum(onehot * (csum - 1), axis=1)
    counts = csum[-1]
    padded = (counts + rb - 1) // rb * rb
    seg_end = jnp.cumsum(padded)
    seg_start = seg_end - padded
    dest = (jnp.sum(onehot * seg_start[None, :], axis=1) + rank).astype(jnp.int32)
    pad_lo = (seg_start + counts).astype(jnp.int32)
    pad_hi = seg_end.astype(jnp.int32)
    per = (padded + MOE_ITEM_ROWS - 1) // MOE_ITEM_ROWS
    item_end = jnp.cumsum(per)
    item_start = item_end - per
    ids = jnp.arange(n_items, dtype=jnp.int32)
    e = jnp.minimum(jnp.searchsorted(item_end, ids, side="right"), N_EXPERTS - 1).astype(jnp.int32)
    valid = ids < item_end[-1]
    piece = ids - item_start[e]
    row0 = seg_start[e] + piece * MOE_ITEM_ROWS
    rows = jnp.clip(padded[e] - piece * MOE_ITEM_ROWS, 0, MOE_ITEM_ROWS)
    nch = jnp.where(valid, rows // rb, 0).astype(jnp.int32)
    last_e = e[jnp.maximum(item_end[-1] - 1, 0)]
    item_e = jnp.where(valid, e, last_e).astype(jnp.int32)
    item_row0 = jnp.where(valid, row0, 0).astype(jnp.int32)
    return dest, pad_lo, pad_hi, (item_e, item_row0, nch, pad_hi[-1:])


def _pad_rows(a, rows):
    return jnp.concatenate([a, jnp.zeros((rows - a.shape[0],) + a.shape[1:], a.dtype)], axis=0)


def kernel(x_prompt, x_sample, cache_k, cache_v, state_ret, page_table, p_prompt, p_sample,
           norm_mix, w_in, q_norm, k_norm, w_o, norm_ffn, w_router, b_router, w_gu, b_gu,
           w_down, b_down, norm_ple, w_ple_gate, w_ple_proj):
    depth = norm_mix.shape[0]
    assert depth == 1
    batch, seq_len, d = x_prompt.shape
    n_dec, dec_seq, _ = x_sample.shape
    assert dec_seq == 1 and n_dec <= SAMPLE_ROWS
    past_len = page_table.shape[1] * cache_k.shape[2]
    ret_w = RET_HEADS * RET_DK
    moba_w = MOBA_HEADS * MOBA_HEAD_DIM
    off = [0, ret_w, 2 * ret_w, 3 * ret_w, 4 * ret_w, 4 * ret_w + moba_w, 4 * ret_w + 2 * moba_w]

    w_in_bf = w_in[0].astype(BF16)
    w_o_bf = w_o[0].astype(BF16)
    wg_ple_bf = w_ple_gate[0].astype(BF16)
    wp_ple_bf = w_ple_proj[0].astype(BF16)
    wr_pad = jnp.concatenate([w_router[0], jnp.zeros((d, LANES - N_EXPERTS), F32)],
                             axis=1).astype(BF16)
    br_pad = jnp.concatenate([b_router[0], jnp.full((LANES - N_EXPERTS,), -jnp.inf, F32)])[None, :]
    qn, kn = q_norm, k_norm

    t_p = batch * seq_len
    xp = x_prompt.reshape(t_p, d)
    tm = 1024
    tables_p = _rope_tables(jnp.arange(seq_len, dtype=jnp.int32))
    z_p = _inproj(xp, norm_mix, w_in_bf, tables_p, qn, kn, seq_len, tm)
    ret_p, state_p = _ret_prompt(z_p, batch, seq_len)
    moba_p = _moba_prompt(z_p, batch, seq_len)

    xs_rows = _pad_rows(x_sample.reshape(n_dec, d), SAMPLE_ROWS)
    tables_s = _rope_tables(jnp.full((SAMPLE_ROWS,), past_len, jnp.int32))
    z_s = _inproj(xs_rows, norm_mix, w_in_bf, tables_s, qn, kn, SAMPLE_ROWS, SAMPLE_ROWS)
    zs = z_s[:n_dec]
    ret_s, state_s = _ret_sample(zs[:, off[0]:off[1]], zs[:, off[1]:off[2]], zs[:, off[2]:off[3]],
                                 zs[:, off[3]:off[4]], state_ret[0])
    heads = lambda a: a.reshape(n_dec, MOBA_HEADS, MOBA_HEAD_DIM)
    mk_s, mv_s = zs[:, off[5]:off[6]], zs[:, off[6]:]
    pool_shape = cache_k.shape[1:]
    moba_s = _moba_sample(heads(zs[:, off[4]:off[5]]), heads(mk_s), heads(mv_s),
                          cache_k.reshape(pool_shape), cache_v.reshape(pool_shape), page_table)

    h_p, hn_p, idx_p, gate_p = _outproj(xp, ret_p, moba_p, w_o_bf, norm_ffn, wr_pad, br_pad, 256)
    h_s, hn_s, idx_s, gate_s = _outproj(xs_rows, _pad_rows(ret_s, SAMPLE_ROWS),
                                        _pad_rows(moba_s, SAMPLE_ROWS), w_o_bf, norm_ffn,
                                        wr_pad, br_pad, SAMPLE_ROWS)

    n_tok = t_p + n_dec
    top_idx = jnp.concatenate([idx_p[:, :TOP_K], idx_s[:n_dec, :TOP_K]], axis=0)
    n_assign = n_tok * TOP_K
    n_chunks = n_assign // MOE_ROW_BLOCK + N_EXPERTS
    n_slots = n_chunks * MOE_ROW_BLOCK
    n_items = N_EXPERTS + pl.cdiv(n_assign, MOE_ITEM_ROWS)
    dest, pad_lo, pad_hi, items = _route(top_idx, n_items)
    hn3 = jnp.concatenate([hn_p, hn_s[:n_dec]], axis=0).reshape(n_tok, ROW_PIECES, d // ROW_PIECES)
    xs3 = _dispatch(hn3, dest, pad_lo, pad_hi, n_slots)
    out2 = _moe(xs3, items, w_gu[0], b_gu[0], w_down[0], b_down[0])
    dest_pad = jnp.concatenate([dest, jnp.zeros(((SAMPLE_ROWS - n_dec) * TOP_K,), jnp.int32)])
    h2_p = _combine(dest_pad, h_p, gate_p, out2, 0, COMBINE_TOKENS)
    h2_s = _combine(dest_pad, h_s, gate_s, out2, t_p, SAMPLE_ROWS)

    y_p = _ple(h2_p, p_prompt[0].reshape(t_p, -1), norm_ple, wg_ple_bf, wp_ple_bf, 512)
    y_s = _ple(h2_s, _pad_rows(p_sample[0].reshape(n_dec, -1), SAMPLE_ROWS), norm_ple,
               wg_ple_bf, wp_ple_bf, SAMPLE_ROWS)

    kv = lambda a, n, l: a.reshape(1, n, l, MOBA_HEADS, MOBA_HEAD_DIM)
    return (y_p.reshape(batch, seq_len, d), y_s[:n_dec].reshape(n_dec, 1, d),
            kv(z_p[:, off[5]:off[6]], batch, seq_len), kv(z_p[:, off[6]:], batch, seq_len),
            state_p[None], kv(mk_s, n_dec, 1), kv(mv_s, n_dec, 1), state_s[None])
```

```python
import functools
import math

import jax
import jax.numpy as jnp
from jax import lax
from jax.experimental import pallas as pl
from jax.experimental.pallas import tpu as pltpu

F32 = jnp.float32
BF16 = jnp.bfloat16

RET_HEADS = 4
RET_DK = 256
RET_DV = 256
RET_ROPE_THETA = 10000.0
MOBA_HEADS = 8
MOBA_HEAD_DIM = 128
MOBA_BLOCK = 256
MOBA_TOPK = 3
ROPE_THETA = 500000.0
ROPE_DIM = MOBA_HEAD_DIM // 4
N_EXPERTS = 32
TOP_K = 4
SWIGLU_LIMIT = 7.0
SWIGLU_ALPHA = 1.702
EPS = 1e-6

LANES = 128
SUBLANES = 8
VMEM_LIMIT_BYTES = 56 * 1024 * 1024

SAMPLE_ROWS = 16
RET_CHUNK = 256
MOE_ROW_BLOCK = 128
MOE_MATMUL_ROWS = 256
MOE_ITEM_ROWS = 1536
MOE_FF_TILE = 512
MOE_VMEM_LIMIT_BYTES = 60 * 1024 * 1024
MOE_WEIGHT_STREAMS = 4
ROW_PIECES = 16
COMBINE_TOKENS = 128
MOBA_KV_GROUP = 4
MOBA_KV_STREAMS = 4


def _cparams(semantics, vmem=VMEM_LIMIT_BYTES):
    return pltpu.CompilerParams(dimension_semantics=semantics, vmem_limit_bytes=vmem)


def _nt_dot(a, b, **kw):
    return lax.dot_general(a, b, (((1,), (1,)), ((), ())), preferred_element_type=F32, **kw)


def _rms(x):
    return x * lax.rsqrt(jnp.mean(x * x, axis=-1, keepdims=True) + EPS)


def _rnd(a):
    return a.astype(BF16).astype(F32)


_IN_TN = 512
IN_WEIGHT_STREAMS = 4


def _inproj_kernel(x_ref, g_ref, *refs):
    w_refs = refs[:IN_WEIGHT_STREAMS]
    cr_ref, sr_ref, cm_ref, sa_ref, sb_ref, qn_ref, kn_ref, z_ref, xn_ref = refs[IN_WEIGHT_STREAMS:]
    j = pl.program_id(1)
    n_ret = RET_HEADS * RET_DK // _IN_TN
    n_moba = MOBA_HEADS * MOBA_HEAD_DIM // _IN_TN
    moba0 = 4 * n_ret

    @pl.when(j == 0)
    def _():
        xn_ref[...] = (_rms(x_ref[...]) * g_ref[...]).astype(BF16)

    kq = xn_ref.shape[1] // IN_WEIGHT_STREAMS
    acc = sum(jnp.dot(xn_ref[:, q * kq:(q + 1) * kq], w_refs[q][...], preferred_element_type=F32)
              for q in range(IN_WEIGHT_STREAMS))

    @pl.when(j < 2 * n_ret)
    def _():
        half = RET_DK // 2
        c = cr_ref[...]
        s = sr_ref[...]
        scale = jnp.where(j < n_ret, 1.0, RET_DK ** -0.5).astype(F32)
        for hh in range(_IN_TN // RET_DK):
            x1 = acc[:, hh * RET_DK:hh * RET_DK + half]
            x2 = acc[:, hh * RET_DK + half:(hh + 1) * RET_DK]
            z_ref[:, hh * RET_DK:hh * RET_DK + half] = (x1 * c - x2 * s) * scale
            z_ref[:, hh * RET_DK + half:(hh + 1) * RET_DK] = (x2 * c + x1 * s) * scale

    @pl.when(((j >= 2 * n_ret) & (j < moba0)) | (j >= moba0 + 2 * n_moba))
    def _():
        z_ref[...] = acc

    @pl.when((j >= moba0) & (j < moba0 + 2 * n_moba))
    def _():
        gain = jnp.where(j < moba0 + n_moba, qn_ref[...], kn_ref[...])
        for hh in range(_IN_TN // MOBA_HEAD_DIM):
            t = _rms(acc[:, hh * MOBA_HEAD_DIM:(hh + 1) * MOBA_HEAD_DIM]) * gain
            up = pltpu.roll(t, MOBA_HEAD_DIM - ROPE_DIM // 2, 1)
            dn = pltpu.roll(t, ROPE_DIM // 2, 1)
            z_ref[:, hh * MOBA_HEAD_DIM:(hh + 1) * MOBA_HEAD_DIM] = (
                t * cm_ref[...] + up * sa_ref[...] + dn * sb_ref[...])


def _rope_tables(pos):
    posf = pos.astype(F32)[:, None]
    half = RET_DK // 2
    inv = RET_ROPE_THETA ** (-2.0 * jnp.arange(half, dtype=F32) / RET_DK)
    ang = posf * inv[None, :]
    cr, sr = jnp.cos(ang), jnp.sin(ang)
    mh = ROPE_DIM // 2
    inv_m = ROPE_THETA ** (-2.0 * jnp.arange(mh, dtype=F32) / ROPE_DIM)
    ang_m = posf * inv_m[None, :]
    cos_m, sin_m = jnp.cos(ang_m), jnp.sin(ang_m)
    n = pos.shape[0]
    rest = MOBA_HEAD_DIM - ROPE_DIM
    cm = jnp.concatenate([cos_m, cos_m, jnp.ones((n, rest), F32)], axis=1)
    sa = jnp.concatenate([-sin_m, jnp.zeros((n, MOBA_HEAD_DIM - mh), F32)], axis=1)
    sb = jnp.concatenate([jnp.zeros((n, mh), F32), sin_m, jnp.zeros((n, rest), F32)], axis=1)
    return cr, sr, cm, sa, sb


def _inproj(x2d, norm, w_bf, tables, qn, kn, seq_len, tm):
    t, d = x2d.shape
    n_out = w_bf.shape[1]
    tpos = seq_len // tm
    cr, sr, cm, sa, sb = tables
    tab = pl.BlockSpec((tm, LANES), lambda i, j: (i % tpos, 0))
    vec = pl.BlockSpec((1, LANES), lambda i, j: (0, 0))
    return pl.pallas_call(
        _inproj_kernel,
        grid=(t // tm, n_out // _IN_TN),
        in_specs=[
            pl.BlockSpec((tm, d), lambda i, j: (i, 0)),
            pl.BlockSpec((1, d), lambda i, j: (0, 0)),
            *[pl.BlockSpec((d // IN_WEIGHT_STREAMS, _IN_TN), (lambda i, j, q=q: (q, j)))
              for q in range(IN_WEIGHT_STREAMS)],
            tab, tab, tab, tab, tab, vec, vec,
        ],
        out_specs=pl.BlockSpec((tm, _IN_TN), lambda i, j: (i, j)),
        out_shape=jax.ShapeDtypeStruct((t, n_out), F32),
        scratch_shapes=[pltpu.VMEM((tm, d), BF16)],
        compiler_params=_cparams(("parallel", "arbitrary")),
        name="inproj",
    )(x2d, norm, *([w_bf] * IN_WEIGHT_STREAMS), cr, sr, cm, sa, sb, qn, kn)


def _ret_gate(o, g):
    return _rms(o) * (g * jax.nn.sigmoid(g))


def _ret_prompt_kernel(logg_ref, q_ref, k_ref, v_ref, g_ref, o_ref, st_ref, r_ref):
    h = pl.program_id(1)
    c = pl.program_id(2)
    n = q_ref.shape[0]

    @pl.when(c == 0)
    def _():
        r_ref[...] = jnp.zeros_like(r_ref)

    lg = logg_ref[h]
    i = lax.broadcasted_iota(jnp.int32, (n, 1), 0).astype(F32)
    jj = lax.broadcasted_iota(jnp.int32, (1, n), 1).astype(F32)
    diff = i - jj
    inner = jnp.where(diff >= 0, jnp.exp(lg * jnp.maximum(diff, 0.0)), 0.0)
    q_dec = jnp.exp(lg * (i + 1.0))
    k_dec = jnp.exp(lg * (n - 1.0 - i))
    chunk_dec = jnp.exp(lg * jnp.full((1, 1), float(n), F32))

    q = q_ref[...]
    k = k_ref[...]
    vb = v_ref[...].astype(BF16)
    r = r_ref[...]
    s = _nt_dot(q.astype(BF16), k.astype(BF16)) * inner
    o = (jnp.dot(s.astype(BF16), vb, preferred_element_type=F32)
         + jnp.dot((q * q_dec).astype(BF16), r.astype(BF16), preferred_element_type=F32))
    kd_t = (k * k_dec).T.astype(BF16)
    r_new = r * chunk_dec + jnp.dot(kd_t, vb, preferred_element_type=F32)
    r_ref[...] = r_new
    o_ref[...] = _ret_gate(o, g_ref[...])

    @pl.when(c == pl.num_programs(2) - 1)
    def _():
        st_ref[0, 0] = r_new


def _ret_log_decay():
    return jnp.log1p(-jnp.exp2(-5.0 - jnp.arange(RET_HEADS, dtype=F32)))


def _ret_prompt(z, batch, seq_len):
    nc = seq_len // RET_CHUNK
    w = RET_DK

    def col(off):
        return pl.BlockSpec((RET_CHUNK, w), lambda b, h, c: (b * nc + c, off + h))

    return pl.pallas_call(
        _ret_prompt_kernel,
        grid=(batch, RET_HEADS, nc),
        in_specs=[pl.BlockSpec(memory_space=pltpu.SMEM),
                  col(0), col(RET_HEADS), col(2 * RET_HEADS), col(3 * RET_HEADS)],
        out_specs=[
            pl.BlockSpec((RET_CHUNK, RET_DV), lambda b, h, c: (b * nc + c, h)),
            pl.BlockSpec((1, 1, RET_DK, RET_DV), lambda b, h, c: (b, h, 0, 0)),
        ],
        scratch_shapes=[pltpu.VMEM((RET_DK, RET_DV), F32)],
        out_shape=[
            jax.ShapeDtypeStruct((batch * seq_len, RET_HEADS * RET_DV), F32),
            jax.ShapeDtypeStruct((batch, RET_HEADS, RET_DK, RET_DV), F32),
        ],
        compiler_params=_cparams(("parallel", "parallel", "arbitrary")),
        name="ret_prompt",
    )(_ret_log_decay(), z, z, z, z)


def _ret_sample_kernel(logg_ref, q_ref, kc_ref, kr_ref, v_ref, g_ref, s0_ref, o_ref, st_ref):
    h = pl.program_id(1)
    dec = jnp.exp(logg_ref[h] * jnp.ones((1, 1), F32))
    q = q_ref[0, 0]
    v = v_ref[0, 0]
    r0 = s0_ref[0, 0]
    qk = jnp.sum(q * kr_ref[0, 0], axis=-1, keepdims=True)
    q8 = jnp.broadcast_to(q * dec, (SUBLANES, RET_DK)).astype(BF16)
    qr = jnp.dot(q8, r0.astype(BF16), preferred_element_type=F32)[0:1]
    o = qk * v + qr
    st_ref[0, 0] = r0 * dec + kc_ref[0, 0] * v
    o_ref[0, 0] = _ret_gate(o, g_ref[0, 0])


def _ret_sample(q, k, v, g, state):
    n = q.shape[0]
    row = lambda a: a.reshape(n, RET_HEADS, 1, RET_DK)
    rspec = pl.BlockSpec((1, 1, 1, RET_DK), lambda b, h: (b, h, 0, 0))
    mspec = pl.BlockSpec((1, 1, RET_DK, RET_DV), lambda b, h: (b, h, 0, 0))
    o, st = pl.pallas_call(
        _ret_sample_kernel,
        grid=(n, RET_HEADS),
        in_specs=[pl.BlockSpec(memory_space=pltpu.SMEM),
                  rspec, pl.BlockSpec((1, 1, RET_DK, 1), lambda b, h: (b, h, 0, 0)),
                  rspec, rspec, rspec, mspec],
        out_specs=[rspec, mspec],
        out_shape=[jax.ShapeDtypeStruct((n, RET_HEADS, 1, RET_DV), F32),
                   jax.ShapeDtypeStruct(state.shape, F32)],
        compiler_params=_cparams(("parallel", "parallel")),
        name="ret_sample",
    )(_ret_log_decay(), row(q), k.reshape(n, RET_HEADS, RET_DK, 1), row(k), row(v), row(g), state)
    return o.reshape(n, RET_HEADS * RET_DV), st


def _moba_prompt_kernel(q_ref, *refs):
    ns = MOBA_KV_STREAMS
    k_refs, v_refs = refs[:ns], refs[ns:2 * ns]
    o_ref, kb_ref, vt_ref, km_ref, sel_ref = refs[2 * ns:]
    qi = pl.program_id(2)
    blk = MOBA_BLOCK
    nb = kb_ref.shape[0] // blk
    per = nb // ns
    scale = MOBA_HEAD_DIM ** -0.5

    @pl.when(qi == 0)
    def _():
        for n in range(nb):
            rows = slice(n * blk, (n + 1) * blk)
            local = slice((n % per) * blk, (n % per + 1) * blk)
            kf = k_refs[n // per][local, :]
            kb_ref[rows, :] = kf.astype(BF16)
            vt_ref[:, rows] = v_refs[n // per][local, :].T.astype(BF16)
            km_ref[n:n + 1, :] = jnp.mean(kf, axis=0, keepdims=True)

    qb = q_ref[...].astype(BF16)
    gate = _nt_dot(km_ref[...].astype(BF16), qb)
    kblock = lax.broadcasted_iota(jnp.int32, (nb, blk), 0)
    past = kblock < qi
    g = jnp.where(past, gate, -jnp.inf)
    cnt = jnp.zeros((nb, blk), F32)
    for m in range(nb):
        gm = g[m:m + 1, :]
        cnt = cnt + jnp.where(gm > g, 1.0, jnp.where((gm == g) & (kblock > m), 1.0, 0.0))
    sel_ref[...] = jnp.where(past & (cnt < MOBA_TOPK), 1.0, 0.0)

    own = pl.ds(pl.multiple_of(qi * blk, blk), blk)
    s = _nt_dot(kb_ref[own, :], qb) * scale
    key = lax.broadcasted_iota(jnp.int32, (blk, blk), 0)
    qry = lax.broadcasted_iota(jnp.int32, (blk, blk), 1)
    s = jnp.where(key <= qry, s, -jnp.inf)
    m0 = jnp.max(s, axis=0, keepdims=True)
    p = jnp.exp(s - m0)
    l0 = jnp.sum(p, axis=0, keepdims=True)
    a0 = jnp.dot(vt_ref[:, own], p.astype(BF16), preferred_element_type=F32)

    grp = MOBA_KV_GROUP
    span = grp * blk
    last = nb // grp - 1

    def scores(j):
        rows = pl.ds(pl.multiple_of(j * span, span), span)
        return _nt_dot(kb_ref[rows, :], qb)

    def body(j, carry):
        m_i, l_i, a_i, s_cur = carry
        s_next = scores(jnp.minimum(j + 1, last))
        rows = pl.ds(pl.multiple_of(j * span, span), span)
        allowed = jnp.concatenate(
            [jnp.broadcast_to(sel_ref[pl.ds(j * grp + t, 1), :], (blk, blk)) for t in range(grp)],
            axis=0)
        sn = jnp.where(allowed > 0.0, s_cur * scale, -jnp.inf)
        m_new = jnp.maximum(m_i, jnp.max(sn, axis=0, keepdims=True))
        alpha = jnp.exp(m_i - m_new)
        pn = jnp.exp(sn - m_new)
        l_new = alpha * l_i + jnp.sum(pn, axis=0, keepdims=True)
        a_new = alpha * a_i + jnp.dot(vt_ref[:, rows], pn.astype(BF16), preferred_element_type=F32)
        return m_new, l_new, a_new, s_next

    _, l_f, a_f, _ = lax.fori_loop(0, (qi + grp - 1) // grp, body, (m0, l0, a0, scores(0)))
    o_ref[...] = (a_f / l_f).T


def _moba_prompt(z, batch, seq_len):
    nq = seq_len // MOBA_BLOCK
    hd = MOBA_HEAD_DIM
    qoff = (2 * RET_HEADS * RET_DK + 2 * RET_HEADS * RET_DV) // hd
    koff = qoff + MOBA_HEADS
    voff = koff + MOBA_HEADS
    ns = MOBA_KV_STREAMS
    assert nq % ns == 0 and nq % MOBA_KV_GROUP == 0
    slabs = lambda off: [pl.BlockSpec((seq_len // ns, hd), (lambda b, h, i, s=s: (b * ns + s, off + h)))
                         for s in range(ns)]
    return pl.pallas_call(
        _moba_prompt_kernel,
        grid=(batch, MOBA_HEADS, nq),
        in_specs=[pl.BlockSpec((MOBA_BLOCK, hd), lambda b, h, i: (b * nq + i, qoff + h))]
        + slabs(koff) + slabs(voff),
        out_specs=pl.BlockSpec((MOBA_BLOCK, hd), lambda b, h, i: (b * nq + i, h)),
        out_shape=jax.ShapeDtypeStruct((batch * seq_len, MOBA_HEADS * hd), F32),
        scratch_shapes=[pltpu.VMEM((seq_len, hd), BF16), pltpu.VMEM((hd, seq_len), BF16),
                        pltpu.VMEM((nq, hd), F32), pltpu.VMEM((nq, MOBA_BLOCK), F32)],
        compiler_params=_cparams(("parallel", "parallel", "arbitrary")),
        name="moba_prompt",
    )(z, *([z] * (2 * ns)))


MOBA_SCAN_PAGES = 8


def _moba_scan_kernel(pt_ref, q_ref, *refs):
    k_refs = refs[:MOBA_SCAN_PAGES]
    sel_ref, gate_ref = refs[MOBA_SCAN_PAGES:]
    p = pl.program_id(1)
    ppb = MOBA_BLOCK // k_refs[0].shape[1]
    blocks_per_step = MOBA_SCAN_PAGES // ppb

    @pl.when(p == 0)
    def _():
        gate_ref[...] = jnp.zeros_like(gate_ref)

    q = _rnd(q_ref[0])
    lane = lax.broadcasted_iota(jnp.int32, gate_ref.shape, 1)
    gate = gate_ref[...]
    for jb in range(blocks_per_step):
        k_sum = sum(jnp.sum(k_refs[jb * ppb + j][0], axis=0) for j in range(ppb))
        g = jnp.sum(q * _rnd(k_sum * (1.0 / MOBA_BLOCK)), axis=1, keepdims=True)
        gate = jnp.where(lane == p * blocks_per_step + jb, g, gate)
    gate_ref[...] = gate

    @pl.when(p == pl.num_programs(1) - 1)
    def _():
        g = jnp.where(lane < pl.num_programs(1) * blocks_per_step, gate, -jnp.inf)
        out = jnp.zeros(gate_ref.shape, jnp.int32)
        for t in range(MOBA_TOPK):
            mx = jnp.max(g, axis=1, keepdims=True)
            idx = jnp.min(jnp.where(g == mx, lane, LANES), axis=1, keepdims=True)
            out = jnp.where(lane == t, idx, out)
            g = jnp.where(lane == idx, -jnp.inf, g)
        sel_ref[0] = out


def _moba_attend_kernel(pt_ref, selp_ref, q_ref, kn_ref, vn_ref, *refs):
    nsrc = (len(refs) - 1) // 2
    k_refs = refs[:nsrc]
    v_refs = refs[nsrc:2 * nsrc]
    o_ref = refs[2 * nsrc]
    h = pl.program_id(1)
    scale = MOBA_HEAD_DIM ** -0.5
    mine2 = lax.broadcasted_iota(jnp.int32, (MOBA_HEADS, MOBA_HEAD_DIM), 0) == h
    mine3 = lax.broadcasted_iota(jnp.int32, (1, MOBA_HEADS, 1), 1) == h
    all2 = lambda a, op: op(op(a, axis=1, keepdims=True), axis=0, keepdims=True)
    all3 = lambda a, op: op(op(a, axis=0, keepdims=True), axis=1, keepdims=True)
    q = _rnd(q_ref[0])
    s_new = all2(jnp.where(mine2, q * _rnd(kn_ref[0]), 0.0), jnp.sum) * scale
    ss = [jnp.where(mine3, jnp.sum(_rnd(kr[0]) * q[None], axis=-1, keepdims=True) * scale, -jnp.inf)
          for kr in k_refs]
    mx = s_new.reshape(1, 1, 1)
    for sj in ss:
        mx = jnp.maximum(mx, all3(sj, jnp.max))
    p_new = jnp.exp(s_new.reshape(1, 1, 1) - mx)
    ps = [jnp.exp(sj - mx) for sj in ss]
    den = p_new
    for pj in ps:
        den = den + all3(pj, jnp.sum)
    inv = 1.0 / den
    acc = jnp.where(mine2, _rnd(p_new * inv).reshape(1, 1) * _rnd(vn_ref[0]), 0.0)
    for pj, vr in zip(ps, v_refs):
        acc = acc + jnp.sum(_rnd(pj * inv) * _rnd(vr[0]), axis=0)
    o_ref[0, 0] = jnp.sum(acc, axis=0, keepdims=True)


def _moba_sample(q, k_new, v_new, cache_k, cache_v, page_table):
    n, npages = page_table.shape
    page = cache_k.shape[1]
    hd = MOBA_HEAD_DIM
    ppb = MOBA_BLOCK // page
    pp = MOBA_SCAN_PAGES
    assert npages % pp == 0 and npages // ppb <= LANES
    page_block = (1, page, MOBA_HEADS, hd)
    tok2 = pl.BlockSpec((1, MOBA_HEADS, hd), lambda b, p, pt: (b, 0, 0))
    scan_specs = [pl.BlockSpec(page_block, (lambda b, p, pt, j=j: (pt[b, p * pp + j], 0, 0, 0)))
                  for j in range(pp)]
    sel = pl.pallas_call(
        _moba_scan_kernel,
        grid_spec=pltpu.PrefetchScalarGridSpec(
            num_scalar_prefetch=1,
            grid=(n, npages // pp),
            in_specs=[tok2] + scan_specs,
            out_specs=pl.BlockSpec((1, MOBA_HEADS, LANES), lambda b, p, pt: (b, 0, 0)),
            scratch_shapes=[pltpu.VMEM((MOBA_HEADS, LANES), F32)],
        ),
        out_shape=jax.ShapeDtypeStruct((n, MOBA_HEADS, LANES), jnp.int32),
        compiler_params=_cparams(("parallel", "arbitrary")),
        name="moba_scan",
    )(page_table, q, *([cache_k] * pp))
    selp = (sel[:, :, :MOBA_TOPK, None] * ppb + jnp.arange(ppb, dtype=jnp.int32)).reshape(-1)
    nsrc = MOBA_TOPK * ppb

    def page_spec(j):
        def index(b, h, pt, sp):
            return (pt[b, sp[(b * MOBA_HEADS + h) * nsrc + j]], 0, 0, 0)
        return pl.BlockSpec(page_block, index)

    tok = pl.BlockSpec((1, MOBA_HEADS, hd), lambda b, h, pt, sp: (b, 0, 0))
    pages = [page_spec(j) for j in range(nsrc)]
    out = pl.pallas_call(
        _moba_attend_kernel,
        grid_spec=pltpu.PrefetchScalarGridSpec(
            num_scalar_prefetch=2,
            grid=(n, MOBA_HEADS),
            in_specs=[tok, tok, tok] + pages + pages,
            out_specs=pl.BlockSpec((1, 1, 1, hd), lambda b, h, pt, sp: (b, h, 0, 0)),
        ),
        out_shape=jax.ShapeDtypeStruct((n, MOBA_HEADS, 1, hd), F32),
        compiler_params=_cparams(("parallel", "parallel")),
        name="moba_attend",
    )(page_table, selp, q, k_new, v_new, *([cache_k] * nsrc), *([cache_v] * nsrc))
    return out.reshape(n, MOBA_HEADS * hd)


def _outproj_kernel(x_ref, ret_ref, moba_ref, wo_ref, nf_ref, wr_ref, br_ref,
                    h_ref, hn_ref, idx_ref, gate_ref):
    rw = ret_ref.shape[1]
    h = (x_ref[...]
         + jnp.dot(ret_ref[...].astype(BF16), wo_ref[:rw, :], preferred_element_type=F32)
         + jnp.dot(moba_ref[...].astype(BF16), wo_ref[rw:, :], preferred_element_type=F32))
    h_ref[...] = h
    hn = _rms(h) * nf_ref[...]
    hn_ref[...] = hn
    logits = jnp.dot(hn.astype(BF16), wr_ref[...], preferred_element_type=F32) + br_ref[...]
    lane = lax.broadcasted_iota(jnp.int32, logits.shape, 1)
    vals, idxs = [], []
    for _ in range(TOP_K):
        mx = jnp.max(logits, axis=1, keepdims=True)
        ix = jnp.min(jnp.where(logits == mx, lane, LANES), axis=1, keepdims=True)
        vals.append(mx)
        idxs.append(ix)
        logits = jnp.where(lane == ix, -jnp.inf, logits)
    es = [jnp.exp(v - vals[0]) for v in vals]
    den = es[0] + es[1] + es[2] + es[3]
    idx_out = jnp.zeros(lane.shape, jnp.int32)
    gate_out = jnp.zeros(lane.shape, F32)
    for t in range(TOP_K):
        idx_out = jnp.where(lane == t, idxs[t], idx_out)
        gate_out = jnp.where(lane == t, es[t] / den, gate_out)
    idx_ref[...] = idx_out
    gate_ref[...] = gate_out


def _outproj(x2d, ret, moba, wo_bf, norm_ffn, wr_pad, br_pad, tm):
    t, d = x2d.shape
    rw, mw = ret.shape[1], moba.shape[1]
    const = lambda shape: pl.BlockSpec(shape, lambda i: (0, 0))
    rows = lambda w: pl.BlockSpec((tm, w), lambda i: (i, 0))
    return pl.pallas_call(
        _outproj_kernel,
        grid=(t // tm,),
        in_specs=[rows(d), rows(rw), rows(mw), const((rw + mw, d)), const((1, d)),
                  const((d, LANES)), const((1, LANES))],
        out_specs=[rows(d), rows(d), rows(LANES), rows(LANES)],
        out_shape=[jax.ShapeDtypeStruct((t, d), F32), jax.ShapeDtypeStruct((t, d), F32),
                   jax.ShapeDtypeStruct((t, LANES), jnp.int32),
                   jax.ShapeDtypeStruct((t, LANES), F32)],
        compiler_params=_cparams(("parallel",)),
        name="outproj",
    )(x2d, ret, moba, wo_bf, norm_ffn, wr_pad, br_pad)


def _dispatch_kernel(dest_ref, pad_lo_ref, pad_hi_ref, hn_ref, xs_hbm, zero_ref, sem, zsem, *,
                     ntok):
    i = pl.program_id(0)
    tb = hn_ref.shape[0]
    first = i * tb
    count = jnp.minimum(tb, ntok - first)

    def row_copy(r, k):
        return pltpu.make_async_copy(hn_ref.at[r], xs_hbm.at[dest_ref[(first + r) * TOP_K + k]], sem)

    def issue(r, c):
        for k in range(TOP_K):
            row_copy(r, k).start()
        return c

    lax.fori_loop(0, count, issue, 0)

    @pl.when(i == 0)
    def _():
        zero_ref[...] = jnp.zeros_like(zero_ref)

        def pad_copy(s):
            return pltpu.make_async_copy(zero_ref.at[0], xs_hbm.at[s], zsem)

        def per_expert(e, c):
            def pad_row(s, c2):
                pad_copy(s).start()
                return c2
            lax.fori_loop(pad_lo_ref[e], pad_hi_ref[e], pad_row, 0)
            return c

        lax.fori_loop(0, N_EXPERTS, per_expert, 0)

        rb = zero_ref.shape[0]
        tail0 = pad_hi_ref[N_EXPERTS - 1]
        n_tail = (xs_hbm.shape[0] - tail0) // rb

        def tail_copy(c):
            rows = pl.ds(pl.multiple_of(tail0 + c * rb, rb), rb)
            return pltpu.make_async_copy(zero_ref, xs_hbm.at[rows], zsem)

        def tail_start(c, c2):
            tail_copy(c).start()
            return c2

        lax.fori_loop(0, n_tail, tail_start, 0)

        def per_expert_wait(e, c):
            def pad_wait(s, c2):
                pad_copy(s).wait()
                return c2
            lax.fori_loop(pad_lo_ref[e], pad_hi_ref[e], pad_wait, 0)
            return c

        lax.fori_loop(0, N_EXPERTS, per_expert_wait, 0)

        def tail_wait(c, c2):
            tail_copy(c).wait()
            return c2

        lax.fori_loop(0, n_tail, tail_wait, 0)

    @pl.when(count == tb)
    def _():
        for k in range(TOP_K):
            pltpu.make_async_copy(hn_ref, xs_hbm.at[pl.ds(0, tb)], sem).wait()

    @pl.when(count < tb)
    def _():
        def drain(r, c):
            for k in range(TOP_K):
                row_copy(r, k).wait()
            return c

        lax.fori_loop(0, count, drain, 0)


def _dispatch(hn3, dest_flat, pad_lo, pad_hi, n_slots):
    ntok = hn3.shape[0]
    tb = COMBINE_TOKENS
    return pl.pallas_call(
        functools.partial(_dispatch_kernel, ntok=ntok),
        grid_spec=pltpu.PrefetchScalarGridSpec(
            num_scalar_prefetch=3,
            grid=(pl.cdiv(ntok, tb),),
            in_specs=[pl.BlockSpec((tb,) + hn3.shape[1:], lambda i, d, lo, hi: (i, 0, 0))],
            out_specs=pl.BlockSpec(memory_space=pl.ANY),
            scratch_shapes=[pltpu.VMEM((MOE_ROW_BLOCK,) + hn3.shape[1:], F32),
                            pltpu.SemaphoreType.DMA(()), pltpu.SemaphoreType.DMA(())],
        ),
        out_shape=jax.ShapeDtypeStruct((n_slots,) + hn3.shape[1:], F32),
        compiler_params=_cparams(("arbitrary",)),
        name="moe_dispatch",
    )(dest_flat, pad_lo, pad_hi, hn3)


def _moe_kernel(item_e, item_row0, item_nch, tail_ref, xs_hbm, *refs):
    ns = MOE_WEIGHT_STREAMS
    wg_refs, wu_refs, wd_refs = refs[:ns], refs[ns:2 * ns], refs[2 * ns:3 * ns]
    (bg_ref, bu_ref, bd_ref, out_hbm, x_ref, acc_ref, stage_ref, wgb_ref, wub_ref, wdb_ref,
     sem_in, sem_out) = refs[3 * ns:]
    it = pl.program_id(0)
    f = pl.program_id(1)
    nf = pl.num_programs(1)
    rb = MOE_ROW_BLOCK
    big = MOE_MATMUL_ROWS
    pieces = ROW_PIECES
    nch = item_nch[it]
    row0 = item_row0[it]
    n_big = nch // (big // rb)
    has_tail = nch % (big // rb) != 0
    tail_row = pl.multiple_of(n_big * big, rb)

    def hbm_rows(start, size):
        return pl.ds(pl.multiple_of((row0 + start) * pieces, rb * pieces), size * pieces)

    def in_copy(start, size, slot):
        return pltpu.make_async_copy(xs_hbm.at[hbm_rows(start, size)],
                                     stage_ref.at[slot, pl.ds(0, size * pieces)], sem_in.at[slot])

    def out_copy(start, size, slot):
        return pltpu.make_async_copy(stage_ref.at[slot, pl.ds(0, size * pieces)],
                                     out_hbm.at[hbm_rows(start, size)], sem_out.at[slot])

    def stage_to_x(start, size, slot):
        rows = pl.ds(start, size)
        for p in range(pieces):
            piece = stage_ref[slot, pl.ds(p, size, stride=pieces), :]
            x_ref[rows, p * LANES:(p + 1) * LANES] = piece.astype(BF16)
        acc_ref[rows, :] = jnp.broadcast_to(bd_ref[0], (size, acc_ref.shape[1]))

    def acc_to_stage(start, size, slot):
        rows = pl.ds(start, size)
        for p in range(pieces):
            stage_ref[slot, pl.ds(p, size, stride=pieces), :] = acc_ref[rows, p * LANES:(p + 1) * LANES]

    @pl.when(nch > 0)
    def _():
        for src, dst in ((wg_refs, wgb_ref), (wu_refs, wub_ref), (wd_refs, wdb_ref)):
            rows = dst.shape[0] // ns
            for q in range(ns):
                dst[q * rows:(q + 1) * rows, :] = src[q][0].astype(BF16)
        bg = bg_ref[0]
        bu = bu_ref[0]

        def compute(start, size):
            rows = pl.ds(start, size)
            x = x_ref[rows, :]
            g = jnp.dot(x, wgb_ref[...], preferred_element_type=F32) + bg
            u = jnp.dot(x, wub_ref[...], preferred_element_type=F32) + bu
            g = jnp.minimum(g, SWIGLU_LIMIT)
            u = jnp.clip(u, -SWIGLU_LIMIT, SWIGLU_LIMIT)
            act = (u + 1.0) * g * jax.nn.sigmoid(SWIGLU_ALPHA * g)
            acc_ref[rows, :] += jnp.dot(act.astype(BF16), wdb_ref[...], preferred_element_type=F32)

        @pl.when(f == 0)
        def _():
            @pl.when(n_big > 0)
            def _():
                in_copy(0, big, 0).start()

            def step(r, c):
                slot = r % 2
                start = pl.multiple_of(r * big, big)

                @pl.when(r + 1 < n_big)
                def _():
                    in_copy(start + big, big, 1 - slot).start()

                in_copy(start, big, slot).wait()
                stage_to_x(start, big, slot)
                compute(start, big)
                return c

            lax.fori_loop(0, n_big, step, 0)

            @pl.when(has_tail)
            def _():
                cp = in_copy(tail_row, rb, 0)
                cp.start()
                cp.wait()
                stage_to_x(tail_row, rb, 0)
                compute(tail_row, rb)

        @pl.when((f > 0) & (f < nf - 1))
        def _():
            def step(r, c):
                compute(pl.multiple_of(r * big, big), big)
                return c

            lax.fori_loop(0, n_big, step, 0)

            @pl.when(has_tail)
            def _():
                compute(tail_row, rb)

        @pl.when(f == nf - 1)
        def _():
            def step(r, c):
                slot = r % 2
                start = pl.multiple_of(r * big, big)
                compute(start, big)

                @pl.when(r >= 2)
                def _():
                    out_copy(start - 2 * big, big, slot).wait()

                acc_to_stage(start, big, slot)
                out_copy(start, big, slot).start()
                return c

            lax.fori_loop(0, n_big, step, 0)

            @pl.when(n_big >= 2)
            def _():
                out_copy(0, big, n_big % 2).wait()

            @pl.when(n_big >= 1)
            def _():
                out_copy(0, big, (n_big - 1) % 2).wait()

            @pl.when(has_tail)
            def _():
                compute(tail_row, rb)
                acc_to_stage(tail_row, rb, 0)
                cp = out_copy(tail_row, rb, 0)
                cp.start()
                cp.wait()

    @pl.when((it == pl.num_programs(0) - 1) & (f == nf - 1))
    def _():
        tail0 = tail_ref[0]
        n_tail = (out_hbm.shape[0] // pieces - tail0) // rb
        stage_ref[0] = jnp.zeros(stage_ref.shape[1:], F32)

        def tail_copy(c):
            rows = pl.ds(pl.multiple_of((tail0 + c * rb) * pieces, rb * pieces), rb * pieces)
            return pltpu.make_async_copy(stage_ref.at[0, pl.ds(0, rb * pieces)], out_hbm.at[rows],
                                         sem_out.at[0])

        def tail_start(c, c2):
            tail_copy(c).start()
            return c2

        def tail_wait(c, c2):
            tail_copy(c).wait()
            return c2

        lax.fori_loop(0, n_tail, tail_start, 0)
        lax.fori_loop(0, n_tail, tail_wait, 0)


def _moe(xs3, items, w_gu, b_gu, w_down, b_down):
    item_e, item_row0, item_nch, tail0 = items
    n_items = item_e.shape[0]
    n_slots = xs3.shape[0]
    d = w_gu.shape[1]
    ff = w_down.shape[1]
    tf = MOE_FF_TILE
    nf = ff // tf
    assert MOE_MATMUL_ROWS == 2 * MOE_ROW_BLOCK and MOE_ITEM_ROWS % MOE_MATMUL_ROWS == 0 and nf >= 2
    xs2 = xs3.reshape(n_slots * ROW_PIECES, LANES)

    def ftile(it, f, nch):
        return jnp.where(nch[it] > 0, f, nf - 1)

    ns = MOE_WEIGHT_STREAMS

    def slab(rows, cols, index):
        return [pl.BlockSpec((1, rows // ns, cols),
                             (lambda it, f, e, r0, nch, t0, q=q: index(e[it], ftile(it, f, nch), q)))
                for q in range(ns)]

    wg_specs = slab(d, tf, lambda e, ft, q: (e, q, ft))
    wu_specs = slab(d, tf, lambda e, ft, q: (e, q, nf + ft))
    wd_specs = slab(tf, d, lambda e, ft, q: (e, ft * ns + q, 0))
    bg_spec = pl.BlockSpec((1, 1, tf), lambda it, f, e, r0, nch, t0: (e[it], 0, ftile(it, f, nch)))
    bu_spec = pl.BlockSpec((1, 1, tf), lambda it, f, e, r0, nch, t0: (e[it], 0, nf + ftile(it, f, nch)))
    bd_spec = pl.BlockSpec((1, 1, d), lambda it, f, e, r0, nch, t0: (e[it], 0, 0))
    b_gu3 = b_gu.reshape(N_EXPERTS, 1, 2 * ff)
    b_down3 = b_down.reshape(N_EXPERTS, 1, d)
    out2 = pl.pallas_call(
        _moe_kernel,
        grid_spec=pltpu.PrefetchScalarGridSpec(
            num_scalar_prefetch=4,
            grid=(n_items, nf),
            in_specs=([pl.BlockSpec(memory_space=pl.ANY)] + wg_specs + wu_specs + wd_specs
                      + [bg_spec, bu_spec, bd_spec]),
            out_specs=pl.BlockSpec(memory_space=pl.ANY),
            scratch_shapes=[
                pltpu.VMEM((MOE_ITEM_ROWS, d), BF16),
                pltpu.VMEM((MOE_ITEM_ROWS, d), F32),
                pltpu.VMEM((2, MOE_MATMUL_ROWS * ROW_PIECES, LANES), F32),
                pltpu.VMEM((d, tf), BF16), pltpu.VMEM((d, tf), BF16), pltpu.VMEM((tf, d), BF16),
                pltpu.SemaphoreType.DMA((2,)), pltpu.SemaphoreType.DMA((2,)),
            ],
        ),
        out_shape=jax.ShapeDtypeStruct(xs2.shape, F32),
        compiler_params=_cparams(("arbitrary", "arbitrary"), MOE_VMEM_LIMIT_BYTES),
        name="moe_experts",
    )(item_e, item_row0, item_nch, tail0, xs2, *([w_gu] * (2 * ns)), *([w_down] * ns),
      b_gu3, b_gu3, b_down3)
    return out2


def _combine_kernel(dest_ref, h_ref, gate_ref, out_hbm, o_ref, buf_ref, sem, *, tok_off):
    i = pl.program_id(0)
    tb = h_ref.shape[0]
    pieces = ROW_PIECES

    def row_copy(step, r, k, slot):
        d = dest_ref[(tok_off + step * tb + r) * TOP_K + k]
        src = out_hbm.at[pl.ds(pl.multiple_of(d * pieces, pieces), pieces)]
        dst = buf_ref.at[slot, pl.ds(pl.multiple_of((k * tb + r) * pieces, pieces), pieces)]
        return pltpu.make_async_copy(src, dst, sem.at[slot])

    def issue(step, slot):
        def body(r, c):
            for k in range(TOP_K):
                row_copy(step, r, k, slot).start()
            return c
        lax.fori_loop(0, tb, body, 0)

    @pl.when(i == 0)
    def _():
        issue(0, 0)

    @pl.when(i + 1 < pl.num_programs(0))
    def _():
        issue(i + 1, (i + 1) % 2)

    slot = i % 2
    pltpu.make_async_copy(out_hbm.at[pl.ds(0, TOP_K * tb * pieces)], buf_ref.at[slot],
                          sem.at[slot]).wait()

    for p in range(pieces):
        f = jnp.zeros((tb, LANES), F32)
        for k in range(TOP_K):
            rows = buf_ref[slot, pl.ds(k * tb * pieces + p, tb, stride=pieces), :]
            f = f + gate_ref[:, k:k + 1] * rows
        o_ref[:, p * LANES:(p + 1) * LANES] = h_ref[:, p * LANES:(p + 1) * LANES] + f


def _combine(dest_flat, h, gates, out2, tok_off, tb):
    t, d = h.shape
    return pl.pallas_call(
        functools.partial(_combine_kernel, tok_off=tok_off),
        grid_spec=pltpu.PrefetchScalarGridSpec(
            num_scalar_prefetch=1,
            grid=(t // tb,),
            in_specs=[pl.BlockSpec((tb, d), lambda i, ds: (i, 0)),
                      pl.BlockSpec((tb, LANES), lambda i, ds: (i, 0)),
                      pl.BlockSpec(memory_space=pl.ANY)],
            out_specs=pl.BlockSpec((tb, d), lambda i, ds: (i, 0)),
            scratch_shapes=[pltpu.VMEM((2, TOP_K * tb * ROW_PIECES, LANES), F32),
                            pltpu.SemaphoreType.DMA((2,))],
        ),
        out_shape=jax.ShapeDtypeStruct((t, d), F32),
        compiler_params=_cparams(("arbitrary",)),
        name="moe_combine",
    )(dest_flat, h, gates, out2)


def _ple_kernel(h_ref, p_ref, n_ref, wg_ref, wp_ref, y_ref):
    h = h_ref[...]
    hn = (_rms(h) * n_ref[...]).astype(BF16)
    gate = jax.nn.sigmoid(jnp.dot(hn, wg_ref[...], preferred_element_type=F32))
    proj = jnp.dot(p_ref[...].astype(BF16), wp_ref[...], preferred_element_type=F32)
    y_ref[...] = h + gate * proj


def _ple(h, p, norm, wg_bf, wp_bf, tm):
    t, d = h.shape
    pd = p.shape[1]
    const = lambda shape: pl.BlockSpec(shape, lambda i: (0, 0))
    return pl.pallas_call(
        _ple_kernel,
        grid=(t // tm,),
        in_specs=[pl.BlockSpec((tm, d), lambda i: (i, 0)), pl.BlockSpec((tm, pd), lambda i: (i, 0)),
                  const((1, d)), const((d, d)), const((pd, d))],
        out_specs=pl.BlockSpec((tm, d), lambda i: (i, 0)),
        out_shape=jax.ShapeDtypeStruct((t, d), F32),
        compiler_params=_cparams(("parallel",)),
        name="ple",
    )(h, p, norm, wg_bf, wp_bf)


def _route(top_idx, n_items):
    rb = MOE_ROW_BLOCK
    flat = top_idx.reshape(-1)
    onehot = (flat[:, None] == jnp.arange(N_EXPERTS, dtype=jnp.int32)[None, :]).astype(jnp.int32)
    csum = jnp.cumsum(onehot, axis=0)
    rank = jnp.sum(onehot * (csum - 1), axis=1)
    counts = csum[-1]
    padded = (counts + rb - 1) // rb * rb
    seg_end = jnp.cumsum(padded)
    seg_start = seg_end - padded
    dest = (jnp.sum(onehot * seg_start[None, :], axis=1) + rank).astype(jnp.int32)
    pad_lo = (seg_start + counts).astype(jnp.int32)
    pad_hi = seg_end.astype(jnp.int32)
    per = (padded + MOE_ITEM_ROWS - 1) // MOE_ITEM_ROWS
    item_end = jnp.cumsum(per)
    item_start = item_end - per
    ids = jnp.arange(n_items, dtype=jnp.int32)
    e = jnp.minimum(jnp.searchsorted(item_end, ids, side="right"), N_EXPERTS - 1).astype(jnp.int32)
    valid = ids < item_end[-1]
    piece = ids - item_start[e]
    row0 = seg_start[e] + piece * MOE_ITEM_ROWS
    rows = jnp.clip(padded[e] - piece * MOE_ITEM_ROWS, 0, MOE_ITEM_ROWS)
    nch = jnp.where(valid, rows // rb, 0).astype(jnp.int32)
    last_e = e[jnp.maximum(item_end[-1] - 1, 0)]
    item_e = jnp.where(valid, e, last_e).astype(jnp.int32)
    item_row0 = jnp.where(valid, row0, 0).astype(jnp.int32)
    return dest, pad_lo, pad_hi, (item_e, item_row0, nch, pad_hi[-1:])


def _pad_rows(a, rows):
    return jnp.concatenate([a, jnp.zeros((rows - a.shape[0],) + a.shape[1:], a.dtype)], axis=0)


def kernel(x_prompt, x_sample, cache_k, cache_v, state_ret, page_table, p_prompt, p_sample,
           norm_mix, w_in, q_norm, k_norm, w_o, norm_ffn, w_router, b_router, w_gu, b_gu,
           w_down, b_down, norm_ple, w_ple_gate, w_ple_proj):
    depth = norm_mix.shape[0]
    assert depth == 1
    batch, seq_len, d = x_prompt.shape
    n_dec, dec_seq, _ = x_sample.shape
    assert dec_seq == 1 and n_dec <= SAMPLE_ROWS
    past_len = page_table.shape[1] * cache_k.shape[2]
    ret_w = RET_HEADS * RET_DK
    moba_w = MOBA_HEADS * MOBA_HEAD_DIM
    off = [0, ret_w, 2 * ret_w, 3 * ret_w, 4 * ret_w, 4 * ret_w + moba_w, 4 * ret_w + 2 * moba_w]

    w_in_bf = w_in[0].astype(BF16)
    w_o_bf = w_o[0].astype(BF16)
    wg_ple_bf = w_ple_gate[0].astype(BF16)
    wp_ple_bf = w_ple_proj[0].astype(BF16)
    wr_pad = jnp.concatenate([w_router[0], jnp.zeros((d, LANES - N_EXPERTS), F32)],
                             axis=1).astype(BF16)
    br_pad = jnp.concatenate([b_router[0], jnp.full((LANES - N_EXPERTS,), -jnp.inf, F32)])[None, :]
    qn, kn = q_norm, k_norm

    t_p = batch * seq_len
    xp = x_prompt.reshape(t_p, d)
    tm = 1024
    tables_p = _rope_tables(jnp.arange(seq_len, dtype=jnp.int32))
    z_p = _inproj(xp, norm_mix, w_in_bf, tables_p, qn, kn, seq_len, tm)
    ret_p, state_p = _ret_prompt(z_p, batch, seq_len)
    moba_p = _moba_prompt(z_p, batch, seq_len)

    xs_rows = _pad_rows(x_sample.reshape(n_dec, d), SAMPLE_ROWS)
    tables_s = _rope_tables(jnp.full((SAMPLE_ROWS,), past_len, jnp.int32))
    z_s = _inproj(xs_rows, norm_mix, w_in_bf, tables_s, qn, kn, SAMPLE_ROWS, SAMPLE_ROWS)
    zs = z_s[:n_dec]
    ret_s, state_s = _ret_sample(zs[:, off[0]:off[1]], zs[:, off[1]:off[2]], zs[:, off[2]:off[3]],
                                 zs[:, off[3]:off[4]], state_ret[0])
    heads = lambda a: a.reshape(n_dec, MOBA_HEADS, MOBA_HEAD_DIM)
    mk_s, mv_s = zs[:, off[5]:off[6]], zs[:, off[6]:]
    pool_shape = cache_k.shape[1:]
    moba_s = _moba_sample(heads(zs[:, off[4]:off[5]]), heads(mk_s), heads(mv_s),
                          cache_k.reshape(pool_shape), cache_v.reshape(pool_shape), page_table)

    h_p, hn_p, idx_p, gate_p = _outproj(xp, ret_p, moba_p, w_o_bf, norm_ffn, wr_pad, br_pad, 256)
    h_s, hn_s, idx_s, gate_s = _outproj(xs_rows, _pad_rows(ret_s, SAMPLE_ROWS),
                                        _pad_rows(moba_s, SAMPLE_ROWS), w_o_bf, norm_ffn,
                                        wr_pad, br_pad, SAMPLE_ROWS)

    n_tok = t_p + n_dec
    top_idx = jnp.concatenate([idx_p[:, :TOP_K], idx_s[:n_dec, :TOP_K]], axis=0)
    n_assign = n_tok * TOP_K
    n_chunks = n_assign // MOE_ROW_BLOCK + N_EXPERTS
    n_slots = n_chunks * MOE_ROW_BLOCK
    n_items = N_EXPERTS + pl.cdiv(n_assign, MOE_ITEM_ROWS)
    dest, pad_lo, pad_hi, items = _route(top_idx, n_items)
    hn3 = jnp.concatenate([hn_p, hn_s[:n_dec]], axis=0).reshape(n_tok, ROW_PIECES, d // ROW_PIECES)
    xs3 = _dispatch(hn3, dest, pad_lo, pad_hi, n_slots)
    out2 = _moe(xs3, items, w_gu[0], b_gu[0], w_down[0], b_down[0])
    dest_pad = jnp.concatenate([dest, jnp.zeros(((SAMPLE_ROWS - n_dec) * TOP_K,), jnp.int32)])
    h2_p = _combine(dest_pad, h_p, gate_p, out2, 0, COMBINE_TOKENS)
    h2_s = _combine(dest_pad, h_s, gate_s, out2, t_p, SAMPLE_ROWS)

    y_p = _ple(h2_p, p_prompt[0].reshape(t_p, -1), norm_ple, wg_ple_bf, wp_ple_bf, 512)
    y_s = _ple(h2_s, _pad_rows(p_sample[0].reshape(n_dec, -1), SAMPLE_ROWS), norm_ple,
               wg_ple_bf, wp_ple_bf, SAMPLE_ROWS)

    kv = lambda a, n, l: a.reshape(1, n, l, MOBA_HEADS, MOBA_HEAD_DIM)
    return (y_p.reshape(batch, seq_len, d), y_s[:n_dec].reshape(n_dec, 1, d),
            kv(z_p[:, off[5]:off[6]], batch, seq_len), kv(z_p[:, off[6]:], batch, seq_len),
            state_p[None], kv(mk_s, n_dec, 1), kv(mv_s, n_dec, 1), state_s[None])
```

```python
import functools
import math

import jax
import jax.numpy as jnp
from jax import lax
from jax.experimental import pallas as pl
from jax.experimental.pallas import tpu as pltpu

F32 = jnp.float32
BF16 = jnp.bfloat16

RET_HEADS = 4
RET_DK = 256
RET_DV = 256
RET_ROPE_THETA = 10000.0
MOBA_HEADS = 8
MOBA_HEAD_DIM = 128
MOBA_BLOCK = 256
MOBA_TOPK = 3
ROPE_THETA = 500000.0
ROPE_DIM = MOBA_HEAD_DIM // 4
N_EXPERTS = 32
TOP_K = 4
SWIGLU_LIMIT = 7.0
SWIGLU_ALPHA = 1.702
EPS = 1e-6

LANES = 128
SUBLANES = 8
VMEM_LIMIT_BYTES = 56 * 1024 * 1024

SAMPLE_ROWS = 16
RET_CHUNK = 256
MOE_ROW_BLOCK = 128
MOE_MATMUL_ROWS = 256
MOE_ITEM_ROWS = 1536
MOE_FF_TILE = 512
MOE_VMEM_LIMIT_BYTES = 60 * 1024 * 1024
MOE_WEIGHT_STREAMS = 4
ROW_PIECES = 16
COMBINE_TOKENS = 128
MOBA_KV_GROUP = 4
MOBA_KV_STREAMS = 4


def _cparams(semantics, vmem=VMEM_LIMIT_BYTES):
    return pltpu.CompilerParams(dimension_semantics=semantics, vmem_limit_bytes=vmem)


def _nt_dot(a, b, **kw):
    return lax.dot_general(a, b, (((1,), (1,)), ((), ())), preferred_element_type=F32, **kw)


def _rms(x):
    return x * lax.rsqrt(jnp.mean(x * x, axis=-1, keepdims=True) + EPS)


def _rnd(a):
    return a.astype(BF16).astype(F32)


_IN_TN = 512
IN_WEIGHT_STREAMS = 4


def _inproj_kernel(x_ref, g_ref, *refs):
    w_refs = refs[:IN_WEIGHT_STREAMS]
    (cr_ref, sr_ref, cm_ref, sa_ref, sb_ref, qn_ref, kn_ref, z_ref, k_ref, v_ref,
     xn_ref) = refs[IN_WEIGHT_STREAMS:]
    j = pl.program_id(1)
    n_ret = RET_HEADS * RET_DK // _IN_TN
    n_moba = MOBA_HEADS * MOBA_HEAD_DIM // _IN_TN
    moba0 = 4 * n_ret

    @pl.when(j == 0)
    def _():
        xn_ref[...] = (_rms(x_ref[...]) * g_ref[...]).astype(BF16)

    kq = xn_ref.shape[1] // IN_WEIGHT_STREAMS
    acc = sum(jnp.dot(xn_ref[:, q * kq:(q + 1) * kq], w_refs[q][...], preferred_element_type=F32)
              for q in range(IN_WEIGHT_STREAMS))

    @pl.when(j < 2 * n_ret)
    def _():
        half = RET_DK // 2
        c = cr_ref[...]
        s = sr_ref[...]
        scale = jnp.where(j < n_ret, 1.0, RET_DK ** -0.5).astype(F32)
        for hh in range(_IN_TN // RET_DK):
            x1 = acc[:, hh * RET_DK:hh * RET_DK + half]
            x2 = acc[:, hh * RET_DK + half:(hh + 1) * RET_DK]
            z_ref[:, hh * RET_DK:hh * RET_DK + half] = (x1 * c - x2 * s) * scale
            z_ref[:, hh * RET_DK + half:(hh + 1) * RET_DK] = (x2 * c + x1 * s) * scale

    @pl.when(((j >= 2 * n_ret) & (j < moba0)) | (j >= moba0 + 2 * n_moba))
    def _():
        z_ref[...] = acc

    @pl.when(j >= moba0 + 2 * n_moba)
    def _():
        v_ref[...] = acc

    @pl.when((j >= moba0) & (j < moba0 + 2 * n_moba))
    def _():
        gain = jnp.where(j < moba0 + n_moba, qn_ref[...], kn_ref[...])
        for hh in range(_IN_TN // MOBA_HEAD_DIM):
            t = _rms(acc[:, hh * MOBA_HEAD_DIM:(hh + 1) * MOBA_HEAD_DIM]) * gain
            up = pltpu.roll(t, MOBA_HEAD_DIM - ROPE_DIM // 2, 1)
            dn = pltpu.roll(t, ROPE_DIM // 2, 1)
            z_ref[:, hh * MOBA_HEAD_DIM:(hh + 1) * MOBA_HEAD_DIM] = (
                t * cm_ref[...] + up * sa_ref[...] + dn * sb_ref[...])

    @pl.when((j >= moba0 + n_moba) & (j < moba0 + 2 * n_moba))
    def _():
        k_ref[...] = z_ref[...]


def _rope_tables(pos):
    posf = pos.astype(F32)[:, None]
    half = RET_DK // 2
    inv = RET_ROPE_THETA ** (-2.0 * jnp.arange(half, dtype=F32) / RET_DK)
    ang = posf * inv[None, :]
    cr, sr = jnp.cos(ang), jnp.sin(ang)
    mh = ROPE_DIM // 2
    inv_m = ROPE_THETA ** (-2.0 * jnp.arange(mh, dtype=F32) / ROPE_DIM)
    ang_m = posf * inv_m[None, :]
    cos_m, sin_m = jnp.cos(ang_m), jnp.sin(ang_m)
    n = pos.shape[0]
    rest = MOBA_HEAD_DIM - ROPE_DIM
    cm = jnp.concatenate([cos_m, cos_m, jnp.ones((n, rest), F32)], axis=1)
    sa = jnp.concatenate([-sin_m, jnp.zeros((n, MOBA_HEAD_DIM - mh), F32)], axis=1)
    sb = jnp.concatenate([jnp.zeros((n, mh), F32), sin_m, jnp.zeros((n, rest), F32)], axis=1)
    return cr, sr, cm, sa, sb


def _inproj(x2d, norm, w_bf, tables, qn, kn, seq_len, tm):
    t, d = x2d.shape
    n_out = w_bf.shape[1]
    tpos = seq_len // tm
    moba_w = MOBA_HEADS * MOBA_HEAD_DIM
    n_moba = moba_w // _IN_TN
    k0 = (n_out - 2 * moba_w) // _IN_TN
    cr, sr, cm, sa, sb = tables
    tab = pl.BlockSpec((tm, LANES), lambda i, j: (i % tpos, 0))
    vec = pl.BlockSpec((1, LANES), lambda i, j: (0, 0))
    return pl.pallas_call(
        _inproj_kernel,
        grid=(t // tm, n_out // _IN_TN),
        in_specs=[
            pl.BlockSpec((tm, d), lambda i, j: (i, 0)),
            pl.BlockSpec((1, d), lambda i, j: (0, 0)),
            *[pl.BlockSpec((d // IN_WEIGHT_STREAMS, _IN_TN), (lambda i, j, q=q: (q, j)))
              for q in range(IN_WEIGHT_STREAMS)],
            tab, tab, tab, tab, tab, vec, vec,
        ],
        out_specs=[pl.BlockSpec((tm, _IN_TN), lambda i, j: (i, j)),
                   pl.BlockSpec((tm, _IN_TN), lambda i, j: (i, jnp.clip(j - k0, 0, n_moba - 1))),
                   pl.BlockSpec((tm, _IN_TN), lambda i, j: (i, jnp.clip(j - k0 - n_moba, 0, n_moba - 1)))],
        out_shape=[jax.ShapeDtypeStruct((t, n_out), F32),
                   jax.ShapeDtypeStruct((t, moba_w), F32), jax.ShapeDtypeStruct((t, moba_w), F32)],
        scratch_shapes=[pltpu.VMEM((tm, d), BF16)],
        compiler_params=_cparams(("arbitrary", "arbitrary")),
        name="inproj",
    )(x2d, norm, *([w_bf] * IN_WEIGHT_STREAMS), cr, sr, cm, sa, sb, qn, kn)


def _ret_gate(o, g):
    return _rms(o) * (g * jax.nn.sigmoid(g))


def _ret_prompt_kernel(logg_ref, q_ref, k_ref, v_ref, g_ref, o_ref, st_ref, r_ref):
    h = pl.program_id(1)
    c = pl.program_id(2)
    n = q_ref.shape[0]

    @pl.when(c == 0)
    def _():
        r_ref[...] = jnp.zeros_like(r_ref)

    lg = logg_ref[h]
    i = lax.broadcasted_iota(jnp.int32, (n, 1), 0).astype(F32)
    jj = lax.broadcasted_iota(jnp.int32, (1, n), 1).astype(F32)
    diff = i - jj
    inner = jnp.where(diff >= 0, jnp.exp(lg * jnp.maximum(diff, 0.0)), 0.0)
    q_dec = jnp.exp(lg * (i + 1.0))
    k_dec = jnp.exp(lg * (n - 1.0 - i))
    chunk_dec = jnp.exp(lg * jnp.full((1, 1), float(n), F32))

    q = q_ref[...]
    k = k_ref[...]
    vb = v_ref[...].astype(BF16)
    r = r_ref[...]
    s = _nt_dot(q.astype(BF16), k.astype(BF16)) * inner
    o = (jnp.dot(s.astype(BF16), vb, preferred_element_type=F32)
         + jnp.dot((q * q_dec).astype(BF16), r.astype(BF16), preferred_element_type=F32))
    kd_t = (k * k_dec).T.astype(BF16)
    r_new = r * chunk_dec + jnp.dot(kd_t, vb, preferred_element_type=F32)
    r_ref[...] = r_new
    o_ref[...] = _ret_gate(o, g_ref[...])

    @pl.when(c == pl.num_programs(2) - 1)
    def _():
        st_ref[0, 0] = r_new


def _ret_log_decay():
    return jnp.log1p(-jnp.exp2(-5.0 - jnp.arange(RET_HEADS, dtype=F32)))


def _ret_prompt(z, batch, seq_len):
    nc = seq_len // RET_CHUNK
    w = RET_DK

    def col(off):
        return pl.BlockSpec((RET_CHUNK, w), lambda b, h, c: (b * nc + c, off + h))

    return pl.pallas_call(
        _ret_prompt_kernel,
        grid=(batch, RET_HEADS, nc),
        in_specs=[pl.BlockSpec(memory_space=pltpu.SMEM),
                  col(0), col(RET_HEADS), col(2 * RET_HEADS), col(3 * RET_HEADS)],
        out_specs=[
            pl.BlockSpec((RET_CHUNK, RET_DV), lambda b, h, c: (b * nc + c, h)),
            pl.BlockSpec((1, 1, RET_DK, RET_DV), lambda b, h, c: (b, h, 0, 0)),
        ],
        scratch_shapes=[pltpu.VMEM((RET_DK, RET_DV), F32)],
        out_shape=[
            jax.ShapeDtypeStruct((batch * seq_len, RET_HEADS * RET_DV), F32),
            jax.ShapeDtypeStruct((batch, RET_HEADS, RET_DK, RET_DV), F32),
        ],
        compiler_params=_cparams(("parallel", "parallel", "arbitrary")),
        name="ret_prompt",
    )(_ret_log_decay(), z, z, z, z)


def _ret_sample_kernel(logg_ref, q_ref, kc_ref, kr_ref, v_ref, g_ref, s0_ref, o_ref, st_ref):
    h = pl.program_id(1)
    dec = jnp.exp(logg_ref[h] * jnp.ones((1, 1), F32))
    q = q_ref[0, 0]
    v = v_ref[0, 0]
    r0 = s0_ref[0, 0]
    qk = jnp.sum(q * kr_ref[0, 0], axis=-1, keepdims=True)
    q8 = jnp.broadcast_to(q * dec, (SUBLANES, RET_DK)).astype(BF16)
    qr = jnp.dot(q8, r0.astype(BF16), preferred_element_type=F32)[0:1]
    o = qk * v + qr
    st_ref[0, 0] = r0 * dec + kc_ref[0, 0] * v
    o_ref[0, 0] = _ret_gate(o, g_ref[0, 0])


def _ret_sample(q, k, v, g, state):
    n = q.shape[0]
    row = lambda a: a.reshape(n, RET_HEADS, 1, RET_DK)
    rspec = pl.BlockSpec((1, 1, 1, RET_DK), lambda b, h: (b, h, 0, 0))
    mspec = pl.BlockSpec((1, 1, RET_DK, RET_DV), lambda b, h: (b, h, 0, 0))
    o, st = pl.pallas_call(
        _ret_sample_kernel,
        grid=(n, RET_HEADS),
        in_specs=[pl.BlockSpec(memory_space=pltpu.SMEM),
                  rspec, pl.BlockSpec((1, 1, RET_DK, 1), lambda b, h: (b, h, 0, 0)),
                  rspec, rspec, rspec, mspec],
        out_specs=[rspec, mspec],
        out_shape=[jax.ShapeDtypeStruct((n, RET_HEADS, 1, RET_DV), F32),
                   jax.ShapeDtypeStruct(state.shape, F32)],
        compiler_params=_cparams(("parallel", "parallel")),
        name="ret_sample",
    )(_ret_log_decay(), row(q), k.reshape(n, RET_HEADS, RET_DK, 1), row(k), row(v), row(g), state)
    return o.reshape(n, RET_HEADS * RET_DV), st


def _moba_prompt_kernel(q_ref, *refs):
    ns = MOBA_KV_STREAMS
    k_refs, v_refs = refs[:ns], refs[ns:2 * ns]
    o_ref, kb_ref, vt_ref, km_ref, sel_ref = refs[2 * ns:]
    qi = pl.program_id(2)
    blk = MOBA_BLOCK
    nb = kb_ref.shape[0] // blk
    per = nb // ns
    scale = MOBA_HEAD_DIM ** -0.5 * math.log2(math.e)

    @pl.when(qi == 0)
    def _():
        for n in range(nb):
            rows = slice(n * blk, (n + 1) * blk)
            local = slice((n % per) * blk, (n % per + 1) * blk)
            kf = k_refs[n // per][local, :]
            kb_ref[rows, :] = kf.astype(BF16)
            vt_ref[:, rows] = v_refs[n // per][local, :].T.astype(BF16)
            km_ref[n:n + 1, :] = jnp.mean(kf, axis=0, keepdims=True)

    qb = q_ref[...].astype(BF16)
    gate = _nt_dot(km_ref[...].astype(BF16), qb)
    kblock = lax.broadcasted_iota(jnp.int32, (nb, blk), 0)
    past = kblock < qi
    g = jnp.where(past, gate, -jnp.inf)
    cnt = jnp.zeros((nb, blk), F32)
    for m in range(nb):
        gm = g[m:m + 1, :]
        cnt = cnt + jnp.where(gm > g, 1.0, jnp.where((gm == g) & (kblock > m), 1.0, 0.0))
    sel_ref[...] = jnp.where(past & (cnt < MOBA_TOPK), 0.0, -jnp.inf)

    own = pl.ds(pl.multiple_of(qi * blk, blk), blk)
    s = _nt_dot(kb_ref[own, :], qb) * scale
    key = lax.broadcasted_iota(jnp.int32, (blk, blk), 0)
    qry = lax.broadcasted_iota(jnp.int32, (blk, blk), 1)
    s = jnp.where(key <= qry, s, -jnp.inf)
    m0 = jnp.max(s, axis=0, keepdims=True)
    p = jnp.exp2(s - m0)
    l0 = jnp.sum(p, axis=0, keepdims=True)
    a0 = jnp.dot(vt_ref[:, own], p.astype(BF16), preferred_element_type=F32)

    grp = MOBA_KV_GROUP
    span = grp * blk
    last = nb // grp - 1

    def scores(j):
        rows = pl.ds(pl.multiple_of(j * span, span), span)
        return _nt_dot(kb_ref[rows, :], qb)

    def body(j, carry):
        m_i, l_i, a_i, s_cur = carry
        s_next = scores(jnp.minimum(j + 1, last))
        rows = pl.ds(pl.multiple_of(j * span, span), span)
        allowed = jnp.concatenate(
            [jnp.broadcast_to(sel_ref[pl.ds(j * grp + t, 1), :], (blk, blk)) for t in range(grp)],
            axis=0)
        sn = s_cur * scale + allowed
        m_new = jnp.maximum(m_i, jnp.max(sn, axis=0, keepdims=True))
        alpha = jnp.exp2(m_i - m_new)
        pn = jnp.exp2(sn - m_new)
        l_new = alpha * l_i + jnp.sum(pn, axis=0, keepdims=True)
        a_new = alpha * a_i + jnp.dot(vt_ref[:, rows], pn.astype(BF16), preferred_element_type=F32)
        return m_new, l_new, a_new, s_next

    _, l_f, a_f, _ = lax.fori_loop(0, (qi + grp - 1) // grp, body, (m0, l0, a0, scores(0)))
    o_ref[...] = (a_f / l_f).T


def _moba_prompt(z, batch, seq_len):
    nq = seq_len // MOBA_BLOCK
    hd = MOBA_HEAD_DIM
    qoff = (2 * RET_HEADS * RET_DK + 2 * RET_HEADS * RET_DV) // hd
    koff = qoff + MOBA_HEADS
    voff = koff + MOBA_HEADS
    ns = MOBA_KV_STREAMS
    assert nq % ns == 0 and nq % MOBA_KV_GROUP == 0
    slabs = lambda off: [pl.BlockSpec((seq_len // ns, hd), (lambda b, h, i, s=s: (b * ns + s, off + h)))
                         for s in range(ns)]
    return pl.pallas_call(
        _moba_prompt_kernel,
        grid=(batch, MOBA_HEADS, nq),
        in_specs=[pl.BlockSpec((MOBA_BLOCK, hd), lambda b, h, i: (b * nq + i, qoff + h))]
        + slabs(koff) + slabs(voff),
        out_specs=pl.BlockSpec((MOBA_BLOCK, hd), lambda b, h, i: (b * nq + i, h)),
        out_shape=jax.ShapeDtypeStruct((batch * seq_len, MOBA_HEADS * hd), F32),
        scratch_shapes=[pltpu.VMEM((seq_len, hd), BF16), pltpu.VMEM((hd, seq_len), BF16),
                        pltpu.VMEM((nq, hd), F32), pltpu.VMEM((nq, MOBA_BLOCK), F32)],
        compiler_params=_cparams(("parallel", "parallel", "arbitrary")),
        name="moba_prompt",
    )(z, *([z] * (2 * ns)))


MOBA_SCAN_PAGES = 8


def _moba_scan_kernel(pt_ref, q_ref, *refs):
    k_refs = refs[:MOBA_SCAN_PAGES]
    sel_ref, gate_ref = refs[MOBA_SCAN_PAGES:]
    p = pl.program_id(1)
    ppb = MOBA_BLOCK // k_refs[0].shape[1]
    blocks_per_step = MOBA_SCAN_PAGES // ppb

    @pl.when(p == 0)
    def _():
        gate_ref[...] = jnp.zeros_like(gate_ref)

    q = _rnd(q_ref[0])
    lane = lax.broadcasted_iota(jnp.int32, gate_ref.shape, 1)
    gate = gate_ref[...]
    for jb in range(blocks_per_step):
        k_sum = sum(jnp.sum(k_refs[jb * ppb + j][0], axis=0) for j in range(ppb))
        g = jnp.sum(q * _rnd(k_sum * (1.0 / MOBA_BLOCK)), axis=1, keepdims=True)
        gate = jnp.where(lane == p * blocks_per_step + jb, g, gate)
    gate_ref[...] = gate

    @pl.when(p == pl.num_programs(1) - 1)
    def _():
        g = jnp.where(lane < pl.num_programs(1) * blocks_per_step, gate, -jnp.inf)
        out = jnp.zeros(gate_ref.shape, jnp.int32)
        for t in range(MOBA_TOPK):
            mx = jnp.max(g, axis=1, keepdims=True)
            idx = jnp.min(jnp.where(g == mx, lane, LANES), axis=1, keepdims=True)
            out = jnp.where(lane == t, idx, out)
            g = jnp.where(lane == idx, -jnp.inf, g)
        sel_ref[0] = out


def _moba_attend_kernel(pt_ref, selp_ref, q_ref, kn_ref, vn_ref, *refs):
    nsrc = (len(refs) - 1) // 2
    k_refs = refs[:nsrc]
    v_refs = refs[nsrc:2 * nsrc]
    o_ref = refs[2 * nsrc]
    h = pl.program_id(1)
    scale = MOBA_HEAD_DIM ** -0.5
    mine2 = lax.broadcasted_iota(jnp.int32, (MOBA_HEADS, MOBA_HEAD_DIM), 0) == h
    mine3 = lax.broadcasted_iota(jnp.int32, (1, MOBA_HEADS, 1), 1) == h
    all2 = lambda a, op: op(op(a, axis=1, keepdims=True), axis=0, keepdims=True)
    all3 = lambda a, op: op(op(a, axis=0, keepdims=True), axis=1, keepdims=True)
    q = _rnd(q_ref[0])
    s_new = all2(jnp.where(mine2, q * _rnd(kn_ref[0]), 0.0), jnp.sum) * scale
    ss = [jnp.where(mine3, jnp.sum(_rnd(kr[0]) * q[None], axis=-1, keepdims=True) * scale, -jnp.inf)
          for kr in k_refs]
    mx = s_new.reshape(1, 1, 1)
    for sj in ss:
        mx = jnp.maximum(mx, all3(sj, jnp.max))
    p_new = jnp.exp(s_new.reshape(1, 1, 1) - mx)
    ps = [jnp.exp(sj - mx) for sj in ss]
    den = p_new
    for pj in ps:
        den = den + all3(pj, jnp.sum)
    inv = 1.0 / den
    acc = jnp.where(mine2, _rnd(p_new * inv).reshape(1, 1) * _rnd(vn_ref[0]), 0.0)
    for pj, vr in zip(ps, v_refs):
        acc = acc + jnp.sum(_rnd(pj * inv) * _rnd(vr[0]), axis=0)
    o_ref[0, 0] = jnp.sum(acc, axis=0, keepdims=True)


def _moba_sample(q, k_new, v_new, cache_k, cache_v, page_table):
    n, npages = page_table.shape
    page = cache_k.shape[1]
    hd = MOBA_HEAD_DIM
    ppb = MOBA_BLOCK // page
    pp = MOBA_SCAN_PAGES
    assert npages % pp == 0 and npages // ppb <= LANES
    page_block = (1, page, MOBA_HEADS, hd)
    tok2 = pl.BlockSpec((1, MOBA_HEADS, hd), lambda b, p, pt: (b, 0, 0))
    scan_specs = [pl.BlockSpec(page_block, (lambda b, p, pt, j=j: (pt[b, p * pp + j], 0, 0, 0)))
                  for j in range(pp)]
    sel = pl.pallas_call(
        _moba_scan_kernel,
        grid_spec=pltpu.PrefetchScalarGridSpec(
            num_scalar_prefetch=1,
            grid=(n, npages // pp),
            in_specs=[tok2] + scan_specs,
            out_specs=pl.BlockSpec((1, MOBA_HEADS, LANES), lambda b, p, pt: (b, 0, 0)),
            scratch_shapes=[pltpu.VMEM((MOBA_HEADS, LANES), F32)],
        ),
        out_shape=jax.ShapeDtypeStruct((n, MOBA_HEADS, LANES), jnp.int32),
        compiler_params=_cparams(("parallel", "arbitrary")),
        name="moba_scan",
    )(page_table, q, *([cache_k] * pp))
    selp = (sel[:, :, :MOBA_TOPK, None] * ppb + jnp.arange(ppb, dtype=jnp.int32)).reshape(-1)
    nsrc = MOBA_TOPK * ppb

    def page_spec(j):
        def index(b, h, pt, sp):
            return (pt[b, sp[(b * MOBA_HEADS + h) * nsrc + j]], 0, 0, 0)
        return pl.BlockSpec(page_block, index)

    tok = pl.BlockSpec((1, MOBA_HEADS, hd), lambda b, h, pt, sp: (b, 0, 0))
    pages = [page_spec(j) for j in range(nsrc)]
    out = pl.pallas_call(
        _moba_attend_kernel,
        grid_spec=pltpu.PrefetchScalarGridSpec(
            num_scalar_prefetch=2,
            grid=(n, MOBA_HEADS),
            in_specs=[tok, tok, tok] + pages + pages,
            out_specs=pl.BlockSpec((1, 1, 1, hd), lambda b, h, pt, sp: (b, h, 0, 0)),
        ),
        out_shape=jax.ShapeDtypeStruct((n, MOBA_HEADS, 1, hd), F32),
        compiler_params=_cparams(("parallel", "parallel")),
        name="moba_attend",
    )(page_table, selp, q, k_new, v_new, *([cache_k] * nsrc), *([cache_v] * nsrc))
    return out.reshape(n, MOBA_HEADS * hd)


def _outproj_kernel(x_ref, ret_ref, moba_ref, wo_ref, nf_ref, wr_ref, br_ref,
                    h_ref, hn_ref, idx_ref, gate_ref):
    rw = ret_ref.shape[1]
    h = (x_ref[...]
         + jnp.dot(ret_ref[...].astype(BF16), wo_ref[:rw, :], preferred_element_type=F32)
         + jnp.dot(moba_ref[...].astype(BF16), wo_ref[rw:, :], preferred_element_type=F32))
    h_ref[...] = h
    hn = _rms(h) * nf_ref[...]
    hn_ref[...] = hn
    logits = jnp.dot(hn.astype(BF16), wr_ref[...], preferred_element_type=F32) + br_ref[...]
    lane = lax.broadcasted_iota(jnp.int32, logits.shape, 1)
    vals, idxs = [], []
    for _ in range(TOP_K):
        mx = jnp.max(logits, axis=1, keepdims=True)
        ix = jnp.min(jnp.where(logits == mx, lane, LANES), axis=1, keepdims=True)
        vals.append(mx)
        idxs.append(ix)
        logits = jnp.where(lane == ix, -jnp.inf, logits)
    es = [jnp.exp(v - vals[0]) for v in vals]
    den = es[0] + es[1] + es[2] + es[3]
    idx_out = jnp.zeros(lane.shape, jnp.int32)
    gate_out = jnp.zeros(lane.shape, F32)
    for t in range(TOP_K):
        idx_out = jnp.where(lane == t, idxs[t], idx_out)
        gate_out = jnp.where(lane == t, es[t] / den, gate_out)
    idx_ref[...] = idx_out
    gate_ref[...] = gate_out


def _outproj(x2d, ret, moba, wo_bf, norm_ffn, wr_pad, br_pad, tm):
    t, d = x2d.shape
    rw, mw = ret.shape[1], moba.shape[1]
    const = lambda shape: pl.BlockSpec(shape, lambda i: (0, 0))
    rows = lambda w: pl.BlockSpec((tm, w), lambda i: (i, 0))
    return pl.pallas_call(
        _outproj_kernel,
        grid=(t // tm,),
        in_specs=[rows(d), rows(rw), rows(mw), const((rw + mw, d)), const((1, d)),
                  const((d, LANES)), const((1, LANES))],
        out_specs=[rows(d), rows(d), rows(LANES), rows(LANES)],
        out_shape=[jax.ShapeDtypeStruct((t, d), F32), jax.ShapeDtypeStruct((t, d), F32),
                   jax.ShapeDtypeStruct((t, LANES), jnp.int32),
                   jax.ShapeDtypeStruct((t, LANES), F32)],
        compiler_params=_cparams(("parallel",)),
        name="outproj",
    )(x2d, ret, moba, wo_bf, norm_ffn, wr_pad, br_pad)


def _dispatch_kernel(dest_ref, pad_lo_ref, pad_hi_ref, hn_ref, xs_hbm, zero_ref, sem, zsem, *,
                     ntok):
    i = pl.program_id(0)
    tb = hn_ref.shape[0]
    first = i * tb
    count = jnp.minimum(tb, ntok - first)

    def row_copy(r, k):
        return pltpu.make_async_copy(hn_ref.at[r], xs_hbm.at[dest_ref[(first + r) * TOP_K + k]], sem)

    def issue(r, c):
        for k in range(TOP_K):
            row_copy(r, k).start()
        return c

    lax.fori_loop(0, count, issue, 0)

    @pl.when(i == 0)
    def _():
        zero_ref[...] = jnp.zeros_like(zero_ref)

        def pad_copy(s):
            return pltpu.make_async_copy(zero_ref.at[0], xs_hbm.at[s], zsem)

        def per_expert(e, c):
            def pad_row(s, c2):
                pad_copy(s).start()
                return c2
            lax.fori_loop(pad_lo_ref[e], pad_hi_ref[e], pad_row, 0)
            return c

        lax.fori_loop(0, N_EXPERTS, per_expert, 0)

        rb = zero_ref.shape[0]
        tail0 = pad_hi_ref[N_EXPERTS - 1]
        n_tail = (xs_hbm.shape[0] - tail0) // rb

        def tail_copy(c):
            rows = pl.ds(pl.multiple_of(tail0 + c * rb, rb), rb)
            return pltpu.make_async_copy(zero_ref, xs_hbm.at[rows], zsem)

        def tail_start(c, c2):
            tail_copy(c).start()
            return c2

        lax.fori_loop(0, n_tail, tail_start, 0)

        def per_expert_wait(e, c):
            def pad_wait(s, c2):
                pad_copy(s).wait()
                return c2
            lax.fori_loop(pad_lo_ref[e], pad_hi_ref[e], pad_wait, 0)
            return c

        lax.fori_loop(0, N_EXPERTS, per_expert_wait, 0)

        def tail_wait(c, c2):
            tail_copy(c).wait()
            return c2

        lax.fori_loop(0, n_tail, tail_wait, 0)

    @pl.when(count == tb)
    def _():
        for k in range(TOP_K):
            pltpu.make_async_copy(hn_ref, xs_hbm.at[pl.ds(0, tb)], sem).wait()

    @pl.when(count < tb)
    def _():
        def drain(r, c):
            for k in range(TOP_K):
                row_copy(r, k).wait()
            return c

        lax.fori_loop(0, count, drain, 0)


def _dispatch(hn3, dest_flat, pad_lo, pad_hi, n_slots):
    ntok = hn3.shape[0]
    tb = COMBINE_TOKENS
    return pl.pallas_call(
        functools.partial(_dispatch_kernel, ntok=ntok),
        grid_spec=pltpu.PrefetchScalarGridSpec(
            num_scalar_prefetch=3,
            grid=(pl.cdiv(ntok, tb),),
            in_specs=[pl.BlockSpec((tb,) + hn3.shape[1:], lambda i, d, lo, hi: (i, 0, 0))],
            out_specs=pl.BlockSpec(memory_space=pl.ANY),
            scratch_shapes=[pltpu.VMEM((MOE_ROW_BLOCK,) + hn3.shape[1:], F32),
                            pltpu.SemaphoreType.DMA(()), pltpu.SemaphoreType.DMA(())],
        ),
        out_shape=jax.ShapeDtypeStruct((n_slots,) + hn3.shape[1:], F32),
        compiler_params=_cparams(("arbitrary",)),
        name="moe_dispatch",
    )(dest_flat, pad_lo, pad_hi, hn3)


def _moe_kernel(item_e, item_row0, item_nch, tail_ref, xs_hbm, *refs):
    ns = MOE_WEIGHT_STREAMS
    wg_refs, wu_refs, wd_refs = refs[:ns], refs[ns:2 * ns], refs[2 * ns:3 * ns]
    (bg_ref, bu_ref, bd_ref, out_hbm, x_ref, acc_ref, stage_ref, wgb_ref, wub_ref, wdb_ref,
     sem_in, sem_out) = refs[3 * ns:]
    it = pl.program_id(0)
    f = pl.program_id(1)
    nf = pl.num_programs(1)
    rb = MOE_ROW_BLOCK
    big = MOE_MATMUL_ROWS
    pieces = ROW_PIECES
    nch = item_nch[it]
    row0 = item_row0[it]
    n_big = nch // (big // rb)
    has_tail = nch % (big // rb) != 0
    tail_row = pl.multiple_of(n_big * big, rb)

    def hbm_rows(start, size):
        return pl.ds(pl.multiple_of((row0 + start) * pieces, rb * pieces), size * pieces)

    def in_copy(start, size, slot):
        return pltpu.make_async_copy(xs_hbm.at[hbm_rows(start, size)],
                                     stage_ref.at[slot, pl.ds(0, size * pieces)], sem_in.at[slot])

    def out_copy(start, size, slot):
        return pltpu.make_async_copy(stage_ref.at[slot, pl.ds(0, size * pieces)],
                                     out_hbm.at[hbm_rows(start, size)], sem_out.at[slot])

    def stage_to_x(start, size, slot):
        rows = pl.ds(start, size)
        for p in range(pieces):
            piece = stage_ref[slot, pl.ds(p, size, stride=pieces), :]
            x_ref[rows, p * LANES:(p + 1) * LANES] = piece.astype(BF16)
        acc_ref[rows, :] = jnp.broadcast_to(bd_ref[0], (size, acc_ref.shape[1]))

    def acc_to_stage(start, size, slot):
        rows = pl.ds(start, size)
        for p in range(pieces):
            stage_ref[slot, pl.ds(p, size, stride=pieces), :] = acc_ref[rows, p * LANES:(p + 1) * LANES]

    @pl.when(nch > 0)
    def _():
        for src, dst in ((wg_refs, wgb_ref), (wu_refs, wub_ref), (wd_refs, wdb_ref)):
            rows = dst.shape[0] // ns
            for q in range(ns):
                dst[q * rows:(q + 1) * rows, :] = src[q][0].astype(BF16)
        bg = bg_ref[0]
        bu = bu_ref[0]

        def compute(start, size):
            rows = pl.ds(start, size)
            x = x_ref[rows, :]
            g = jnp.dot(x, wgb_ref[...], preferred_element_type=F32) + bg
            u = jnp.dot(x, wub_ref[...], preferred_element_type=F32) + bu
            g = jnp.minimum(g, SWIGLU_LIMIT)
            u = jnp.clip(u, -SWIGLU_LIMIT, SWIGLU_LIMIT)
            act = (u + 1.0) * g * jax.nn.sigmoid(SWIGLU_ALPHA * g)
            acc_ref[rows, :] += jnp.dot(act.astype(BF16), wdb_ref[...], preferred_element_type=F32)

        @pl.when(f == 0)
        def _():
            @pl.when(n_big > 0)
            def _():
                in_copy(0, big, 0).start()

            def step(r, c):
                slot = r % 2
                start = pl.multiple_of(r * big, big)

                @pl.when(r + 1 < n_big)
                def _():
                    in_copy(start + big, big, 1 - slot).start()

                in_copy(start, big, slot).wait()
                stage_to_x(start, big, slot)
                compute(start, big)
                return c

            lax.fori_loop(0, n_big, step, 0)

            @pl.when(has_tail)
            def _():
                cp = in_copy(tail_row, rb, 0)
                cp.start()
                cp.wait()
                stage_to_x(tail_row, rb, 0)
                compute(tail_row, rb)

        @pl.when((f > 0) & (f < nf - 1))
        def _():
            def step(r, c):
                compute(pl.multiple_of(r * big, big), big)
                return c

            lax.fori_loop(0, n_big, step, 0)

            @pl.when(has_tail)
            def _():
                compute(tail_row, rb)

        @pl.when(f == nf - 1)
        def _():
            def step(r, c):
                slot = r % 2
                start = pl.multiple_of(r * big, big)
                compute(start, big)

                @pl.when(r >= 2)
                def _():
                    out_copy(start - 2 * big, big, slot).wait()

                acc_to_stage(start, big, slot)
                out_copy(start, big, slot).start()
                return c

            lax.fori_loop(0, n_big, step, 0)

            @pl.when(n_big >= 2)
            def _():
                out_copy(0, big, n_big % 2).wait()

            @pl.when(n_big >= 1)
            def _():
                out_copy(0, big, (n_big - 1) % 2).wait()

            @pl.when(has_tail)
            def _():
                compute(tail_row, rb)
                acc_to_stage(tail_row, rb, 0)
                cp = out_copy(tail_row, rb, 0)
                cp.start()
                cp.wait()

    @pl.when((it == pl.num_programs(0) - 1) & (f == nf - 1))
    def _():
        tail0 = tail_ref[0]
        n_tail = (out_hbm.shape[0] // pieces - tail0) // rb
        stage_ref[0] = jnp.zeros(stage_ref.shape[1:], F32)

        def tail_copy(c):
            rows = pl.ds(pl.multiple_of((tail0 + c * rb) * pieces, rb * pieces), rb * pieces)
            return pltpu.make_async_copy(stage_ref.at[0, pl.ds(0, rb * pieces)], out_hbm.at[rows],
                                         sem_out.at[0])

        def tail_start(c, c2):
            tail_copy(c).start()
            return c2

        def tail_wait(c, c2):
            tail_copy(c).wait()
            return c2

        lax.fori_loop(0, n_tail, tail_start, 0)
        lax.fori_loop(0, n_tail, tail_wait, 0)


def _moe(xs3, items, w_gu, b_gu, w_down, b_down):
    item_e, item_row0, item_nch, tail0 = items
    n_items = item_e.shape[0]
    n_slots = xs3.shape[0]
    d = w_gu.shape[1]
    ff = w_down.shape[1]
    tf = MOE_FF_TILE
    nf = ff // tf
    assert MOE_MATMUL_ROWS == 2 * MOE_ROW_BLOCK and MOE_ITEM_ROWS % MOE_MATMUL_ROWS == 0 and nf >= 2
    xs2 = xs3.reshape(n_slots * ROW_PIECES, LANES)

    def ftile(it, f, nch):
        return jnp.where(nch[it] > 0, f, nf - 1)

    ns = MOE_WEIGHT_STREAMS

    def slab(rows, cols, index):
        return [pl.BlockSpec((1, rows // ns, cols),
                             (lambda it, f, e, r0, nch, t0, q=q: index(e[it], ftile(it, f, nch), q)))
                for q in range(ns)]

    wg_specs = slab(d, tf, lambda e, ft, q: (e, q, ft))
    wu_specs = slab(d, tf, lambda e, ft, q: (e, q, nf + ft))
    wd_specs = slab(tf, d, lambda e, ft, q: (e, ft * ns + q, 0))
    bg_spec = pl.BlockSpec((1, 1, tf), lambda it, f, e, r0, nch, t0: (e[it], 0, ftile(it, f, nch)))
    bu_spec = pl.BlockSpec((1, 1, tf), lambda it, f, e, r0, nch, t0: (e[it], 0, nf + ftile(it, f, nch)))
    bd_spec = pl.BlockSpec((1, 1, d), lambda it, f, e, r0, nch, t0: (e[it], 0, 0))
    b_gu3 = b_gu.reshape(N_EXPERTS, 1, 2 * ff)
    b_down3 = b_down.reshape(N_EXPERTS, 1, d)
    out2 = pl.pallas_call(
        _moe_kernel,
        grid_spec=pltpu.PrefetchScalarGridSpec(
            num_scalar_prefetch=4,
            grid=(n_items, nf),
            in_specs=([pl.BlockSpec(memory_space=pl.ANY)] + wg_specs + wu_specs + wd_specs
                      + [bg_spec, bu_spec, bd_spec]),
            out_specs=pl.BlockSpec(memory_space=pl.ANY),
            scratch_shapes=[
                pltpu.VMEM((MOE_ITEM_ROWS, d), BF16),
                pltpu.VMEM((MOE_ITEM_ROWS, d), F32),
                pltpu.VMEM((2, MOE_MATMUL_ROWS * ROW_PIECES, LANES), F32),
                pltpu.VMEM((d, tf), BF16), pltpu.VMEM((d, tf), BF16), pltpu.VMEM((tf, d), BF16),
                pltpu.SemaphoreType.DMA((2,)), pltpu.SemaphoreType.DMA((2,)),
            ],
        ),
        out_shape=jax.ShapeDtypeStruct(xs2.shape, F32),
        compiler_params=_cparams(("arbitrary", "arbitrary"), MOE_VMEM_LIMIT_BYTES),
        name="moe_experts",
    )(item_e, item_row0, item_nch, tail0, xs2, *([w_gu] * (2 * ns)), *([w_down] * ns),
      b_gu3, b_gu3, b_down3)
    return out2


def _combine_kernel(dest_ref, h_ref, gate_ref, out_hbm, o_ref, buf_ref, sem, *, tok_off):
    i = pl.program_id(0)
    tb = h_ref.shape[0]
    pieces = ROW_PIECES

    def row_copy(step, r, k, slot):
        d = dest_ref[(tok_off + step * tb + r) * TOP_K + k]
        src = out_hbm.at[pl.ds(pl.multiple_of(d * pieces, pieces), pieces)]
        dst = buf_ref.at[slot, pl.ds(pl.multiple_of((k * tb + r) * pieces, pieces), pieces)]
        return pltpu.make_async_copy(src, dst, sem.at[slot])

    def issue(step, slot):
        def body(r, c):
            for k in range(TOP_K):
                row_copy(step, r, k, slot).start()
            return c
        lax.fori_loop(0, tb, body, 0)

    @pl.when(i == 0)
    def _():
        issue(0, 0)

    @pl.when(i + 1 < pl.num_programs(0))
    def _():
        issue(i + 1, (i + 1) % 2)

    slot = i % 2
    pltpu.make_async_copy(out_hbm.at[pl.ds(0, TOP_K * tb * pieces)], buf_ref.at[slot],
                          sem.at[slot]).wait()

    for p in range(pieces):
        f = jnp.zeros((tb, LANES), F32)
        for k in range(TOP_K):
            rows = buf_ref[slot, pl.ds(k * tb * pieces + p, tb, stride=pieces), :]
            f = f + gate_ref[:, k:k + 1] * rows
        o_ref[:, p * LANES:(p + 1) * LANES] = h_ref[:, p * LANES:(p + 1) * LANES] + f


def _combine(dest_flat, h, gates, out2, tok_off, tb):
    t, d = h.shape
    return pl.pallas_call(
        functools.partial(_combine_kernel, tok_off=tok_off),
        grid_spec=pltpu.PrefetchScalarGridSpec(
            num_scalar_prefetch=1,
            grid=(t // tb,),
            in_specs=[pl.BlockSpec((tb, d), lambda i, ds: (i, 0)),
                      pl.BlockSpec((tb, LANES), lambda i, ds: (i, 0)),
                      pl.BlockSpec(memory_space=pl.ANY)],
            out_specs=pl.BlockSpec((tb, d), lambda i, ds: (i, 0)),
            scratch_shapes=[pltpu.VMEM((2, TOP_K * tb * ROW_PIECES, LANES), F32),
                            pltpu.SemaphoreType.DMA((2,))],
        ),
        out_shape=jax.ShapeDtypeStruct((t, d), F32),
        compiler_params=_cparams(("arbitrary",)),
        name="moe_combine",
    )(dest_flat, h, gates, out2)


def _ple_kernel(h_ref, p_ref, n_ref, wg_ref, wp_ref, y_ref):
    h = h_ref[...]
    hn = (_rms(h) * n_ref[...]).astype(BF16)
    gate = jax.nn.sigmoid(jnp.dot(hn, wg_ref[...], preferred_element_type=F32))
    proj = jnp.dot(p_ref[...].astype(BF16), wp_ref[...], preferred_element_type=F32)
    y_ref[...] = h + gate * proj


def _ple(h, p, norm, wg_bf, wp_bf, tm):
    t, d = h.shape
    pd = p.shape[1]
    const = lambda shape: pl.BlockSpec(shape, lambda i: (0, 0))
    return pl.pallas_call(
        _ple_kernel,
        grid=(t // tm,),
        in_specs=[pl.BlockSpec((tm, d), lambda i: (i, 0)), pl.BlockSpec((tm, pd), lambda i: (i, 0)),
                  const((1, d)), const((d, d)), const((pd, d))],
        out_specs=pl.BlockSpec((tm, d), lambda i: (i, 0)),
        out_shape=jax.ShapeDtypeStruct((t, d), F32),
        compiler_params=_cparams(("parallel",)),
        name="ple",
    )(h, p, norm, wg_bf, wp_bf)


def _route(top_idx, n_items):
    rb = MOE_ROW_BLOCK
    flat = top_idx.reshape(-1)
    onehot = (flat[:, None] == jnp.arange(N_EXPERTS, dtype=jnp.int32)[None, :]).astype(jnp.int32)
    csum = jnp.cumsum(onehot, axis=0)
    rank = jnp.sum(onehot * (csum - 1), axis=1)
    counts = csum[-1]
    padded = (counts + rb - 1) // rb * rb
    seg_end = jnp.cumsum(padded)
    seg_start = seg_end - padded
    dest = (jnp.sum(onehot * seg_start[None, :], axis=1) + rank).astype(jnp.int32)
    pad_lo = (seg_start + counts).astype(jnp.int32)
    pad_hi = seg_end.astype(jnp.int32)
    per = (padded + MOE_ITEM_ROWS - 1) // MOE_ITEM_ROWS
    item_end = jnp.cumsum(per)
    item_start = item_end - per
    ids = jnp.arange(n_items, dtype=jnp.int32)
    e = jnp.minimum(jnp.searchsorted(item_end, ids, side="right"), N_EXPERTS - 1).astype(jnp.int32)
    valid = ids < item_end[-1]
    piece = ids - item_start[e]
    row0 = seg_start[e] + piece * MOE_ITEM_ROWS
    rows = jnp.clip(padded[e] - piece * MOE_ITEM_ROWS, 0, MOE_ITEM_ROWS)
    nch = jnp.where(valid, rows // rb, 0).astype(jnp.int32)
    last_e = e[jnp.maximum(item_end[-1] - 1, 0)]
    item_e = jnp.where(valid, e, last_e).astype(jnp.int32)
    item_row0 = jnp.where(valid, row0, 0).astype(jnp.int32)
    return dest, pad_lo, pad_hi, (item_e, item_row0, nch, pad_hi[-1:])


def _pad_rows(a, rows):
    return jnp.concatenate([a, jnp.zeros((rows - a.shape[0],) + a.shape[1:], a.dtype)], axis=0)


def kernel(x_prompt, x_sample, cache_k, cache_v, state_ret, page_table, p_prompt, p_sample,
           norm_mix, w_in, q_norm, k_norm, w_o, norm_ffn, w_router, b_router, w_gu, b_gu,
           w_down, b_down, norm_ple, w_ple_gate, w_ple_proj):
    depth = norm_mix.shape[0]
    assert depth == 1
    batch, seq_len, d = x_prompt.shape
    n_dec, dec_seq, _ = x_sample.shape
    assert dec_seq == 1 and n_dec <= SAMPLE_ROWS
    past_len = page_table.shape[1] * cache_k.shape[2]
    ret_w = RET_HEADS * RET_DK
    moba_w = MOBA_HEADS * MOBA_HEAD_DIM
    off = [0, ret_w, 2 * ret_w, 3 * ret_w, 4 * ret_w, 4 * ret_w + moba_w, 4 * ret_w + 2 * moba_w]

    w_in_bf = w_in[0].astype(BF16)
    w_o_bf = w_o[0].astype(BF16)
    wg_ple_bf = w_ple_gate[0].astype(BF16)
    wp_ple_bf = w_ple_proj[0].astype(BF16)
    wr_pad = jnp.concatenate([w_router[0], jnp.zeros((d, LANES - N_EXPERTS), F32)],
                             axis=1).astype(BF16)
    br_pad = jnp.concatenate([b_router[0], jnp.full((LANES - N_EXPERTS,), -jnp.inf, F32)])[None, :]
    qn, kn = q_norm, k_norm

    t_p = batch * seq_len
    xp = x_prompt.reshape(t_p, d)
    tm = 1024
    tables_p = _rope_tables(jnp.arange(seq_len, dtype=jnp.int32))
    z_p, k_p, v_p = _inproj(xp, norm_mix, w_in_bf, tables_p, qn, kn, seq_len, tm)
    ret_p, state_p = _ret_prompt(z_p, batch, seq_len)
    moba_p = _moba_prompt(z_p, batch, seq_len)

    xs_rows = _pad_rows(x_sample.reshape(n_dec, d), SAMPLE_ROWS)
    tables_s = _rope_tables(jnp.full((SAMPLE_ROWS,), past_len, jnp.int32))
    z_s, k_s, v_s = _inproj(xs_rows, norm_mix, w_in_bf, tables_s, qn, kn, SAMPLE_ROWS, SAMPLE_ROWS)
    zs = z_s[:n_dec]
    ret_s, state_s = _ret_sample(zs[:, off[0]:off[1]], zs[:, off[1]:off[2]], zs[:, off[2]:off[3]],
                                 zs[:, off[3]:off[4]], state_ret[0])
    heads = lambda a: a.reshape(n_dec, MOBA_HEADS, MOBA_HEAD_DIM)
    mk_s, mv_s = k_s[:n_dec], v_s[:n_dec]
    pool_shape = cache_k.shape[1:]
    moba_s = _moba_sample(heads(zs[:, off[4]:off[5]]), heads(mk_s), heads(mv_s),
                          cache_k.reshape(pool_shape), cache_v.reshape(pool_shape), page_table)

    h_p, hn_p, idx_p, gate_p = _outproj(xp, ret_p, moba_p, w_o_bf, norm_ffn, wr_pad, br_pad, 256)
    h_s, hn_s, idx_s, gate_s = _outproj(xs_rows, _pad_rows(ret_s, SAMPLE_ROWS),
                                        _pad_rows(moba_s, SAMPLE_ROWS), w_o_bf, norm_ffn,
                                        wr_pad, br_pad, SAMPLE_ROWS)

    n_tok = t_p + n_dec
    top_idx = jnp.concatenate([idx_p[:, :TOP_K], idx_s[:n_dec, :TOP_K]], axis=0)
    n_assign = n_tok * TOP_K
    n_chunks = n_assign // MOE_ROW_BLOCK + N_EXPERTS
    n_slots = n_chunks * MOE_ROW_BLOCK
    n_items = N_EXPERTS + pl.cdiv(n_assign, MOE_ITEM_ROWS)
    dest, pad_lo, pad_hi, items = _route(top_idx, n_items)
    hn3 = jnp.concatenate([hn_p, hn_s[:n_dec]], axis=0).reshape(n_tok, ROW_PIECES, d // ROW_PIECES)
    xs3 = _dispatch(hn3, dest, pad_lo, pad_hi, n_slots)
    out2 = _moe(xs3, items, w_gu[0], b_gu[0], w_down[0], b_down[0])
    dest_pad = jnp.concatenate([dest, jnp.zeros(((SAMPLE_ROWS - n_dec) * TOP_K,), jnp.int32)])
    h2_p = _combine(dest_pad, h_p, gate_p, out2, 0, COMBINE_TOKENS)
    h2_s = _combine(dest_pad, h_s, gate_s, out2, t_p, SAMPLE_ROWS)

    y_p = _ple(h2_p, p_prompt[0].reshape(t_p, -1), norm_ple, wg_ple_bf, wp_ple_bf, 512)
    y_s = _ple(h2_s, _pad_rows(p_sample[0].reshape(n_dec, -1), SAMPLE_ROWS), norm_ple,
               wg_ple_bf, wp_ple_bf, SAMPLE_ROWS)

    kv = lambda a, n, l: a.reshape(1, n, l, MOBA_HEADS, MOBA_HEAD_DIM)
    return (y_p.reshape(batch, seq_len, d), y_s[:n_dec].reshape(n_dec, 1, d),
            kv(k_p, batch, seq_len), kv(v_p, batch, seq_len),
            state_p[None], kv(mk_s, n_dec, 1), kv(mv_s, n_dec, 1), state_s[None])
```

```python
import functools
import math

import jax
import jax.numpy as jnp
from jax import lax
from jax.experimental import pallas as pl
from jax.experimental.pallas import tpu as pltpu

F32 = jnp.float32
BF16 = jnp.bfloat16

RET_HEADS = 4
RET_DK = 256
RET_DV = 256
RET_ROPE_THETA = 10000.0
MOBA_HEADS = 8
MOBA_HEAD_DIM = 128
MOBA_BLOCK = 256
MOBA_TOPK = 3
ROPE_THETA = 500000.0
ROPE_DIM = MOBA_HEAD_DIM // 4
N_EXPERTS = 32
TOP_K = 4
SWIGLU_LIMIT = 7.0
SWIGLU_ALPHA = 1.702
EPS = 1e-6

LANES = 128
SUBLANES = 8
VMEM_LIMIT_BYTES = 56 * 1024 * 1024

SAMPLE_ROWS = 16
RET_CHUNK = 256
MOE_ROW_BLOCK = 128
MOE_MATMUL_ROWS = 256
MOE_ITEM_ROWS = 1536
MOE_FF_TILE = 512
MOE_VMEM_LIMIT_BYTES = 60 * 1024 * 1024
ROW_PIECES = 16
COMBINE_TOKENS = 128
MOBA_KV_GROUP = 4


def _cparams(semantics, vmem=VMEM_LIMIT_BYTES):
    return pltpu.CompilerParams(dimension_semantics=semantics, vmem_limit_bytes=vmem)


def _nt_dot(a, b, **kw):
    return lax.dot_general(a, b, (((1,), (1,)), ((), ())), preferred_element_type=F32, **kw)


def _rms(x):
    return x * lax.rsqrt(jnp.mean(x * x, axis=-1, keepdims=True) + EPS)


def _rnd(a):
    return a.astype(BF16).astype(F32)


_IN_TN = 512


def _inproj_kernel(x_ref, g_ref, w_ref, cr_ref, sr_ref, cm_ref, sa_ref, sb_ref, qn_ref, kn_ref,
                   z_ref, k_ref, v_ref, xn_ref):
    j = pl.program_id(1)
    n_ret = RET_HEADS * RET_DK // _IN_TN
    n_moba = MOBA_HEADS * MOBA_HEAD_DIM // _IN_TN
    moba0 = 4 * n_ret

    @pl.when(j == 0)
    def _():
        xn_ref[...] = (_rms(x_ref[...]) * g_ref[...]).astype(BF16)

    acc = jnp.dot(xn_ref[...], w_ref[...], preferred_element_type=F32)

    @pl.when(j < 2 * n_ret)
    def _():
        half = RET_DK // 2
        c = cr_ref[...]
        s = sr_ref[...]
        scale = jnp.where(j < n_ret, 1.0, RET_DK ** -0.5).astype(F32)
        for hh in range(_IN_TN // RET_DK):
            x1 = acc[:, hh * RET_DK:hh * RET_DK + half]
            x2 = acc[:, hh * RET_DK + half:(hh + 1) * RET_DK]
            z_ref[:, hh * RET_DK:hh * RET_DK + half] = (x1 * c - x2 * s) * scale
            z_ref[:, hh * RET_DK + half:(hh + 1) * RET_DK] = (x2 * c + x1 * s) * scale

    @pl.when(((j >= 2 * n_ret) & (j < moba0)) | (j >= moba0 + 2 * n_moba))
    def _():
        z_ref[...] = acc

    @pl.when(j >= moba0 + 2 * n_moba)
    def _():
        v_ref[...] = acc

    @pl.when((j >= moba0) & (j < moba0 + 2 * n_moba))
    def _():
        gain = jnp.where(j < moba0 + n_moba, qn_ref[...], kn_ref[...])
        for hh in range(_IN_TN // MOBA_HEAD_DIM):
            t = _rms(acc[:, hh * MOBA_HEAD_DIM:(hh + 1) * MOBA_HEAD_DIM]) * gain
            up = pltpu.roll(t, MOBA_HEAD_DIM - ROPE_DIM // 2, 1)
            dn = pltpu.roll(t, ROPE_DIM // 2, 1)
            z_ref[:, hh * MOBA_HEAD_DIM:(hh + 1) * MOBA_HEAD_DIM] = (
                t * cm_ref[...] + up * sa_ref[...] + dn * sb_ref[...])

    @pl.when((j >= moba0 + n_moba) & (j < moba0 + 2 * n_moba))
    def _():
        k_ref[...] = z_ref[...]


def _rope_tables(pos):
    posf = pos.astype(F32)[:, None]
    half = RET_DK // 2
    inv = RET_ROPE_THETA ** (-2.0 * jnp.arange(half, dtype=F32) / RET_DK)
    ang = posf * inv[None, :]
    cr, sr = jnp.cos(ang), jnp.sin(ang)
    mh = ROPE_DIM // 2
    inv_m = ROPE_THETA ** (-2.0 * jnp.arange(mh, dtype=F32) / ROPE_DIM)
    ang_m = posf * inv_m[None, :]
    cos_m, sin_m = jnp.cos(ang_m), jnp.sin(ang_m)
    n = pos.shape[0]
    rest = MOBA_HEAD_DIM - ROPE_DIM
    cm = jnp.concatenate([cos_m, cos_m, jnp.ones((n, rest), F32)], axis=1)
    sa = jnp.concatenate([-sin_m, jnp.zeros((n, MOBA_HEAD_DIM - mh), F32)], axis=1)
    sb = jnp.concatenate([jnp.zeros((n, mh), F32), sin_m, jnp.zeros((n, rest), F32)], axis=1)
    return cr, sr, cm, sa, sb


def _inproj(x2d, norm, w_bf, tables, qn, kn, seq_len, tm):
    t, d = x2d.shape
    n_out = w_bf.shape[1]
    tpos = seq_len // tm
    moba_w = MOBA_HEADS * MOBA_HEAD_DIM
    n_moba = moba_w // _IN_TN
    k0 = (n_out - 2 * moba_w) // _IN_TN
    cr, sr, cm, sa, sb = tables
    tab = pl.BlockSpec((tm, LANES), lambda i, j: (i % tpos, 0))
    vec = pl.BlockSpec((1, LANES), lambda i, j: (0, 0))
    return pl.pallas_call(
        _inproj_kernel,
        grid=(t // tm, n_out // _IN_TN),
        in_specs=[
            pl.BlockSpec((tm, d), lambda i, j: (i, 0)),
            pl.BlockSpec((1, d), lambda i, j: (0, 0)),
            pl.BlockSpec((d, _IN_TN), lambda i, j: (0, j)),
            tab, tab, tab, tab, tab, vec, vec,
        ],
        out_specs=[pl.BlockSpec((tm, _IN_TN), lambda i, j: (i, j)),
                   pl.BlockSpec((tm, _IN_TN), lambda i, j: (i, jnp.clip(j - k0, 0, n_moba - 1))),
                   pl.BlockSpec((tm, _IN_TN), lambda i, j: (i, jnp.clip(j - k0 - n_moba, 0, n_moba - 1)))],
        out_shape=[jax.ShapeDtypeStruct((t, n_out), F32),
                   jax.ShapeDtypeStruct((t, moba_w), F32), jax.ShapeDtypeStruct((t, moba_w), F32)],
        scratch_shapes=[pltpu.VMEM((tm, d), BF16)],
        compiler_params=_cparams(("arbitrary", "arbitrary")),
        name="inproj",
    )(x2d, norm, w_bf, cr, sr, cm, sa, sb, qn, kn)


def _ret_gate(o, g):
    return _rms(o) * (g * jax.nn.sigmoid(g))


def _ret_prompt_kernel(logg_ref, q_ref, k_ref, v_ref, g_ref, o_ref, st_ref, r_ref):
    h = pl.program_id(1)
    c = pl.program_id(2)
    n = q_ref.shape[0]

    @pl.when(c == 0)
    def _():
        r_ref[...] = jnp.zeros_like(r_ref)

    lg = logg_ref[h]
    i = lax.broadcasted_iota(jnp.int32, (n, 1), 0).astype(F32)
    jj = lax.broadcasted_iota(jnp.int32, (1, n), 1).astype(F32)
    diff = i - jj
    inner = jnp.where(diff >= 0, jnp.exp(lg * jnp.maximum(diff, 0.0)), 0.0)
    q_dec = jnp.exp(lg * (i + 1.0))
    k_dec = jnp.exp(lg * (n - 1.0 - i))
    chunk_dec = jnp.exp(lg * jnp.full((1, 1), float(n), F32))

    q = q_ref[...]
    k = k_ref[...]
    vb = v_ref[...].astype(BF16)
    r = r_ref[...]
    s = _nt_dot(q.astype(BF16), k.astype(BF16)) * inner
    o = (jnp.dot(s.astype(BF16), vb, preferred_element_type=F32)
         + jnp.dot((q * q_dec).astype(BF16), r.astype(BF16), preferred_element_type=F32))
    kd_t = (k * k_dec).T.astype(BF16)
    r_new = r * chunk_dec + jnp.dot(kd_t, vb, preferred_element_type=F32)
    r_ref[...] = r_new
    o_ref[...] = _ret_gate(o, g_ref[...])

    @pl.when(c == pl.num_programs(2) - 1)
    def _():
        st_ref[0, 0] = r_new


def _ret_log_decay():
    return jnp.log1p(-jnp.exp2(-5.0 - jnp.arange(RET_HEADS, dtype=F32)))


def _ret_prompt(z, batch, seq_len):
    nc = seq_len // RET_CHUNK
    w = RET_DK

    def col(off):
        return pl.BlockSpec((RET_CHUNK, w), lambda b, h, c: (b * nc + c, off + h))

    return pl.pallas_call(
        _ret_prompt_kernel,
        grid=(batch, RET_HEADS, nc),
        in_specs=[pl.BlockSpec(memory_space=pltpu.SMEM),
                  col(0), col(RET_HEADS), col(2 * RET_HEADS), col(3 * RET_HEADS)],
        out_specs=[
            pl.BlockSpec((RET_CHUNK, RET_DV), lambda b, h, c: (b * nc + c, h)),
            pl.BlockSpec((1, 1, RET_DK, RET_DV), lambda b, h, c: (b, h, 0, 0)),
        ],
        scratch_shapes=[pltpu.VMEM((RET_DK, RET_DV), F32)],
        out_shape=[
            jax.ShapeDtypeStruct((batch * seq_len, RET_HEADS * RET_DV), F32),
            jax.ShapeDtypeStruct((batch, RET_HEADS, RET_DK, RET_DV), F32),
        ],
        compiler_params=_cparams(("parallel", "parallel", "arbitrary")),
        name="ret_prompt",
    )(_ret_log_decay(), z, z, z, z)


def _ret_sample_kernel(logg_ref, q_ref, kc_ref, kr_ref, v_ref, g_ref, s0_ref, o_ref, st_ref):
    h = pl.program_id(1)
    dec = jnp.exp(logg_ref[h] * jnp.ones((1, 1), F32))
    q = q_ref[0, 0]
    v = v_ref[0, 0]
    r0 = s0_ref[0, 0]
    qk = jnp.sum(q * kr_ref[0, 0], axis=-1, keepdims=True)
    q8 = jnp.broadcast_to(q * dec, (SUBLANES, RET_DK)).astype(BF16)
    qr = jnp.dot(q8, r0.astype(BF16), preferred_element_type=F32)[0:1]
    o = qk * v + qr
    st_ref[0, 0] = r0 * dec + kc_ref[0, 0] * v
    o_ref[0, 0] = _ret_gate(o, g_ref[0, 0])


def _ret_sample(q, k, v, g, state):
    n = q.shape[0]
    row = lambda a: a.reshape(n, RET_HEADS, 1, RET_DK)
    rspec = pl.BlockSpec((1, 1, 1, RET_DK), lambda b, h: (b, h, 0, 0))
    mspec = pl.BlockSpec((1, 1, RET_DK, RET_DV), lambda b, h: (b, h, 0, 0))
    o, st = pl.pallas_call(
        _ret_sample_kernel,
        grid=(n, RET_HEADS),
        in_specs=[pl.BlockSpec(memory_space=pltpu.SMEM),
                  rspec, pl.BlockSpec((1, 1, RET_DK, 1), lambda b, h: (b, h, 0, 0)),
                  rspec, rspec, rspec, mspec],
        out_specs=[rspec, mspec],
        out_shape=[jax.ShapeDtypeStruct((n, RET_HEADS, 1, RET_DV), F32),
                   jax.ShapeDtypeStruct(state.shape, F32)],
        compiler_params=_cparams(("parallel", "parallel")),
        name="ret_sample",
    )(_ret_log_decay(), row(q), k.reshape(n, RET_HEADS, RET_DK, 1), row(k), row(v), row(g), state)
    return o.reshape(n, RET_HEADS * RET_DV), st


MOBA_HEAD_GROUP = 2


def _moba_prompt_kernel(q_ref, k_ref, v_ref, o_ref, kb_ref, vt_ref, km_ref, sel_ref, acc_ref):
    qi = pl.program_id(2)
    blk = MOBA_BLOCK
    hd = MOBA_HEAD_DIM
    heads = range(MOBA_HEAD_GROUP)
    nb = k_ref.shape[0] // blk
    scale = hd ** -0.5
    grp = MOBA_KV_GROUP
    span = grp * blk

    @pl.when(qi == 0)
    def _():
        for hh in heads:
            cols = slice(hh * hd, (hh + 1) * hd)
            kb_ref[hh] = k_ref[:, cols].astype(BF16)
            for n in range(nb):
                rows = slice(n * blk, (n + 1) * blk)
                vt_ref[hh, :, rows] = v_ref[rows, cols].T.astype(BF16)
                km_ref[hh, n:n + 1, :] = jnp.mean(k_ref[rows, cols], axis=0, keepdims=True)

    own = pl.ds(pl.multiple_of(qi * blk, blk), blk)
    key = lax.broadcasted_iota(jnp.int32, (blk, blk), 0)
    qry = lax.broadcasted_iota(jnp.int32, (blk, blk), 1)
    kblock = lax.broadcasted_iota(jnp.int32, (nb, blk), 0)
    past = kblock < qi
    qbs, state = [], []
    for hh in heads:
        qb = q_ref[:, hh * hd:(hh + 1) * hd].astype(BF16)
        qbs.append(qb)
        gate = _nt_dot(km_ref[hh].astype(BF16), qb)
        g = jnp.where(past, gate, -jnp.inf)
        cnt = jnp.zeros((nb, blk), F32)
        for m in range(nb):
            gm = g[m:m + 1, :]
            cnt = cnt + jnp.where(gm > g, 1.0, jnp.where((gm == g) & (kblock > m), 1.0, 0.0))
        sel_ref[hh] = jnp.where(past & (cnt < MOBA_TOPK), 1.0, 0.0)
        s = _nt_dot(kb_ref[hh, own, :], qb) * scale
        s = jnp.where(key <= qry, s, -jnp.inf)
        m0 = jnp.max(s, axis=0, keepdims=True)
        p = jnp.exp(s - m0)
        acc_ref[hh] = jnp.dot(vt_ref[hh, :, own], p.astype(BF16), preferred_element_type=F32)
        state.append((m0, jnp.sum(p, axis=0, keepdims=True)))

    def body(j, carry):
        rows = pl.ds(pl.multiple_of(j * span, span), span)
        out = []
        for hh in heads:
            m_i, l_i = carry[hh]
            sn = _nt_dot(kb_ref[hh, rows, :], qbs[hh]) * scale
            allowed = jnp.concatenate(
                [jnp.broadcast_to(sel_ref[hh, pl.ds(j * grp + t, 1), :], (blk, blk))
                 for t in range(grp)], axis=0)
            sn = jnp.where(allowed > 0.0, sn, -jnp.inf)
            m_new = jnp.maximum(m_i, jnp.max(sn, axis=0, keepdims=True))
            alpha = jnp.exp(m_i - m_new)
            pn = jnp.exp(sn - m_new)
            acc_ref[hh] = alpha * acc_ref[hh] + jnp.dot(vt_ref[hh, :, rows], pn.astype(BF16),
                                                        preferred_element_type=F32)
            out.append((m_new, alpha * l_i + jnp.sum(pn, axis=0, keepdims=True)))
        return tuple(out)

    final = lax.fori_loop(0, (qi + grp - 1) // grp, body, tuple(state))
    for hh in heads:
        o_ref[:, hh * hd:(hh + 1) * hd] = (acc_ref[hh] / final[hh][1]).T


def _moba_prompt(z, batch, seq_len):
    nq = seq_len // MOBA_BLOCK
    hd = MOBA_HEAD_DIM
    hg = MOBA_HEAD_GROUP
    w = hg * hd
    first = (2 * RET_HEADS * RET_DK + 2 * RET_HEADS * RET_DV) // w
    koff = first + MOBA_HEADS // hg
    voff = koff + MOBA_HEADS // hg
    assert nq % MOBA_KV_GROUP == 0 and MOBA_HEADS % hg == 0
    full = lambda off: pl.BlockSpec((seq_len, w), lambda b, h, i: (b, off + h))
    return pl.pallas_call(
        _moba_prompt_kernel,
        grid=(batch, MOBA_HEADS // hg, nq),
        in_specs=[pl.BlockSpec((MOBA_BLOCK, w), lambda b, h, i: (b * nq + i, first + h)),
                  full(koff), full(voff)],
        out_specs=pl.BlockSpec((MOBA_BLOCK, w), lambda b, h, i: (b * nq + i, h)),
        out_shape=jax.ShapeDtypeStruct((batch * seq_len, MOBA_HEADS * hd), F32),
        scratch_shapes=[pltpu.VMEM((hg, seq_len, hd), BF16), pltpu.VMEM((hg, hd, seq_len), BF16),
                        pltpu.VMEM((hg, nq, hd), F32), pltpu.VMEM((hg, nq, MOBA_BLOCK), F32),
                        pltpu.VMEM((hg, hd, MOBA_BLOCK), F32)],
        compiler_params=_cparams(("parallel", "parallel", "arbitrary")),
        name="moba_prompt",
    )(z, z, z)


MOBA_SCAN_PAGES = 8


def _moba_scan_kernel(pt_ref, q_ref, *refs):
    k_refs = refs[:MOBA_SCAN_PAGES]
    sel_ref, gate_ref = refs[MOBA_SCAN_PAGES:]
    p = pl.program_id(1)
    ppb = MOBA_BLOCK // k_refs[0].shape[1]
    blocks_per_step = MOBA_SCAN_PAGES // ppb

    @pl.when(p == 0)
    def _():
        gate_ref[...] = jnp.zeros_like(gate_ref)

    q = _rnd(q_ref[0])
    lane = lax.broadcasted_iota(jnp.int32, gate_ref.shape, 1)
    gate = gate_ref[...]
    for jb in range(blocks_per_step):
        k_sum = sum(jnp.sum(k_refs[jb * ppb + j][0], axis=0) for j in range(ppb))
        g = jnp.sum(q * _rnd(k_sum * (1.0 / MOBA_BLOCK)), axis=1, keepdims=True)
        gate = jnp.where(lane == p * blocks_per_step + jb, g, gate)
    gate_ref[...] = gate

    @pl.when(p == pl.num_programs(1) - 1)
    def _():
        g = jnp.where(lane < pl.num_programs(1) * blocks_per_step, gate, -jnp.inf)
        out = jnp.zeros(gate_ref.shape, jnp.int32)
        for t in range(MOBA_TOPK):
            mx = jnp.max(g, axis=1, keepdims=True)
            idx = jnp.min(jnp.where(g == mx, lane, LANES), axis=1, keepdims=True)
            out = jnp.where(lane == t, idx, out)
            g = jnp.where(lane == idx, -jnp.inf, g)
        sel_ref[0] = out


def _moba_attend_kernel(pt_ref, selp_ref, q_ref, kn_ref, vn_ref, *refs):
    nsrc = (len(refs) - 1) // 2
    k_refs = refs[:nsrc]
    v_refs = refs[nsrc:2 * nsrc]
    o_ref = refs[2 * nsrc]
    h = pl.program_id(1)
    scale = MOBA_HEAD_DIM ** -0.5
    mine2 = lax.broadcasted_iota(jnp.int32, (MOBA_HEADS, MOBA_HEAD_DIM), 0) == h
    mine3 = lax.broadcasted_iota(jnp.int32, (1, MOBA_HEADS, 1), 1) == h
    all2 = lambda a, op: op(op(a, axis=1, keepdims=True), axis=0, keepdims=True)
    all3 = lambda a, op: op(op(a, axis=0, keepdims=True), axis=1, keepdims=True)
    q = _rnd(q_ref[0])
    s_new = all2(jnp.where(mine2, q * _rnd(kn_ref[0]), 0.0), jnp.sum) * scale
    ss = [jnp.where(mine3, jnp.sum(_rnd(kr[0]) * q[None], axis=-1, keepdims=True) * scale, -jnp.inf)
          for kr in k_refs]
    mx = s_new.reshape(1, 1, 1)
    for sj in ss:
        mx = jnp.maximum(mx, all3(sj, jnp.max))
    p_new = jnp.exp(s_new.reshape(1, 1, 1) - mx)
    ps = [jnp.exp(sj - mx) for sj in ss]
    den = p_new
    for pj in ps:
        den = den + all3(pj, jnp.sum)
    inv = 1.0 / den
    acc = jnp.where(mine2, _rnd(p_new * inv).reshape(1, 1) * _rnd(vn_ref[0]), 0.0)
    for pj, vr in zip(ps, v_refs):
        acc = acc + jnp.sum(_rnd(pj * inv) * _rnd(vr[0]), axis=0)
    o_ref[0, 0] = jnp.sum(acc, axis=0, keepdims=True)


def _moba_sample(q, k_new, v_new, cache_k, cache_v, page_table):
    n, npages = page_table.shape
    page = cache_k.shape[1]
    hd = MOBA_HEAD_DIM
    ppb = MOBA_BLOCK // page
    pp = MOBA_SCAN_PAGES
    assert npages % pp == 0 and npages // ppb <= LANES
    page_block = (1, page, MOBA_HEADS, hd)
    tok2 = pl.BlockSpec((1, MOBA_HEADS, hd), lambda b, p, pt: (b, 0, 0))
    scan_specs = [pl.BlockSpec(page_block, (lambda b, p, pt, j=j: (pt[b, p * pp + j], 0, 0, 0)))
                  for j in range(pp)]
    sel = pl.pallas_call(
        _moba_scan_kernel,
        grid_spec=pltpu.PrefetchScalarGridSpec(
            num_scalar_prefetch=1,
            grid=(n, npages // pp),
            in_specs=[tok2] + scan_specs,
            out_specs=pl.BlockSpec((1, MOBA_HEADS, LANES), lambda b, p, pt: (b, 0, 0)),
            scratch_shapes=[pltpu.VMEM((MOBA_HEADS, LANES), F32)],
        ),
        out_shape=jax.ShapeDtypeStruct((n, MOBA_HEADS, LANES), jnp.int32),
        compiler_params=_cparams(("parallel", "arbitrary")),
        name="moba_scan",
    )(page_table, q, *([cache_k] * pp))
    selp = (sel[:, :, :MOBA_TOPK, None] * ppb + jnp.arange(ppb, dtype=jnp.int32)).reshape(-1)
    nsrc = MOBA_TOPK * ppb

    def page_spec(j):
        def index(b, h, pt, sp):
            return (pt[b, sp[(b * MOBA_HEADS + h) * nsrc + j]], 0, 0, 0)
        return pl.BlockSpec(page_block, index)

    tok = pl.BlockSpec((1, MOBA_HEADS, hd), lambda b, h, pt, sp: (b, 0, 0))
    pages = [page_spec(j) for j in range(nsrc)]
    out = pl.pallas_call(
        _moba_attend_kernel,
        grid_spec=pltpu.PrefetchScalarGridSpec(
            num_scalar_prefetch=2,
            grid=(n, MOBA_HEADS),
            in_specs=[tok, tok, tok] + pages + pages,
            out_specs=pl.BlockSpec((1, 1, 1, hd), lambda b, h, pt, sp: (b, h, 0, 0)),
        ),
        out_shape=jax.ShapeDtypeStruct((n, MOBA_HEADS, 1, hd), F32),
        compiler_params=_cparams(("parallel", "parallel")),
        name="moba_attend",
    )(page_table, selp, q, k_new, v_new, *([cache_k] * nsrc), *([cache_v] * nsrc))
    return out.reshape(n, MOBA_HEADS * hd)


def _outproj_kernel(x_ref, ret_ref, moba_ref, wo_ref, nf_ref, wr_ref, br_ref,
                    h_ref, hn_ref, idx_ref, gate_ref):
    rw = ret_ref.shape[1]
    h = (x_ref[...]
         + jnp.dot(ret_ref[...].astype(BF16), wo_ref[:rw, :], preferred_element_type=F32)
         + jnp.dot(moba_ref[...].astype(BF16), wo_ref[rw:, :], preferred_element_type=F32))
    h_ref[...] = h
    hn = _rms(h) * nf_ref[...]
    hn_ref[...] = hn
    logits = jnp.dot(hn.astype(BF16), wr_ref[...], preferred_element_type=F32) + br_ref[...]
    lane = lax.broadcasted_iota(jnp.int32, logits.shape, 1)
    vals, idxs = [], []
    for _ in range(TOP_K):
        mx = jnp.max(logits, axis=1, keepdims=True)
        ix = jnp.min(jnp.where(logits == mx, lane, LANES), axis=1, keepdims=True)
        vals.append(mx)
        idxs.append(ix)
        logits = jnp.where(lane == ix, -jnp.inf, logits)
    es = [jnp.exp(v - vals[0]) for v in vals]
    den = es[0] + es[1] + es[2] + es[3]
    idx_out = jnp.zeros(lane.shape, jnp.int32)
    gate_out = jnp.zeros(lane.shape, F32)
    for t in range(TOP_K):
        idx_out = jnp.where(lane == t, idxs[t], idx_out)
        gate_out = jnp.where(lane == t, es[t] / den, gate_out)
    idx_ref[...] = idx_out
    gate_ref[...] = gate_out


def _outproj(x2d, ret, moba, wo_bf, norm_ffn, wr_pad, br_pad, tm):
    t, d = x2d.shape
    rw, mw = ret.shape[1], moba.shape[1]
    const = lambda shape: pl.BlockSpec(shape, lambda i: (0, 0))
    rows = lambda w: pl.BlockSpec((tm, w), lambda i: (i, 0))
    return pl.pallas_call(
        _outproj_kernel,
        grid=(t // tm,),
        in_specs=[rows(d), rows(rw), rows(mw), const((rw + mw, d)), const((1, d)),
                  const((d, LANES)), const((1, LANES))],
        out_specs=[rows(d), rows(d), rows(LANES), rows(LANES)],
        out_shape=[jax.ShapeDtypeStruct((t, d), F32), jax.ShapeDtypeStruct((t, d), F32),
                   jax.ShapeDtypeStruct((t, LANES), jnp.int32),
                   jax.ShapeDtypeStruct((t, LANES), F32)],
        compiler_params=_cparams(("parallel",)),
        name="outproj",
    )(x2d, ret, moba, wo_bf, norm_ffn, wr_pad, br_pad)


def _dispatch_kernel(dest_ref, pad_lo_ref, pad_hi_ref, hn_ref, xs_hbm, zero_ref, sem, zsem, *,
                     ntok):
    i = pl.program_id(0)
    tb = hn_ref.shape[0]
    first = i * tb
    count = jnp.minimum(tb, ntok - first)

    def row_copy(r, k):
        return pltpu.make_async_copy(hn_ref.at[r], xs_hbm.at[dest_ref[(first + r) * TOP_K + k]], sem)

    def issue(r, c):
        for k in range(TOP_K):
            row_copy(r, k).start()
        return c

    lax.fori_loop(0, count, issue, 0)

    @pl.when(i == 0)
    def _():
        zero_ref[...] = jnp.zeros_like(zero_ref)

        def pad_copy(s):
            return pltpu.make_async_copy(zero_ref.at[0], xs_hbm.at[s], zsem)

        def per_expert(e, c):
            def pad_row(s, c2):
                pad_copy(s).start()
                return c2
            lax.fori_loop(pad_lo_ref[e], pad_hi_ref[e], pad_row, 0)
            return c

        lax.fori_loop(0, N_EXPERTS, per_expert, 0)

        rb = zero_ref.shape[0]
        tail0 = pad_hi_ref[N_EXPERTS - 1]
        n_tail = (xs_hbm.shape[0] - tail0) // rb

        def tail_copy(c):
            rows = pl.ds(pl.multiple_of(tail0 + c * rb, rb), rb)
            return pltpu.make_async_copy(zero_ref, xs_hbm.at[rows], zsem)

        def tail_start(c, c2):
            tail_copy(c).start()
            return c2

        lax.fori_loop(0, n_tail, tail_start, 0)

        def per_expert_wait(e, c):
            def pad_wait(s, c2):
                pad_copy(s).wait()
                return c2
            lax.fori_loop(pad_lo_ref[e], pad_hi_ref[e], pad_wait, 0)
            return c

        lax.fori_loop(0, N_EXPERTS, per_expert_wait, 0)

        def tail_wait(c, c2):
            tail_copy(c).wait()
            return c2

        lax.fori_loop(0, n_tail, tail_wait, 0)

    @pl.when(count == tb)
    def _():
        for k in range(TOP_K):
            pltpu.make_async_copy(hn_ref, xs_hbm.at[pl.ds(0, tb)], sem).wait()

    @pl.when(count < tb)
    def _():
        def drain(r, c):
            for k in range(TOP_K):
                row_copy(r, k).wait()
            return c

        lax.fori_loop(0, count, drain, 0)


def _dispatch(hn3, dest_flat, pad_lo, pad_hi, n_slots):
    ntok = hn3.shape[0]
    tb = COMBINE_TOKENS
    return pl.pallas_call(
        functools.partial(_dispatch_kernel, ntok=ntok),
        grid_spec=pltpu.PrefetchScalarGridSpec(
            num_scalar_prefetch=3,
            grid=(pl.cdiv(ntok, tb),),
            in_specs=[pl.BlockSpec((tb,) + hn3.shape[1:], lambda i, d, lo, hi: (i, 0, 0))],
            out_specs=pl.BlockSpec(memory_space=pl.ANY),
            scratch_shapes=[pltpu.VMEM((MOE_ROW_BLOCK,) + hn3.shape[1:], F32),
                            pltpu.SemaphoreType.DMA(()), pltpu.SemaphoreType.DMA(())],
        ),
        out_shape=jax.ShapeDtypeStruct((n_slots,) + hn3.shape[1:], F32),
        compiler_params=_cparams(("arbitrary",)),
        name="moe_dispatch",
    )(dest_flat, pad_lo, pad_hi, hn3)


def _moe_kernel(item_e, item_row0, item_nch, tail_ref, xs_hbm, wg_ref, wu_ref, bg_ref, bu_ref,
                wd_ref, bd_ref, out_hbm, x_ref, acc_ref, stage_ref, wgb_ref, wub_ref, wdb_ref,
                sem_in, sem_out):
    it = pl.program_id(0)
    f = pl.program_id(1)
    nf = pl.num_programs(1)
    rb = MOE_ROW_BLOCK
    big = MOE_MATMUL_ROWS
    pieces = ROW_PIECES
    nch = item_nch[it]
    row0 = item_row0[it]
    n_big = nch // (big // rb)
    has_tail = nch % (big // rb) != 0
    tail_row = pl.multiple_of(n_big * big, rb)

    def hbm_rows(start, size):
        return pl.ds(pl.multiple_of((row0 + start) * pieces, rb * pieces), size * pieces)

    def in_copy(start, size, slot):
        return pltpu.make_async_copy(xs_hbm.at[hbm_rows(start, size)],
                                     stage_ref.at[slot, pl.ds(0, size * pieces)], sem_in.at[slot])

    def out_copy(start, size, slot):
        return pltpu.make_async_copy(stage_ref.at[slot, pl.ds(0, size * pieces)],
                                     out_hbm.at[hbm_rows(start, size)], sem_out.at[slot])

    def stage_to_x(start, size, slot):
        rows = pl.ds(start, size)
        for p in range(pieces):
            piece = stage_ref[slot, pl.ds(p, size, stride=pieces), :]
            x_ref[rows, p * LANES:(p + 1) * LANES] = piece.astype(BF16)
        acc_ref[rows, :] = jnp.broadcast_to(bd_ref[0], (size, acc_ref.shape[1]))

    def acc_to_stage(start, size, slot):
        rows = pl.ds(start, size)
        for p in range(pieces):
            stage_ref[slot, pl.ds(p, size, stride=pieces), :] = acc_ref[rows, p * LANES:(p + 1) * LANES]

    @pl.when(nch > 0)
    def _():
        wgb_ref[...] = wg_ref[0].astype(BF16)
        wub_ref[...] = wu_ref[0].astype(BF16)
        wdb_ref[...] = wd_ref[0].astype(BF16)
        bg = bg_ref[0]
        bu = bu_ref[0]

        def compute(start, size):
            rows = pl.ds(start, size)
            x = x_ref[rows, :]
            g = jnp.dot(x, wgb_ref[...], preferred_element_type=F32) + bg
            u = jnp.dot(x, wub_ref[...], preferred_element_type=F32) + bu
            g = jnp.minimum(g, SWIGLU_LIMIT)
            u = jnp.clip(u, -SWIGLU_LIMIT, SWIGLU_LIMIT)
            act = (u + 1.0) * g * jax.nn.sigmoid(SWIGLU_ALPHA * g)
            acc_ref[rows, :] += jnp.dot(act.astype(BF16), wdb_ref[...], preferred_element_type=F32)

        @pl.when(f == 0)
        def _():
            @pl.when(n_big > 0)
            def _():
                in_copy(0, big, 0).start()

            def step(r, c):
                slot = r % 2
                start = pl.multiple_of(r * big, big)

                @pl.when(r + 1 < n_big)
                def _():
                    in_copy(start + big, big, 1 - slot).start()

                in_copy(start, big, slot).wait()
                stage_to_x(start, big, slot)
                compute(start, big)
                return c

            lax.fori_loop(0, n_big, step, 0)

            @pl.when(has_tail)
            def _():
                cp = in_copy(tail_row, rb, 0)
                cp.start()
                cp.wait()
                stage_to_x(tail_row, rb, 0)
                compute(tail_row, rb)

        @pl.when((f > 0) & (f < nf - 1))
        def _():
            def step(r, c):
                compute(pl.multiple_of(r * big, big), big)
                return c

            lax.fori_loop(0, n_big, step, 0)

            @pl.when(has_tail)
            def _():
                compute(tail_row, rb)

        @pl.when(f == nf - 1)
        def _():
            def step(r, c):
                slot = r % 2
                start = pl.multiple_of(r * big, big)
                compute(start, big)

                @pl.when(r >= 2)
                def _():
                    out_copy(start - 2 * big, big, slot).wait()

                acc_to_stage(start, big, slot)
                out_copy(start, big, slot).start()
                return c

            lax.fori_loop(0, n_big, step, 0)

            @pl.when(n_big >= 2)
            def _():
                out_copy(0, big, n_big % 2).wait()

            @pl.when(n_big >= 1)
            def _():
                out_copy(0, big, (n_big - 1) % 2).wait()

            @pl.when(has_tail)
            def _():
                compute(tail_row, rb)
                acc_to_stage(tail_row, rb, 0)
                cp = out_copy(tail_row, rb, 0)
                cp.start()
                cp.wait()

    @pl.when((it == pl.num_programs(0) - 1) & (f == nf - 1))
    def _():
        tail0 = tail_ref[0]
        n_tail = (out_hbm.shape[0] // pieces - tail0) // rb
        stage_ref[0] = jnp.zeros(stage_ref.shape[1:], F32)

        def tail_copy(c):
            rows = pl.ds(pl.multiple_of((tail0 + c * rb) * pieces, rb * pieces), rb * pieces)
            return pltpu.make_async_copy(stage_ref.at[0, pl.ds(0, rb * pieces)], out_hbm.at[rows],
                                         sem_out.at[0])

        def tail_start(c, c2):
            tail_copy(c).start()
            return c2

        def tail_wait(c, c2):
            tail_copy(c).wait()
            return c2

        lax.fori_loop(0, n_tail, tail_start, 0)
        lax.fori_loop(0, n_tail, tail_wait, 0)


def _moe(xs3, items, w_gu, b_gu, w_down, b_down):
    item_e, item_row0, item_nch, tail0 = items
    n_items = item_e.shape[0]
    n_slots = xs3.shape[0]
    d = w_gu.shape[1]
    ff = w_down.shape[1]
    tf = MOE_FF_TILE
    nf = ff // tf
    assert MOE_MATMUL_ROWS == 2 * MOE_ROW_BLOCK and MOE_ITEM_ROWS % MOE_MATMUL_ROWS == 0 and nf >= 2
    xs2 = xs3.reshape(n_slots * ROW_PIECES, LANES)

    def ftile(it, f, nch):
        return jnp.where(nch[it] > 0, f, nf - 1)

    wg_spec = pl.BlockSpec((1, d, tf), lambda it, f, e, r0, nch, t0: (e[it], 0, ftile(it, f, nch)))
    wu_spec = pl.BlockSpec((1, d, tf), lambda it, f, e, r0, nch, t0: (e[it], 0, nf + ftile(it, f, nch)))
    bg_spec = pl.BlockSpec((1, 1, tf), lambda it, f, e, r0, nch, t0: (e[it], 0, ftile(it, f, nch)))
    bu_spec = pl.BlockSpec((1, 1, tf), lambda it, f, e, r0, nch, t0: (e[it], 0, nf + ftile(it, f, nch)))
    wd_spec = pl.BlockSpec((1, tf, d), lambda it, f, e, r0, nch, t0: (e[it], ftile(it, f, nch), 0))
    bd_spec = pl.BlockSpec((1, 1, d), lambda it, f, e, r0, nch, t0: (e[it], 0, 0))
    b_gu3 = b_gu.reshape(N_EXPERTS, 1, 2 * ff)
    b_down3 = b_down.reshape(N_EXPERTS, 1, d)
    out2 = pl.pallas_call(
        _moe_kernel,
        grid_spec=pltpu.PrefetchScalarGridSpec(
            num_scalar_prefetch=4,
            grid=(n_items, nf),
            in_specs=[pl.BlockSpec(memory_space=pl.ANY), wg_spec, wu_spec, bg_spec, bu_spec,
                      wd_spec, bd_spec],
            out_specs=pl.BlockSpec(memory_space=pl.ANY),
            scratch_shapes=[
                pltpu.VMEM((MOE_ITEM_ROWS, d), BF16),
                pltpu.VMEM((MOE_ITEM_ROWS, d), F32),
                pltpu.VMEM((2, MOE_MATMUL_ROWS * ROW_PIECES, LANES), F32),
                pltpu.VMEM((d, tf), BF16), pltpu.VMEM((d, tf), BF16), pltpu.VMEM((tf, d), BF16),
                pltpu.SemaphoreType.DMA((2,)), pltpu.SemaphoreType.DMA((2,)),
            ],
        ),
        out_shape=jax.ShapeDtypeStruct(xs2.shape, F32),
        compiler_params=_cparams(("arbitrary", "arbitrary"), MOE_VMEM_LIMIT_BYTES),
        name="moe_experts",
    )(item_e, item_row0, item_nch, tail0, xs2, w_gu, w_gu, b_gu3, b_gu3, w_down, b_down3)
    return out2


def _combine_kernel(dest_ref, h_ref, gate_ref, out_hbm, o_ref, buf_ref, sem, *, tok_off):
    i = pl.program_id(0)
    tb = h_ref.shape[0]
    pieces = ROW_PIECES

    def row_copy(step, r, k, slot):
        d = dest_ref[(tok_off + step * tb + r) * TOP_K + k]
        src = out_hbm.at[pl.ds(pl.multiple_of(d * pieces, pieces), pieces)]
        dst = buf_ref.at[slot, pl.ds(pl.multiple_of((k * tb + r) * pieces, pieces), pieces)]
        return pltpu.make_async_copy(src, dst, sem.at[slot])

    def issue(step, slot):
        def body(r, c):
            for k in range(TOP_K):
                row_copy(step, r, k, slot).start()
            return c
        lax.fori_loop(0, tb, body, 0)

    @pl.when(i == 0)
    def _():
        issue(0, 0)

    @pl.when(i + 1 < pl.num_programs(0))
    def _():
        issue(i + 1, (i + 1) % 2)

    slot = i % 2
    pltpu.make_async_copy(out_hbm.at[pl.ds(0, TOP_K * tb * pieces)], buf_ref.at[slot],
                          sem.at[slot]).wait()

    for p in range(pieces):
        f = jnp.zeros((tb, LANES), F32)
        for k in range(TOP_K):
            rows = buf_ref[slot, pl.ds(k * tb * pieces + p, tb, stride=pieces), :]
            f = f + gate_ref[:, k:k + 1] * rows
        o_ref[:, p * LANES:(p + 1) * LANES] = h_ref[:, p * LANES:(p + 1) * LANES] + f


def _combine(dest_flat, h, gates, out2, tok_off, tb):
    t, d = h.shape
    return pl.pallas_call(
        functools.partial(_combine_kernel, tok_off=tok_off),
        grid_spec=pltpu.PrefetchScalarGridSpec(
            num_scalar_prefetch=1,
            grid=(t // tb,),
            in_specs=[pl.BlockSpec((tb, d), lambda i, ds: (i, 0)),
                      pl.BlockSpec((tb, LANES), lambda i, ds: (i, 0)),
                      pl.BlockSpec(memory_space=pl.ANY)],
            out_specs=pl.BlockSpec((tb, d), lambda i, ds: (i, 0)),
            scratch_shapes=[pltpu.VMEM((2, TOP_K * tb * ROW_PIECES, LANES), F32),
                            pltpu.SemaphoreType.DMA((2,))],
        ),
        out_shape=jax.ShapeDtypeStruct((t, d), F32),
        compiler_params=_cparams(("arbitrary",)),
        name="moe_combine",
    )(dest_flat, h, gates, out2)


def _ple_kernel(h_ref, p_ref, n_ref, wg_ref, wp_ref, y_ref):
    h = h_ref[...]
    hn = (_rms(h) * n_ref[...]).astype(BF16)
    gate = jax.nn.sigmoid(jnp.dot(hn, wg_ref[...], preferred_element_type=F32))
    proj = jnp.dot(p_ref[...].astype(BF16), wp_ref[...], preferred_element_type=F32)
    y_ref[...] = h + gate * proj


def _ple(h, p, norm, wg_bf, wp_bf, tm):
    t, d = h.shape
    pd = p.shape[1]
    const = lambda shape: pl.BlockSpec(shape, lambda i: (0, 0))
    return pl.pallas_call(
        _ple_kernel,
        grid=(t // tm,),
        in_specs=[pl.BlockSpec((tm, d), lambda i: (i, 0)), pl.BlockSpec((tm, pd), lambda i: (i, 0)),
                  const((1, d)), const((d, d)), const((pd, d))],
        out_specs=pl.BlockSpec((tm, d), lambda i: (i, 0)),
        out_shape=jax.ShapeDtypeStruct((t, d), F32),
        compiler_params=_cparams(("parallel",)),
        name="ple",
    )(h, p, norm, wg_bf, wp_bf)


def _route(top_idx, n_items):
    rb = MOE_ROW_BLOCK
    flat = top_idx.reshape(-1)
    onehot = (flat[:, None] == jnp.arange(N_EXPERTS, dtype=jnp.int32)[None, :]).astype(jnp.int32)
    csum = jnp.cumsum(onehot, axis=0)
    rank = jnp.sum(onehot * (csum - 1), axis=1)
    counts = csum[-1]
    padded = (counts + rb - 1) // rb * rb
    seg_end = jnp.cumsum(padded)
    seg_start = seg_end - padded
    dest = (jnp.sum(onehot * seg_start[None, :], axis=1) + rank).astype(jnp.int32)
    pad_lo = (seg_start + counts).astype(jnp.int32)
    pad_hi = seg_end.astype(jnp.int32)
    per = (padded + MOE_ITEM_ROWS - 1) // MOE_ITEM_ROWS
    item_end = jnp.cumsum(per)
    item_start = item_end - per
    ids = jnp.arange(n_items, dtype=jnp.int32)
    e = jnp.minimum(jnp.searchsorted(item_end, ids, side="right"), N_EXPERTS - 1).astype(jnp.int32)
    valid = ids < item_end[-1]
    piece = ids - item_start[e]
    row0 = seg_start[e] + piece * MOE_ITEM_ROWS
    rows = jnp.clip(padded[e] - piece * MOE_ITEM_ROWS, 0, MOE_ITEM_ROWS)
    nch = jnp.where(valid, rows // rb, 0).astype(jnp.int32)
    last_e = e[jnp.maximum(item_end[-1] - 1, 0)]
    item_e = jnp.where(valid, e, last_e).astype(jnp.int32)
    item_row0 = jnp.where(valid, row0, 0).astype(jnp.int32)
    return dest, pad_lo, pad_hi, (item_e, item_row0, nch, pad_hi[-1:])


def _pad_rows(a, rows):
    return jnp.concatenate([a, jnp.zeros((rows - a.shape[0],) + a.shape[1:], a.dtype)], axis=0)


def kernel(x_prompt, x_sample, cache_k, cache_v, state_ret, page_table, p_prompt, p_sample,
           norm_mix, w_in, q_norm, k_norm, w_o, norm_ffn, w_router, b_router, w_gu, b_gu,
           w_down, b_down, norm_ple, w_ple_gate, w_ple_proj):
    depth = norm_mix.shape[0]
    assert depth == 1
    batch, seq_len, d = x_prompt.shape
    n_dec, dec_seq, _ = x_sample.shape
    assert dec_seq == 1 and n_dec <= SAMPLE_ROWS
    past_len = page_table.shape[1] * cache_k.shape[2]
    ret_w = RET_HEADS * RET_DK
    moba_w = MOBA_HEADS * MOBA_HEAD_DIM
    off = [0, ret_w, 2 * ret_w, 3 * ret_w, 4 * ret_w, 4 * ret_w + moba_w, 4 * ret_w + 2 * moba_w]

    w_in_bf = w_in[0].astype(BF16)
    w_o_bf = w_o[0].astype(BF16)
    wg_ple_bf = w_ple_gate[0].astype(BF16)
    wp_ple_bf = w_ple_proj[0].astype(BF16)
    wr_pad = jnp.concatenate([w_router[0], jnp.zeros((d, LANES - N_EXPERTS), F32)],
                             axis=1).astype(BF16)
    br_pad = jnp.concatenate([b_router[0], jnp.full((LANES - N_EXPERTS,), -jnp.inf, F32)])[None, :]
    qn, kn = q_norm, k_norm

    t_p = batch * seq_len
    xp = x_prompt.reshape(t_p, d)
    tm = 1024
    tables_p = _rope_tables(jnp.arange(seq_len, dtype=jnp.int32))
    z_p, k_p, v_p = _inproj(xp, norm_mix, w_in_bf, tables_p, qn, kn, seq_len, tm)
    ret_p, state_p = _ret_prompt(z_p, batch, seq_len)
    moba_p = _moba_prompt(z_p, batch, seq_len)

    xs_rows = _pad_rows(x_sample.reshape(n_dec, d), SAMPLE_ROWS)
    tables_s = _rope_tables(jnp.full((SAMPLE_ROWS,), past_len, jnp.int32))
    z_s, k_s, v_s = _inproj(xs_rows, norm_mix, w_in_bf, tables_s, qn, kn, SAMPLE_ROWS, SAMPLE_ROWS)
    zs = z_s[:n_dec]
    ret_s, state_s = _ret_sample(zs[:, off[0]:off[1]], zs[:, off[1]:off[2]], zs[:, off[2]:off[3]],
                                 zs[:, off[3]:off[4]], state_ret[0])
    heads = lambda a: a.reshape(n_dec, MOBA_HEADS, MOBA_HEAD_DIM)
    mk_s, mv_s = k_s[:n_dec], v_s[:n_dec]
    pool_shape = cache_k.shape[1:]
    moba_s = _moba_sample(heads(zs[:, off[4]:off[5]]), heads(mk_s), heads(mv_s),
                          cache_k.reshape(pool_shape), cache_v.reshape(pool_shape), page_table)

    h_p, hn_p, idx_p, gate_p = _outproj(xp, ret_p, moba_p, w_o_bf, norm_ffn, wr_pad, br_pad, 256)
    h_s, hn_s, idx_s, gate_s = _outproj(xs_rows, _pad_rows(ret_s, SAMPLE_ROWS),
                                        _pad_rows(moba_s, SAMPLE_ROWS), w_o_bf, norm_ffn,
                                        wr_pad, br_pad, SAMPLE_ROWS)

    n_tok = t_p + n_dec
    top_idx = jnp.concatenate([idx_p[:, :TOP_K], idx_s[:n_dec, :TOP_K]], axis=0)
    n_assign = n_tok * TOP_K
    n_chunks = n_assign // MOE_ROW_BLOCK + N_EXPERTS
    n_slots = n_chunks * MOE_ROW_BLOCK
    n_items = N_EXPERTS + pl.cdiv(n_assign, MOE_ITEM_ROWS)
    dest, pad_lo, pad_hi, items = _route(top_idx, n_items)
    hn3 = jnp.concatenate([hn_p, hn_s[:n_dec]], axis=0).reshape(n_tok, ROW_PIECES, d // ROW_PIECES)
    xs3 = _dispatch(hn3, dest, pad_lo, pad_hi, n_slots)
    out2 = _moe(xs3, items, w_gu[0], b_gu[0], w_down[0], b_down[0])
    dest_pad = jnp.concatenate([dest, jnp.zeros(((SAMPLE_ROWS - n_dec) * TOP_K,), jnp.int32)])
    h2_p = _combine(dest_pad, h_p, gate_p, out2, 0, COMBINE_TOKENS)
    h2_s = _combine(dest_pad, h_s, gate_s, out2, t_p, SAMPLE_ROWS)

    y_p = _ple(h2_p, p_prompt[0].reshape(t_p, -1), norm_ple, wg_ple_bf, wp_ple_bf, 512)
    y_s = _ple(h2_s, _pad_rows(p_sample[0].reshape(n_dec, -1), SAMPLE_ROWS), norm_ple,
               wg_ple_bf, wp_ple_bf, SAMPLE_ROWS)

    kv = lambda a, n, l: a.reshape(1, n, l, MOBA_HEADS, MOBA_HEAD_DIM)
    return (y_p.reshape(batch, seq_len, d), y_s[:n_dec].reshape(n_dec, 1, d),
            kv(k_p, batch, seq_len), kv(v_p, batch, seq_len),
            state_p[None], kv(mk_s, n_dec, 1), kv(mv_s, n_dec, 1), state_s[None])
```

```python
import functools
import math

import jax
import jax.numpy as jnp
from jax import lax
from jax.experimental import pallas as pl
from jax.experimental.pallas import tpu as pltpu

F32 = jnp.float32
BF16 = jnp.bfloat16

RET_HEADS = 4
RET_DK = 256
RET_DV = 256
RET_ROPE_THETA = 10000.0
MOBA_HEADS = 8
MOBA_HEAD_DIM = 128
MOBA_BLOCK = 256
MOBA_TOPK = 3
ROPE_THETA = 500000.0
ROPE_DIM = MOBA_HEAD_DIM // 4
N_EXPERTS = 32
TOP_K = 4
SWIGLU_LIMIT = 7.0
SWIGLU_ALPHA = 1.702
EPS = 1e-6

LANES = 128
SUBLANES = 8
VMEM_LIMIT_BYTES = 56 * 1024 * 1024

SAMPLE_ROWS = 16
RET_CHUNK = 256
MOE_ROW_BLOCK = 128
MOE_MATMUL_ROWS = 256
MOE_ITEM_ROWS = 1536
MOE_FF_TILE = 512
MOE_VMEM_LIMIT_BYTES = 60 * 1024 * 1024
ROW_PIECES = 16
COMBINE_TOKENS = 128
MOBA_KV_GROUP = 4


def _cparams(semantics, vmem=VMEM_LIMIT_BYTES):
    return pltpu.CompilerParams(dimension_semantics=semantics, vmem_limit_bytes=vmem)


def _nt_dot(a, b, **kw):
    return lax.dot_general(a, b, (((1,), (1,)), ((), ())), preferred_element_type=F32, **kw)


def _rms(x):
    return x * lax.rsqrt(jnp.mean(x * x, axis=-1, keepdims=True) + EPS)


def _rnd(a):
    return a.astype(BF16).astype(F32)


_IN_TN = 512


def _inproj_kernel(x_ref, g_ref, w_ref, cr_ref, sr_ref, cm_ref, sa_ref, sb_ref, qn_ref, kn_ref,
                   z_ref, k_ref, v_ref, xn_ref):
    j = pl.program_id(1)
    n_ret = RET_HEADS * RET_DK // _IN_TN
    n_moba = MOBA_HEADS * MOBA_HEAD_DIM // _IN_TN
    moba0 = 4 * n_ret

    @pl.when(j == 0)
    def _():
        xn_ref[...] = (_rms(x_ref[...]) * g_ref[...]).astype(BF16)

    acc = jnp.dot(xn_ref[...], w_ref[...], preferred_element_type=F32)

    @pl.when(j < 2 * n_ret)
    def _():
        half = RET_DK // 2
        c = cr_ref[...]
        s = sr_ref[...]
        scale = jnp.where(j < n_ret, 1.0, RET_DK ** -0.5).astype(F32)
        for hh in range(_IN_TN // RET_DK):
            x1 = acc[:, hh * RET_DK:hh * RET_DK + half]
            x2 = acc[:, hh * RET_DK + half:(hh + 1) * RET_DK]
            z_ref[:, hh * RET_DK:hh * RET_DK + half] = (x1 * c - x2 * s) * scale
            z_ref[:, hh * RET_DK + half:(hh + 1) * RET_DK] = (x2 * c + x1 * s) * scale

    @pl.when(((j >= 2 * n_ret) & (j < moba0)) | (j >= moba0 + 2 * n_moba))
    def _():
        z_ref[...] = acc

    @pl.when(j >= moba0 + 2 * n_moba)
    def _():
        v_ref[...] = acc

    @pl.when((j >= moba0) & (j < moba0 + 2 * n_moba))
    def _():
        gain = jnp.where(j < moba0 + n_moba, qn_ref[...], kn_ref[...])
        for hh in range(_IN_TN // MOBA_HEAD_DIM):
            t = _rms(acc[:, hh * MOBA_HEAD_DIM:(hh + 1) * MOBA_HEAD_DIM]) * gain
            up = pltpu.roll(t, MOBA_HEAD_DIM - ROPE_DIM // 2, 1)
            dn = pltpu.roll(t, ROPE_DIM // 2, 1)
            z_ref[:, hh * MOBA_HEAD_DIM:(hh + 1) * MOBA_HEAD_DIM] = (
                t * cm_ref[...] + up * sa_ref[...] + dn * sb_ref[...])

    @pl.when((j >= moba0 + n_moba) & (j < moba0 + 2 * n_moba))
    def _():
        k_ref[...] = z_ref[...]


def _rope_tables(pos):
    posf = pos.astype(F32)[:, None]
    half = RET_DK // 2
    inv = RET_ROPE_THETA ** (-2.0 * jnp.arange(half, dtype=F32) / RET_DK)
    ang = posf * inv[None, :]
    cr, sr = jnp.cos(ang), jnp.sin(ang)
    mh = ROPE_DIM // 2
    inv_m = ROPE_THETA ** (-2.0 * jnp.arange(mh, dtype=F32) / ROPE_DIM)
    ang_m = posf * inv_m[None, :]
    cos_m, sin_m = jnp.cos(ang_m), jnp.sin(ang_m)
    n = pos.shape[0]
    rest = MOBA_HEAD_DIM - ROPE_DIM
    cm = jnp.concatenate([cos_m, cos_m, jnp.ones((n, rest), F32)], axis=1)
    sa = jnp.concatenate([-sin_m, jnp.zeros((n, MOBA_HEAD_DIM - mh), F32)], axis=1)
    sb = jnp.concatenate([jnp.zeros((n, mh), F32), sin_m, jnp.zeros((n, rest), F32)], axis=1)
    return cr, sr, cm, sa, sb


def _inproj(x2d, norm, w_bf, tables, qn, kn, seq_len, tm):
    t, d = x2d.shape
    n_out = w_bf.shape[1]
    tpos = seq_len // tm
    moba_w = MOBA_HEADS * MOBA_HEAD_DIM
    n_moba = moba_w // _IN_TN
    k0 = (n_out - 2 * moba_w) // _IN_TN
    cr, sr, cm, sa, sb = tables
    tab = pl.BlockSpec((tm, LANES), lambda i, j: (i % tpos, 0))
    vec = pl.BlockSpec((1, LANES), lambda i, j: (0, 0))
    return pl.pallas_call(
        _inproj_kernel,
        grid=(t // tm, n_out // _IN_TN),
        in_specs=[
            pl.BlockSpec((tm, d), lambda i, j: (i, 0)),
            pl.BlockSpec((1, d), lambda i, j: (0, 0)),
            pl.BlockSpec((d, _IN_TN), lambda i, j: (0, j)),
            tab, tab, tab, tab, tab, vec, vec,
        ],
        out_specs=[pl.BlockSpec((tm, _IN_TN), lambda i, j: (i, j)),
                   pl.BlockSpec((tm, _IN_TN), lambda i, j: (i, jnp.clip(j - k0, 0, n_moba - 1))),
                   pl.BlockSpec((tm, _IN_TN), lambda i, j: (i, jnp.clip(j - k0 - n_moba, 0, n_moba - 1)))],
        out_shape=[jax.ShapeDtypeStruct((t, n_out), F32),
                   jax.ShapeDtypeStruct((t, moba_w), F32), jax.ShapeDtypeStruct((t, moba_w), F32)],
        scratch_shapes=[pltpu.VMEM((tm, d), BF16)],
        compiler_params=_cparams(("arbitrary", "arbitrary")),
        name="inproj",
    )(x2d, norm, w_bf, cr, sr, cm, sa, sb, qn, kn)


def _ret_gate(o, g):
    return _rms(o) * (g * jax.nn.sigmoid(g))


def _ret_prompt_kernel(logg_ref, q_ref, k_ref, v_ref, g_ref, o_ref, st_ref, r_ref):
    c = pl.program_id(1)
    n = q_ref.shape[0]

    @pl.when(c == 0)
    def _():
        r_ref[...] = jnp.zeros_like(r_ref)

    i = lax.broadcasted_iota(jnp.int32, (n, 1), 0).astype(F32)
    jj = lax.broadcasted_iota(jnp.int32, (1, n), 1).astype(F32)
    diff = i - jj
    for h in range(RET_HEADS):
        lg = logg_ref[h]
        inner = jnp.where(diff >= 0, jnp.exp(lg * jnp.maximum(diff, 0.0)), 0.0)
        q_dec = jnp.exp(lg * (i + 1.0))
        k_dec = jnp.exp(lg * (n - 1.0 - i))
        chunk_dec = jnp.exp(lg * jnp.full((1, 1), float(n), F32))
        cols = slice(h * RET_DK, (h + 1) * RET_DK)
        q = q_ref[:, cols]
        k = k_ref[:, cols]
        vb = v_ref[:, cols].astype(BF16)
        r = r_ref[h]
        s = _nt_dot(q.astype(BF16), k.astype(BF16)) * inner
        o = (jnp.dot(s.astype(BF16), vb, preferred_element_type=F32)
             + jnp.dot((q * q_dec).astype(BF16), r.astype(BF16), preferred_element_type=F32))
        kd_t = (k * k_dec).T.astype(BF16)
        r_new = r * chunk_dec + jnp.dot(kd_t, vb, preferred_element_type=F32)
        r_ref[h] = r_new
        o_ref[:, cols] = _ret_gate(o, g_ref[:, cols])

        @pl.when(c == pl.num_programs(1) - 1)
        def _():
            st_ref[0, h] = r_new


def _ret_log_decay():
    return jnp.log1p(-jnp.exp2(-5.0 - jnp.arange(RET_HEADS, dtype=F32)))


def _ret_prompt(z, batch, seq_len):
    nc = seq_len // RET_CHUNK
    w = RET_HEADS * RET_DK
    col = lambda sec: pl.BlockSpec((RET_CHUNK, w), lambda b, c: (b * nc + c, sec))
    return pl.pallas_call(
        _ret_prompt_kernel,
        grid=(batch, nc),
        in_specs=[pl.BlockSpec(memory_space=pltpu.SMEM), col(0), col(1), col(2), col(3)],
        out_specs=[
            pl.BlockSpec((RET_CHUNK, w), lambda b, c: (b * nc + c, 0)),
            pl.BlockSpec((1, RET_HEADS, RET_DK, RET_DV), lambda b, c: (b, 0, 0, 0)),
        ],
        scratch_shapes=[pltpu.VMEM((RET_HEADS, RET_DK, RET_DV), F32)],
        out_shape=[
            jax.ShapeDtypeStruct((batch * seq_len, w), F32),
            jax.ShapeDtypeStruct((batch, RET_HEADS, RET_DK, RET_DV), F32),
        ],
        compiler_params=_cparams(("parallel", "arbitrary")),
        name="ret_prompt",
    )(_ret_log_decay(), z, z, z, z)


def _ret_sample_kernel(logg_ref, q_ref, kc_ref, kr_ref, v_ref, g_ref, s0_ref, o_ref, st_ref):
    h = pl.program_id(1)
    dec = jnp.exp(logg_ref[h] * jnp.ones((1, 1), F32))
    q = q_ref[0, 0]
    v = v_ref[0, 0]
    r0 = s0_ref[0, 0]
    qk = jnp.sum(q * kr_ref[0, 0], axis=-1, keepdims=True)
    q8 = jnp.broadcast_to(q * dec, (SUBLANES, RET_DK)).astype(BF16)
    qr = jnp.dot(q8, r0.astype(BF16), preferred_element_type=F32)[0:1]
    o = qk * v + qr
    st_ref[0, 0] = r0 * dec + kc_ref[0, 0] * v
    o_ref[0, 0] = _ret_gate(o, g_ref[0, 0])


def _ret_sample(q, k, v, g, state):
    n = q.shape[0]
    row = lambda a: a.reshape(n, RET_HEADS, 1, RET_DK)
    rspec = pl.BlockSpec((1, 1, 1, RET_DK), lambda b, h: (b, h, 0, 0))
    mspec = pl.BlockSpec((1, 1, RET_DK, RET_DV), lambda b, h: (b, h, 0, 0))
    o, st = pl.pallas_call(
        _ret_sample_kernel,
        grid=(n, RET_HEADS),
        in_specs=[pl.BlockSpec(memory_space=pltpu.SMEM),
                  rspec, pl.BlockSpec((1, 1, RET_DK, 1), lambda b, h: (b, h, 0, 0)),
                  rspec, rspec, rspec, mspec],
        out_specs=[rspec, mspec],
        out_shape=[jax.ShapeDtypeStruct((n, RET_HEADS, 1, RET_DV), F32),
                   jax.ShapeDtypeStruct(state.shape, F32)],
        compiler_params=_cparams(("parallel", "parallel")),
        name="ret_sample",
    )(_ret_log_decay(), row(q), k.reshape(n, RET_HEADS, RET_DK, 1), row(k), row(v), row(g), state)
    return o.reshape(n, RET_HEADS * RET_DV), st


MOBA_HEAD_GROUP = 4


def _moba_prompt_kernel(q_ref, k_ref, v_ref, o_ref, kb_ref, vt_ref, km_ref, sel_ref, acc_ref):
    qi = pl.program_id(2)
    blk = MOBA_BLOCK
    hd = MOBA_HEAD_DIM
    heads = range(MOBA_HEAD_GROUP)
    nb = k_ref.shape[0] // blk
    scale = hd ** -0.5
    grp = MOBA_KV_GROUP
    span = grp * blk

    @pl.when(qi == 0)
    def _():
        for hh in heads:
            cols = slice(hh * hd, (hh + 1) * hd)
            kb_ref[hh] = k_ref[:, cols].astype(BF16)
            for n in range(nb):
                rows = slice(n * blk, (n + 1) * blk)
                vt_ref[hh, :, rows] = v_ref[rows, cols].T.astype(BF16)
                km_ref[hh, n:n + 1, :] = jnp.mean(k_ref[rows, cols], axis=0, keepdims=True)

    own = pl.ds(pl.multiple_of(qi * blk, blk), blk)
    key = lax.broadcasted_iota(jnp.int32, (blk, blk), 0)
    qry = lax.broadcasted_iota(jnp.int32, (blk, blk), 1)
    kblock = lax.broadcasted_iota(jnp.int32, (nb, blk), 0)
    past = kblock < qi
    qbs, state = [], []
    for hh in heads:
        qb = q_ref[:, hh * hd:(hh + 1) * hd].astype(BF16)
        qbs.append(qb)
        gate = _nt_dot(km_ref[hh].astype(BF16), qb)
        g = jnp.where(past, gate, -jnp.inf)
        cnt = jnp.zeros((nb, blk), F32)
        for m in range(nb):
            gm = g[m:m + 1, :]
            cnt = cnt + jnp.where(gm > g, 1.0, jnp.where((gm == g) & (kblock > m), 1.0, 0.0))
        sel_ref[hh] = jnp.where(past & (cnt < MOBA_TOPK), 1.0, 0.0)
        s = _nt_dot(kb_ref[hh, own, :], qb) * scale
        s = jnp.where(key <= qry, s, -jnp.inf)
        m0 = jnp.max(s, axis=0, keepdims=True)
        p = jnp.exp(s - m0)
        acc_ref[hh] = jnp.dot(vt_ref[hh, :, own], p.astype(BF16), preferred_element_type=F32)
        state.append((m0, jnp.sum(p, axis=0, keepdims=True)))

    def body(j, carry):
        rows = pl.ds(pl.multiple_of(j * span, span), span)
        out = []
        for hh in heads:
            m_i, l_i = carry[hh]
            sn = _nt_dot(kb_ref[hh, rows, :], qbs[hh]) * scale
            allowed = jnp.concatenate(
                [jnp.broadcast_to(sel_ref[hh, pl.ds(j * grp + t, 1), :], (blk, blk))
                 for t in range(grp)], axis=0)
            sn = jnp.where(allowed > 0.0, sn, -jnp.inf)
            m_new = jnp.maximum(m_i, jnp.max(sn, axis=0, keepdims=True))
            alpha = jnp.exp(m_i - m_new)
            pn = jnp.exp(sn - m_new)
            acc_ref[hh] = alpha * acc_ref[hh] + jnp.dot(vt_ref[hh, :, rows], pn.astype(BF16),
                                                        preferred_element_type=F32)
            out.append((m_new, alpha * l_i + jnp.sum(pn, axis=0, keepdims=True)))
        return tuple(out)

    final = lax.fori_loop(0, (qi + grp - 1) // grp, body, tuple(state))
    for hh in heads:
        o_ref[:, hh * hd:(hh + 1) * hd] = (acc_ref[hh] / final[hh][1]).T


def _moba_prompt(z, batch, seq_len):
    nq = seq_len // MOBA_BLOCK
    hd = MOBA_HEAD_DIM
    hg = MOBA_HEAD_GROUP
    w = hg * hd
    first = (2 * RET_HEADS * RET_DK + 2 * RET_HEADS * RET_DV) // w
    koff = first + MOBA_HEADS // hg
    voff = koff + MOBA_HEADS // hg
    assert nq % MOBA_KV_GROUP == 0 and MOBA_HEADS % hg == 0
    full = lambda off: pl.BlockSpec((seq_len, w), lambda b, h, i: (b, off + h))
    return pl.pallas_call(
        _moba_prompt_kernel,
        grid=(batch, MOBA_HEADS // hg, nq),
        in_specs=[pl.BlockSpec((MOBA_BLOCK, w), lambda b, h, i: (b * nq + i, first + h)),
                  full(koff), full(voff)],
        out_specs=pl.BlockSpec((MOBA_BLOCK, w), lambda b, h, i: (b * nq + i, h)),
        out_shape=jax.ShapeDtypeStruct((batch * seq_len, MOBA_HEADS * hd), F32),
        scratch_shapes=[pltpu.VMEM((hg, seq_len, hd), BF16), pltpu.VMEM((hg, hd, seq_len), BF16),
                        pltpu.VMEM((hg, nq, hd), F32), pltpu.VMEM((hg, nq, MOBA_BLOCK), F32),
                        pltpu.VMEM((hg, hd, MOBA_BLOCK), F32)],
        compiler_params=_cparams(("parallel", "parallel", "arbitrary")),
        name="moba_prompt",
    )(z, z, z)


MOBA_SCAN_PAGES = 8


def _moba_scan_kernel(pt_ref, q_ref, *refs):
    k_refs = refs[:MOBA_SCAN_PAGES]
    sel_ref, gate_ref = refs[MOBA_SCAN_PAGES:]
    p = pl.program_id(1)
    ppb = MOBA_BLOCK // k_refs[0].shape[1]
    blocks_per_step = MOBA_SCAN_PAGES // ppb

    @pl.when(p == 0)
    def _():
        gate_ref[...] = jnp.zeros_like(gate_ref)

    q = _rnd(q_ref[0])
    lane = lax.broadcasted_iota(jnp.int32, gate_ref.shape, 1)
    gate = gate_ref[...]
    for jb in range(blocks_per_step):
        k_sum = sum(jnp.sum(k_refs[jb * ppb + j][0], axis=0) for j in range(ppb))
        g = jnp.sum(q * _rnd(k_sum * (1.0 / MOBA_BLOCK)), axis=1, keepdims=True)
        gate = jnp.where(lane == p * blocks_per_step + jb, g, gate)
    gate_ref[...] = gate

    @pl.when(p == pl.num_programs(1) - 1)
    def _():
        g = jnp.where(lane < pl.num_programs(1) * blocks_per_step, gate, -jnp.inf)
        out = jnp.zeros(gate_ref.shape, jnp.int32)
        for t in range(MOBA_TOPK):
            mx = jnp.max(g, axis=1, keepdims=True)
            idx = jnp.min(jnp.where(g == mx, lane, LANES), axis=1, keepdims=True)
            out = jnp.where(lane == t, idx, out)
            g = jnp.where(lane == idx, -jnp.inf, g)
        sel_ref[0] = out


def _moba_attend_kernel(pt_ref, selp_ref, q_ref, kn_ref, vn_ref, *refs):
    nsrc = (len(refs) - 1) // 2
    k_refs = refs[:nsrc]
    v_refs = refs[nsrc:2 * nsrc]
    o_ref = refs[2 * nsrc]
    h = pl.program_id(1)
    scale = MOBA_HEAD_DIM ** -0.5
    mine2 = lax.broadcasted_iota(jnp.int32, (MOBA_HEADS, MOBA_HEAD_DIM), 0) == h
    mine3 = lax.broadcasted_iota(jnp.int32, (1, MOBA_HEADS, 1), 1) == h
    all2 = lambda a, op: op(op(a, axis=1, keepdims=True), axis=0, keepdims=True)
    all3 = lambda a, op: op(op(a, axis=0, keepdims=True), axis=1, keepdims=True)
    q = _rnd(q_ref[0])
    s_new = all2(jnp.where(mine2, q * _rnd(kn_ref[0]), 0.0), jnp.sum) * scale
    ss = [jnp.where(mine3, jnp.sum(_rnd(kr[0]) * q[None], axis=-1, keepdims=True) * scale, -jnp.inf)
          for kr in k_refs]
    mx = s_new.reshape(1, 1, 1)
    for sj in ss:
        mx = jnp.maximum(mx, all3(sj, jnp.max))
    p_new = jnp.exp(s_new.reshape(1, 1, 1) - mx)
    ps = [jnp.exp(sj - mx) for sj in ss]
    den = p_new
    for pj in ps:
        den = den + all3(pj, jnp.sum)
    inv = 1.0 / den
    acc = jnp.where(mine2, _rnd(p_new * inv).reshape(1, 1) * _rnd(vn_ref[0]), 0.0)
    for pj, vr in zip(ps, v_refs):
        acc = acc + jnp.sum(_rnd(pj * inv) * _rnd(vr[0]), axis=0)
    o_ref[0, 0] = jnp.sum(acc, axis=0, keepdims=True)


def _moba_sample(q, k_new, v_new, cache_k, cache_v, page_table):
    n, npages = page_table.shape
    page = cache_k.shape[1]
    hd = MOBA_HEAD_DIM
    ppb = MOBA_BLOCK // page
    pp = MOBA_SCAN_PAGES
    assert npages % pp == 0 and npages // ppb <= LANES
    page_block = (1, page, MOBA_HEADS, hd)
    tok2 = pl.BlockSpec((1, MOBA_HEADS, hd), lambda b, p, pt: (b, 0, 0))
    scan_specs = [pl.BlockSpec(page_block, (lambda b, p, pt, j=j: (pt[b, p * pp + j], 0, 0, 0)))
                  for j in range(pp)]
    sel = pl.pallas_call(
        _moba_scan_kernel,
        grid_spec=pltpu.PrefetchScalarGridSpec(
            num_scalar_prefetch=1,
            grid=(n, npages // pp),
            in_specs=[tok2] + scan_specs,
            out_specs=pl.BlockSpec((1, MOBA_HEADS, LANES), lambda b, p, pt: (b, 0, 0)),
            scratch_shapes=[pltpu.VMEM((MOBA_HEADS, LANES), F32)],
        ),
        out_shape=jax.ShapeDtypeStruct((n, MOBA_HEADS, LANES), jnp.int32),
        compiler_params=_cparams(("parallel", "arbitrary")),
        name="moba_scan",
    )(page_table, q, *([cache_k] * pp))
    selp = (sel[:, :, :MOBA_TOPK, None] * ppb + jnp.arange(ppb, dtype=jnp.int32)).reshape(-1)
    nsrc = MOBA_TOPK * ppb

    def page_spec(j):
        def index(b, h, pt, sp):
            return (pt[b, sp[(b * MOBA_HEADS + h) * nsrc + j]], 0, 0, 0)
        return pl.BlockSpec(page_block, index)

    tok = pl.BlockSpec((1, MOBA_HEADS, hd), lambda b, h, pt, sp: (b, 0, 0))
    pages = [page_spec(j) for j in range(nsrc)]
    out = pl.pallas_call(
        _moba_attend_kernel,
        grid_spec=pltpu.PrefetchScalarGridSpec(
            num_scalar_prefetch=2,
            grid=(n, MOBA_HEADS),
            in_specs=[tok, tok, tok] + pages + pages,
            out_specs=pl.BlockSpec((1, 1, 1, hd), lambda b, h, pt, sp: (b, h, 0, 0)),
        ),
        out_shape=jax.ShapeDtypeStruct((n, MOBA_HEADS, 1, hd), F32),
        compiler_params=_cparams(("parallel", "parallel")),
        name="moba_attend",
    )(page_table, selp, q, k_new, v_new, *([cache_k] * nsrc), *([cache_v] * nsrc))
    return out.reshape(n, MOBA_HEADS * hd)


def _outproj_kernel(x_ref, ret_ref, moba_ref, wo_ref, nf_ref, wr_ref, br_ref,
                    h_ref, hn_ref, idx_ref, gate_ref):
    rw = ret_ref.shape[1]
    h = (x_ref[...]
         + jnp.dot(ret_ref[...].astype(BF16), wo_ref[:rw, :], preferred_element_type=F32)
         + jnp.dot(moba_ref[...].astype(BF16), wo_ref[rw:, :], preferred_element_type=F32))
    h_ref[...] = h
    hn = _rms(h) * nf_ref[...]
    hn_ref[...] = hn
    logits = jnp.dot(hn.astype(BF16), wr_ref[...], preferred_element_type=F32) + br_ref[...]
    lane = lax.broadcasted_iota(jnp.int32, logits.shape, 1)
    vals, idxs = [], []
    for _ in range(TOP_K):
        mx = jnp.max(logits, axis=1, keepdims=True)
        ix = jnp.min(jnp.where(logits == mx, lane, LANES), axis=1, keepdims=True)
        vals.append(mx)
        idxs.append(ix)
        logits = jnp.where(lane == ix, -jnp.inf, logits)
    es = [jnp.exp(v - vals[0]) for v in vals]
    den = es[0] + es[1] + es[2] + es[3]
    idx_out = jnp.zeros(lane.shape, jnp.int32)
    gate_out = jnp.zeros(lane.shape, F32)
    for t in range(TOP_K):
        idx_out = jnp.where(lane == t, idxs[t], idx_out)
        gate_out = jnp.where(lane == t, es[t] / den, gate_out)
    idx_ref[...] = idx_out
    gate_ref[...] = gate_out


def _outproj(x2d, ret, moba, wo_bf, norm_ffn, wr_pad, br_pad, tm):
    t, d = x2d.shape
    rw, mw = ret.shape[1], moba.shape[1]
    const = lambda shape: pl.BlockSpec(shape, lambda i: (0, 0))
    rows = lambda w: pl.BlockSpec((tm, w), lambda i: (i, 0))
    return pl.pallas_call(
        _outproj_kernel,
        grid=(t // tm,),
        in_specs=[rows(d), rows(rw), rows(mw), const((rw + mw, d)), const((1, d)),
                  const((d, LANES)), const((1, LANES))],
        out_specs=[rows(d), rows(d), rows(LANES), rows(LANES)],
        out_shape=[jax.ShapeDtypeStruct((t, d), F32), jax.ShapeDtypeStruct((t, d), F32),
                   jax.ShapeDtypeStruct((t, LANES), jnp.int32),
                   jax.ShapeDtypeStruct((t, LANES), F32)],
        compiler_params=_cparams(("parallel",)),
        name="outproj",
    )(x2d, ret, moba, wo_bf, norm_ffn, wr_pad, br_pad)


def _dispatch_kernel(dest_ref, pad_lo_ref, pad_hi_ref, hn_ref, xs_hbm, zero_ref, sem, zsem, *,
                     ntok):
    i = pl.program_id(0)
    tb = hn_ref.shape[0]
    first = i * tb
    count = jnp.minimum(tb, ntok - first)

    def row_copy(r, k):
        return pltpu.make_async_copy(hn_ref.at[r], xs_hbm.at[dest_ref[(first + r) * TOP_K + k]], sem)

    def issue(r, c):
        for k in range(TOP_K):
            row_copy(r, k).start()
        return c

    lax.fori_loop(0, count, issue, 0)

    @pl.when(i == 0)
    def _():
        zero_ref[...] = jnp.zeros_like(zero_ref)

        def pad_copy(s):
            return pltpu.make_async_copy(zero_ref.at[0], xs_hbm.at[s], zsem)

        def per_expert(e, c):
            def pad_row(s, c2):
                pad_copy(s).start()
                return c2
            lax.fori_loop(pad_lo_ref[e], pad_hi_ref[e], pad_row, 0)
            return c

        lax.fori_loop(0, N_EXPERTS, per_expert, 0)

        rb = zero_ref.shape[0]
        tail0 = pad_hi_ref[N_EXPERTS - 1]
        n_tail = (xs_hbm.shape[0] - tail0) // rb

        def tail_copy(c):
            rows = pl.ds(pl.multiple_of(tail0 + c * rb, rb), rb)
            return pltpu.make_async_copy(zero_ref, xs_hbm.at[rows], zsem)

        def tail_start(c, c2):
            tail_copy(c).start()
            return c2

        lax.fori_loop(0, n_tail, tail_start, 0)

        def per_expert_wait(e, c):
            def pad_wait(s, c2):
                pad_copy(s).wait()
                return c2
            lax.fori_loop(pad_lo_ref[e], pad_hi_ref[e], pad_wait, 0)
            return c

        lax.fori_loop(0, N_EXPERTS, per_expert_wait, 0)

        def tail_wait(c, c2):
            tail_copy(c).wait()
            return c2

        lax.fori_loop(0, n_tail, tail_wait, 0)

    @pl.when(count == tb)
    def _():
        for k in range(TOP_K):
            pltpu.make_async_copy(hn_ref, xs_hbm.at[pl.ds(0, tb)], sem).wait()

    @pl.when(count < tb)
    def _():
        def drain(r, c):
            for k in range(TOP_K):
                row_copy(r, k).wait()
            return c

        lax.fori_loop(0, count, drain, 0)


def _dispatch(hn3, dest_flat, pad_lo, pad_hi, n_slots):
    ntok = hn3.shape[0]
    tb = COMBINE_TOKENS
    return pl.pallas_call(
        functools.partial(_dispatch_kernel, ntok=ntok),
        grid_spec=pltpu.PrefetchScalarGridSpec(
            num_scalar_prefetch=3,
            grid=(pl.cdiv(ntok, tb),),
            in_specs=[pl.BlockSpec((tb,) + hn3.shape[1:], lambda i, d, lo, hi: (i, 0, 0))],
            out_specs=pl.BlockSpec(memory_space=pl.ANY),
            scratch_shapes=[pltpu.VMEM((MOE_ROW_BLOCK,) + hn3.shape[1:], F32),
                            pltpu.SemaphoreType.DMA(()), pltpu.SemaphoreType.DMA(())],
        ),
        out_shape=jax.ShapeDtypeStruct((n_slots,) + hn3.shape[1:], F32),
        compiler_params=_cparams(("arbitrary",)),
        name="moe_dispatch",
    )(dest_flat, pad_lo, pad_hi, hn3)


def _moe_kernel(item_e, item_row0, item_nch, tail_ref, xs_hbm, wg_ref, wu_ref, bg_ref, bu_ref,
                wd_ref, bd_ref, out_hbm, x_ref, acc_ref, stage_ref, wgb_ref, wub_ref, wdb_ref,
                sem_in, sem_out):
    it = pl.program_id(0)
    f = pl.program_id(1)
    nf = pl.num_programs(1)
    rb = MOE_ROW_BLOCK
    big = MOE_MATMUL_ROWS
    pieces = ROW_PIECES
    nch = item_nch[it]
    row0 = item_row0[it]
    n_big = nch // (big // rb)
    has_tail = nch % (big // rb) != 0
    tail_row = pl.multiple_of(n_big * big, rb)

    def hbm_rows(start, size):
        return pl.ds(pl.multiple_of((row0 + start) * pieces, rb * pieces), size * pieces)

    def in_copy(start, size, slot):
        return pltpu.make_async_copy(xs_hbm.at[hbm_rows(start, size)],
                                     stage_ref.at[slot, pl.ds(0, size * pieces)], sem_in.at[slot])

    def out_copy(start, size, slot):
        return pltpu.make_async_copy(stage_ref.at[slot, pl.ds(0, size * pieces)],
                                     out_hbm.at[hbm_rows(start, size)], sem_out.at[slot])

    def stage_to_x(start, size, slot):
        rows = pl.ds(start, size)
        for p in range(pieces):
            piece = stage_ref[slot, pl.ds(p, size, stride=pieces), :]
            x_ref[rows, p * LANES:(p + 1) * LANES] = piece.astype(BF16)
        acc_ref[rows, :] = jnp.broadcast_to(bd_ref[0], (size, acc_ref.shape[1]))

    def acc_to_stage(start, size, slot):
        rows = pl.ds(start, size)
        for p in range(pieces):
            stage_ref[slot, pl.ds(p, size, stride=pieces), :] = acc_ref[rows, p * LANES:(p + 1) * LANES]

    @pl.when(nch > 0)
    def _():
        wgb_ref[...] = wg_ref[0].astype(BF16)
        wub_ref[...] = wu_ref[0].astype(BF16)
        wdb_ref[...] = wd_ref[0].astype(BF16)
        bg = bg_ref[0]
        bu = bu_ref[0]

        def compute(start, size):
            rows = pl.ds(start, size)
            x = x_ref[rows, :]
            g = jnp.dot(x, wgb_ref[...], preferred_element_type=F32) + bg
            u = jnp.dot(x, wub_ref[...], preferred_element_type=F32) + bu
            g = jnp.minimum(g, SWIGLU_LIMIT)
            u = jnp.clip(u, -SWIGLU_LIMIT, SWIGLU_LIMIT)
            act = (u + 1.0) * g * jax.nn.sigmoid(SWIGLU_ALPHA * g)
            acc_ref[rows, :] += jnp.dot(act.astype(BF16), wdb_ref[...], preferred_element_type=F32)

        @pl.when(f == 0)
        def _():
            @pl.when(n_big > 0)
            def _():
                in_copy(0, big, 0).start()

            def step(r, c):
                slot = r % 2
                start = pl.multiple_of(r * big, big)

                @pl.when(r + 1 < n_big)
                def _():
                    in_copy(start + big, big, 1 - slot).start()

                in_copy(start, big, slot).wait()
                stage_to_x(start, big, slot)
                compute(start, big)
                return c

            lax.fori_loop(0, n_big, step, 0)

            @pl.when(has_tail)
            def _():
                cp = in_copy(tail_row, rb, 0)
                cp.start()
                cp.wait()
                stage_to_x(tail_row, rb, 0)
                compute(tail_row, rb)

        @pl.when((f > 0) & (f < nf - 1))
        def _():
            def step(r, c):
                compute(pl.multiple_of(r * big, big), big)
                return c

            lax.fori_loop(0, n_big, step, 0)

            @pl.when(has_tail)
            def _():
                compute(tail_row, rb)

        @pl.when(f == nf - 1)
        def _():
            def step(r, c):
                slot = r % 2
                start = pl.multiple_of(r * big, big)
                compute(start, big)

                @pl.when(r >= 2)
                def _():
                    out_copy(start - 2 * big, big, slot).wait()

                acc_to_stage(start, big, slot)
                out_copy(start, big, slot).start()
                return c

            lax.fori_loop(0, n_big, step, 0)

            @pl.when(n_big >= 2)
            def _():
                out_copy(0, big, n_big % 2).wait()

            @pl.when(n_big >= 1)
            def _():
                out_copy(0, big, (n_big - 1) % 2).wait()

            @pl.when(has_tail)
            def _():
                compute(tail_row, rb)
                acc_to_stage(tail_row, rb, 0)
                cp = out_copy(tail_row, rb, 0)
                cp.start()
                cp.wait()

    @pl.when((it == pl.num_programs(0) - 1) & (f == nf - 1))
    def _():
        tail0 = tail_ref[0]
        n_tail = (out_hbm.shape[0] // pieces - tail0) // rb
        stage_ref[0] = jnp.zeros(stage_ref.shape[1:], F32)

        def tail_copy(c):
            rows = pl.ds(pl.multiple_of((tail0 + c * rb) * pieces, rb * pieces), rb * pieces)
            return pltpu.make_async_copy(stage_ref.at[0, pl.ds(0, rb * pieces)], out_hbm.at[rows],
                                         sem_out.at[0])

        def tail_start(c, c2):
            tail_copy(c).start()
            return c2

        def tail_wait(c, c2):
            tail_copy(c).wait()
            return c2

        lax.fori_loop(0, n_tail, tail_start, 0)
        lax.fori_loop(0, n_tail, tail_wait, 0)


def _moe(xs3, items, w_gu, b_gu, w_down, b_down):
    item_e, item_row0, item_nch, tail0 = items
    n_items = item_e.shape[0]
    n_slots = xs3.shape[0]
    d = w_gu.shape[1]
    ff = w_down.shape[1]
    tf = MOE_FF_TILE
    nf = ff // tf
    assert MOE_MATMUL_ROWS == 2 * MOE_ROW_BLOCK and MOE_ITEM_ROWS % MOE_MATMUL_ROWS == 0 and nf >= 2
    xs2 = xs3.reshape(n_slots * ROW_PIECES, LANES)

    def ftile(it, f, nch):
        return jnp.where(nch[it] > 0, f, nf - 1)

    wg_spec = pl.BlockSpec((1, d, tf), lambda it, f, e, r0, nch, t0: (e[it], 0, ftile(it, f, nch)))
    wu_spec = pl.BlockSpec((1, d, tf), lambda it, f, e, r0, nch, t0: (e[it], 0, nf + ftile(it, f, nch)))
    bg_spec = pl.BlockSpec((1, 1, tf), lambda it, f, e, r0, nch, t0: (e[it], 0, ftile(it, f, nch)))
    bu_spec = pl.BlockSpec((1, 1, tf), lambda it, f, e, r0, nch, t0: (e[it], 0, nf + ftile(it, f, nch)))
    wd_spec = pl.BlockSpec((1, tf, d), lambda it, f, e, r0, nch, t0: (e[it], ftile(it, f, nch), 0))
    bd_spec = pl.BlockSpec((1, 1, d), lambda it, f, e, r0, nch, t0: (e[it], 0, 0))
    b_gu3 = b_gu.reshape(N_EXPERTS, 1, 2 * ff)
    b_down3 = b_down.reshape(N_EXPERTS, 1, d)
    out2 = pl.pallas_call(
        _moe_kernel,
        grid_spec=pltpu.PrefetchScalarGridSpec(
            num_scalar_prefetch=4,
            grid=(n_items, nf),
            in_specs=[pl.BlockSpec(memory_space=pl.ANY), wg_spec, wu_spec, bg_spec, bu_spec,
                      wd_spec, bd_spec],
            out_specs=pl.BlockSpec(memory_space=pl.ANY),
            scratch_shapes=[
                pltpu.VMEM((MOE_ITEM_ROWS, d), BF16),
                pltpu.VMEM((MOE_ITEM_ROWS, d), F32),
                pltpu.VMEM((2, MOE_MATMUL_ROWS * ROW_PIECES, LANES), F32),
                pltpu.VMEM((d, tf), BF16), pltpu.VMEM((d, tf), BF16), pltpu.VMEM((tf, d), BF16),
                pltpu.SemaphoreType.DMA((2,)), pltpu.SemaphoreType.DMA((2,)),
            ],
        ),
        out_shape=jax.ShapeDtypeStruct(xs2.shape, F32),
        compiler_params=_cparams(("arbitrary", "arbitrary"), MOE_VMEM_LIMIT_BYTES),
        name="moe_experts",
    )(item_e, item_row0, item_nch, tail0, xs2, w_gu, w_gu, b_gu3, b_gu3, w_down, b_down3)
    return out2


def _combine_kernel(dest_ref, h_ref, gate_ref, out_hbm, o_ref, buf_ref, sem, *, tok_off):
    i = pl.program_id(0)
    tb = h_ref.shape[0]
    pieces = ROW_PIECES

    def row_copy(step, r, k, slot):
        d = dest_ref[(tok_off + step * tb + r) * TOP_K + k]
        src = out_hbm.at[pl.ds(pl.multiple_of(d * pieces, pieces), pieces)]
        dst = buf_ref.at[slot, pl.ds(pl.multiple_of((k * tb + r) * pieces, pieces), pieces)]
        return pltpu.make_async_copy(src, dst, sem.at[slot])

    def issue(step, slot):
        def body(r, c):
            for k in range(TOP_K):
                row_copy(step, r, k, slot).start()
            return c
        lax.fori_loop(0, tb, body, 0)

    @pl.when(i == 0)
    def _():
        issue(0, 0)

    @pl.when(i + 1 < pl.num_programs(0))
    def _():
        issue(i + 1, (i + 1) % 2)

    slot = i % 2
    pltpu.make_async_copy(out_hbm.at[pl.ds(0, TOP_K * tb * pieces)], buf_ref.at[slot],
                          sem.at[slot]).wait()

    for p in range(pieces):
        f = jnp.zeros((tb, LANES), F32)
        for k in range(TOP_K):
            rows = buf_ref[slot, pl.ds(k * tb * pieces + p, tb, stride=pieces), :]
            f = f + gate_ref[:, k:k + 1] * rows
        o_ref[:, p * LANES:(p + 1) * LANES] = h_ref[:, p * LANES:(p + 1) * LANES] + f


def _combine(dest_flat, h, gates, out2, tok_off, tb):
    t, d = h.shape
    return pl.pallas_call(
        functools.partial(_combine_kernel, tok_off=tok_off),
        grid_spec=pltpu.PrefetchScalarGridSpec(
            num_scalar_prefetch=1,
            grid=(t // tb,),
            in_specs=[pl.BlockSpec((tb, d), lambda i, ds: (i, 0)),
                      pl.BlockSpec((tb, LANES), lambda i, ds: (i, 0)),
                      pl.BlockSpec(memory_space=pl.ANY)],
            out_specs=pl.BlockSpec((tb, d), lambda i, ds: (i, 0)),
            scratch_shapes=[pltpu.VMEM((2, TOP_K * tb * ROW_PIECES, LANES), F32),
                            pltpu.SemaphoreType.DMA((2,))],
        ),
        out_shape=jax.ShapeDtypeStruct((t, d), F32),
        compiler_params=_cparams(("arbitrary",)),
        name="moe_combine",
    )(dest_flat, h, gates, out2)


def _ple_kernel(h_ref, p_ref, n_ref, wg_ref, wp_ref, y_ref):
    h = h_ref[...]
    hn = (_rms(h) * n_ref[...]).astype(BF16)
    gate = jax.nn.sigmoid(jnp.dot(hn, wg_ref[...], preferred_element_type=F32))
    proj = jnp.dot(p_ref[...].astype(BF16), wp_ref[...], preferred_element_type=F32)
    y_ref[...] = h + gate * proj


def _ple(h, p, norm, wg_bf, wp_bf, tm):
    t, d = h.shape
    pd = p.shape[1]
    const = lambda shape: pl.BlockSpec(shape, lambda i: (0, 0))
    return pl.pallas_call(
        _ple_kernel,
        grid=(t // tm,),
        in_specs=[pl.BlockSpec((tm, d), lambda i: (i, 0)), pl.BlockSpec((tm, pd), lambda i: (i, 0)),
                  const((1, d)), const((d, d)), const((pd, d))],
        out_specs=pl.BlockSpec((tm, d), lambda i: (i, 0)),
        out_shape=jax.ShapeDtypeStruct((t, d), F32),
        compiler_params=_cparams(("parallel",)),
        name="ple",
    )(h, p, norm, wg_bf, wp_bf)


def _route(top_idx, n_items):
    rb = MOE_ROW_BLOCK
    flat = top_idx.reshape(-1)
    onehot = (flat[:, None] == jnp.arange(N_EXPERTS, dtype=jnp.int32)[None, :]).astype(jnp.int32)
    csum = jnp.cumsum(onehot, axis=0)
    rank = jnp.sum(onehot * (csum - 1), axis=1)
    counts = csum[-1]
    padded = (counts + rb - 1) // rb * rb
    seg_end = jnp.cumsum(padded)
    seg_start = seg_end - padded
    dest = (jnp.sum(onehot * seg_start[None, :], axis=1) + rank).astype(jnp.int32)
    pad_lo = (seg_start + counts).astype(jnp.int32)
    pad_hi = seg_end.astype(jnp.int32)
    per = (padded + MOE_ITEM_ROWS - 1) // MOE_ITEM_ROWS
    item_end = jnp.cumsum(per)
    item_start = item_end - per
    ids = jnp.arange(n_items, dtype=jnp.int32)
    e = jnp.minimum(jnp.searchsorted(item_end, ids, side="right"), N_EXPERTS - 1).astype(jnp.int32)
    valid = ids < item_end[-1]
    piece = ids - item_start[e]
    row0 = seg_start[e] + piece * MOE_ITEM_ROWS
    rows = jnp.clip(padded[e] - piece * MOE_ITEM_ROWS, 0, MOE_ITEM_ROWS)
    nch = jnp.where(valid, rows // rb, 0).astype(jnp.int32)
    last_e = e[jnp.maximum(item_end[-1] - 1, 0)]
    item_e = jnp.where(valid, e, last_e).astype(jnp.int32)
    item_row0 = jnp.where(valid, row0, 0).astype(jnp.int32)
    return dest, pad_lo, pad_hi, (item_e, item_row0, nch, pad_hi[-1:])


def _pad_rows(a, rows):
    return jnp.concatenate([a, jnp.zeros((rows - a.shape[0],) + a.shape[1:], a.dtype)], axis=0)


def kernel(x_prompt, x_sample, cache_k, cache_v, state_ret, page_table, p_prompt, p_sample,
           norm_mix, w_in, q_norm, k_norm, w_o, norm_ffn, w_router, b_router, w_gu, b_gu,
           w_down, b_down, norm_ple, w_ple_gate, w_ple_proj):
    depth = norm_mix.shape[0]
    assert depth == 1
    batch, seq_len, d = x_prompt.shape
    n_dec, dec_seq, _ = x_sample.shape
    assert dec_seq == 1 and n_dec <= SAMPLE_ROWS
    past_len = page_table.shape[1] * cache_k.shape[2]
    ret_w = RET_HEADS * RET_DK
    moba_w = MOBA_HEADS * MOBA_HEAD_DIM
    off = [0, ret_w, 2 * ret_w, 3 * ret_w, 4 * ret_w, 4 * ret_w + moba_w, 4 * ret_w + 2 * moba_w]

    w_in_bf = w_in[0].astype(BF16)
    w_o_bf = w_o[0].astype(BF16)
    wg_ple_bf = w_ple_gate[0].astype(BF16)
    wp_ple_bf = w_ple_proj[0].astype(BF16)
    wr_pad = jnp.concatenate([w_router[0], jnp.zeros((d, LANES - N_EXPERTS), F32)],
                             axis=1).astype(BF16)
    br_pad = jnp.concatenate([b_router[0], jnp.full((LANES - N_EXPERTS,), -jnp.inf, F32)])[None, :]
    qn, kn = q_norm, k_norm

    t_p = batch * seq_len
    xp = x_prompt.reshape(t_p, d)
    tm = 1024
    tables_p = _rope_tables(jnp.arange(seq_len, dtype=jnp.int32))
    z_p, k_p, v_p = _inproj(xp, norm_mix, w_in_bf, tables_p, qn, kn, seq_len, tm)
    ret_p, state_p = _ret_prompt(z_p, batch, seq_len)
    moba_p = _moba_prompt(z_p, batch, seq_len)

    xs_rows = _pad_rows(x_sample.reshape(n_dec, d), SAMPLE_ROWS)
    tables_s = _rope_tables(jnp.full((SAMPLE_ROWS,), past_len, jnp.int32))
    z_s, k_s, v_s = _inproj(xs_rows, norm_mix, w_in_bf, tables_s, qn, kn, SAMPLE_ROWS, SAMPLE_ROWS)
    zs = z_s[:n_dec]
    ret_s, state_s = _ret_sample(zs[:, off[0]:off[1]], zs[:, off[1]:off[2]], zs[:, off[2]:off[3]],
                                 zs[:, off[3]:off[4]], state_ret[0])
    heads = lambda a: a.reshape(n_dec, MOBA_HEADS, MOBA_HEAD_DIM)
    mk_s, mv_s = k_s[:n_dec], v_s[:n_dec]
    pool_shape = cache_k.shape[1:]
    moba_s = _moba_sample(heads(zs[:, off[4]:off[5]]), heads(mk_s), heads(mv_s),
                          cache_k.reshape(pool_shape), cache_v.reshape(pool_shape), page_table)

    h_p, hn_p, idx_p, gate_p = _outproj(xp, ret_p, moba_p, w_o_bf, norm_ffn, wr_pad, br_pad, 256)
    h_s, hn_s, idx_s, gate_s = _outproj(xs_rows, _pad_rows(ret_s, SAMPLE_ROWS),
                                        _pad_rows(moba_s, SAMPLE_ROWS), w_o_bf, norm_ffn,
                                        wr_pad, br_pad, SAMPLE_ROWS)

    n_tok = t_p + n_dec
    top_idx = jnp.concatenate([idx_p[:, :TOP_K], idx_s[:n_dec, :TOP_K]], axis=0)
    n_assign = n_tok * TOP_K
    n_chunks = n_assign // MOE_ROW_BLOCK + N_EXPERTS
    n_slots = n_chunks * MOE_ROW_BLOCK
    n_items = N_EXPERTS + pl.cdiv(n_assign, MOE_ITEM_ROWS)
    dest, pad_lo, pad_hi, items = _route(top_idx, n_items)
    hn3 = jnp.concatenate([hn_p, hn_s[:n_dec]], axis=0).reshape(n_tok, ROW_PIECES, d // ROW_PIECES)
    xs3 = _dispatch(hn3, dest, pad_lo, pad_hi, n_slots)
    out2 = _moe(xs3, items, w_gu[0], b_gu[0], w_down[0], b_down[0])
    dest_pad = jnp.concatenate([dest, jnp.zeros(((SAMPLE_ROWS - n_dec) * TOP_K,), jnp.int32)])
    h2_p = _combine(dest_pad, h_p, gate_p, out2, 0, COMBINE_TOKENS)
    h2_s = _combine(dest_pad, h_s, gate_s, out2, t_p, SAMPLE_ROWS)

    y_p = _ple(h2_p, p_prompt[0].reshape(t_p, -1), norm_ple, wg_ple_bf, wp_ple_bf, 512)
    y_s = _ple(h2_s, _pad_rows(p_sample[0].reshape(n_dec, -1), SAMPLE_ROWS), norm_ple,
               wg_ple_bf, wp_ple_bf, SAMPLE_ROWS)

    kv = lambda a, n, l: a.reshape(1, n, l, MOBA_HEADS, MOBA_HEAD_DIM)
    return (y_p.reshape(batch, seq_len, d), y_s[:n_dec].reshape(n_dec, 1, d),
            kv(k_p, batch, seq_len), kv(v_p, batch, seq_len),
            state_p[None], kv(mk_s, n_dec, 1), kv(mv_s, n_dec, 1), state_s[None])
```

```python
import functools
import math

import jax
import jax.numpy as jnp
from jax import lax
from jax.experimental import pallas as pl
from jax.experimental.pallas import tpu as pltpu

F32 = jnp.float32
BF16 = jnp.bfloat16

RET_HEADS = 4
RET_DK = 256
RET_DV = 256
RET_ROPE_THETA = 10000.0
MOBA_HEADS = 8
MOBA_HEAD_DIM = 128
MOBA_BLOCK = 256
MOBA_TOPK = 3
ROPE_THETA = 500000.0
ROPE_DIM = MOBA_HEAD_DIM // 4
N_EXPERTS = 32
TOP_K = 4
SWIGLU_LIMIT = 7.0
SWIGLU_ALPHA = 1.702
EPS = 1e-6

LANES = 128
SUBLANES = 8
VMEM_LIMIT_BYTES = 56 * 1024 * 1024

SAMPLE_ROWS = 16
RET_CHUNK = 256
MOE_ROW_BLOCK = 128
MOE_MATMUL_ROWS = 256
MOE_ITEM_ROWS = 1536
MOE_FF_TILE = 512
MOE_VMEM_LIMIT_BYTES = 60 * 1024 * 1024
ROW_PIECES = 16
COMBINE_TOKENS = 256
MOBA_KV_GROUP = 4


def _cparams(semantics, vmem=VMEM_LIMIT_BYTES):
    return pltpu.CompilerParams(dimension_semantics=semantics, vmem_limit_bytes=vmem)


def _nt_dot(a, b, **kw):
    return lax.dot_general(a, b, (((1,), (1,)), ((), ())), preferred_element_type=F32, **kw)


def _rms(x):
    return x * lax.rsqrt(jnp.mean(x * x, axis=-1, keepdims=True) + EPS)


def _rnd(a):
    return a.astype(BF16).astype(F32)


_IN_TN = 512


def _inproj_kernel(x_ref, g_ref, w_ref, cr_ref, sr_ref, cm_ref, sa_ref, sb_ref, qn_ref, kn_ref,
                   z_ref, k_ref, v_ref, xn_ref):
    j = pl.program_id(1)
    n_ret = RET_HEADS * RET_DK // _IN_TN
    n_moba = MOBA_HEADS * MOBA_HEAD_DIM // _IN_TN
    moba0 = 4 * n_ret

    @pl.when(j == 0)
    def _():
        xn_ref[...] = (_rms(x_ref[...]) * g_ref[...]).astype(BF16)

    acc = jnp.dot(xn_ref[...], w_ref[...], preferred_element_type=F32)

    @pl.when(j < 2 * n_ret)
    def _():
        half = RET_DK // 2
        c = cr_ref[...]
        s = sr_ref[...]
        scale = jnp.where(j < n_ret, 1.0, RET_DK ** -0.5).astype(F32)
        for hh in range(_IN_TN // RET_DK):
            x1 = acc[:, hh * RET_DK:hh * RET_DK + half]
            x2 = acc[:, hh * RET_DK + half:(hh + 1) * RET_DK]
            z_ref[:, hh * RET_DK:hh * RET_DK + half] = (x1 * c - x2 * s) * scale
            z_ref[:, hh * RET_DK + half:(hh + 1) * RET_DK] = (x2 * c + x1 * s) * scale

    @pl.when(((j >= 2 * n_ret) & (j < moba0)) | (j >= moba0 + 2 * n_moba))
    def _():
        z_ref[...] = acc

    @pl.when(j >= moba0 + 2 * n_moba)
    def _():
        v_ref[...] = acc

    @pl.when((j >= moba0) & (j < moba0 + 2 * n_moba))
    def _():
        gain = jnp.where(j < moba0 + n_moba, qn_ref[...], kn_ref[...])
        for hh in range(_IN_TN // MOBA_HEAD_DIM):
            t = _rms(acc[:, hh * MOBA_HEAD_DIM:(hh + 1) * MOBA_HEAD_DIM]) * gain
            up = pltpu.roll(t, MOBA_HEAD_DIM - ROPE_DIM // 2, 1)
            dn = pltpu.roll(t, ROPE_DIM // 2, 1)
            z_ref[:, hh * MOBA_HEAD_DIM:(hh + 1) * MOBA_HEAD_DIM] = (
                t * cm_ref[...] + up * sa_ref[...] + dn * sb_ref[...])

    @pl.when((j >= moba0 + n_moba) & (j < moba0 + 2 * n_moba))
    def _():
        k_ref[...] = z_ref[...]


def _rope_tables(pos):
    posf = pos.astype(F32)[:, None]
    half = RET_DK // 2
    inv = RET_ROPE_THETA ** (-2.0 * jnp.arange(half, dtype=F32) / RET_DK)
    ang = posf * inv[None, :]
    cr, sr = jnp.cos(ang), jnp.sin(ang)
    mh = ROPE_DIM // 2
    inv_m = ROPE_THETA ** (-2.0 * jnp.arange(mh, dtype=F32) / ROPE_DIM)
    ang_m = posf * inv_m[None, :]
    cos_m, sin_m = jnp.cos(ang_m), jnp.sin(ang_m)
    n = pos.shape[0]
    rest = MOBA_HEAD_DIM - ROPE_DIM
    cm = jnp.concatenate([cos_m, cos_m, jnp.ones((n, rest), F32)], axis=1)
    sa = jnp.concatenate([-sin_m, jnp.zeros((n, MOBA_HEAD_DIM - mh), F32)], axis=1)
    sb = jnp.concatenate([jnp.zeros((n, mh), F32), sin_m, jnp.zeros((n, rest), F32)], axis=1)
    return cr, sr, cm, sa, sb


def _inproj(x2d, norm, w_bf, tables, qn, kn, seq_len, tm):
    t, d = x2d.shape
    n_out = w_bf.shape[1]
    tpos = seq_len // tm
    moba_w = MOBA_HEADS * MOBA_HEAD_DIM
    n_moba = moba_w // _IN_TN
    k0 = (n_out - 2 * moba_w) // _IN_TN
    cr, sr, cm, sa, sb = tables
    tab = pl.BlockSpec((tm, LANES), lambda i, j: (i % tpos, 0))
    vec = pl.BlockSpec((1, LANES), lambda i, j: (0, 0))
    return pl.pallas_call(
        _inproj_kernel,
        grid=(t // tm, n_out // _IN_TN),
        in_specs=[
            pl.BlockSpec((tm, d), lambda i, j: (i, 0)),
            pl.BlockSpec((1, d), lambda i, j: (0, 0)),
            pl.BlockSpec((d, _IN_TN), lambda i, j: (0, j)),
            tab, tab, tab, tab, tab, vec, vec,
        ],
        out_specs=[pl.BlockSpec((tm, _IN_TN), lambda i, j: (i, j)),
                   pl.BlockSpec((tm, _IN_TN), lambda i, j: (i, jnp.clip(j - k0, 0, n_moba - 1))),
                   pl.BlockSpec((tm, _IN_TN), lambda i, j: (i, jnp.clip(j - k0 - n_moba, 0, n_moba - 1)))],
        out_shape=[jax.ShapeDtypeStruct((t, n_out), F32),
                   jax.ShapeDtypeStruct((t, moba_w), F32), jax.ShapeDtypeStruct((t, moba_w), F32)],
        scratch_shapes=[pltpu.VMEM((tm, d), BF16)],
        compiler_params=_cparams(("arbitrary", "arbitrary")),
        name="inproj",
    )(x2d, norm, w_bf, cr, sr, cm, sa, sb, qn, kn)


def _ret_gate(o, g):
    return _rms(o) * (g * jax.nn.sigmoid(g))


def _ret_prompt_kernel(logg_ref, q_ref, k_ref, v_ref, g_ref, o_ref, st_ref, r_ref):
    c = pl.program_id(1)
    n = q_ref.shape[0]

    @pl.when(c == 0)
    def _():
        r_ref[...] = jnp.zeros_like(r_ref)

    i = lax.broadcasted_iota(jnp.int32, (n, 1), 0).astype(F32)
    jj = lax.broadcasted_iota(jnp.int32, (1, n), 1).astype(F32)
    diff = i - jj
    for h in range(RET_HEADS):
        lg = logg_ref[h]
        inner = jnp.where(diff >= 0, jnp.exp(lg * jnp.maximum(diff, 0.0)), 0.0)
        q_dec = jnp.exp(lg * (i + 1.0))
        k_dec = jnp.exp(lg * (n - 1.0 - i))
        chunk_dec = jnp.exp(lg * jnp.full((1, 1), float(n), F32))
        cols = slice(h * RET_DK, (h + 1) * RET_DK)
        q = q_ref[:, cols]
        k = k_ref[:, cols]
        vb = v_ref[:, cols].astype(BF16)
        r = r_ref[h]
        s = _nt_dot(q.astype(BF16), k.astype(BF16)) * inner
        o = (jnp.dot(s.astype(BF16), vb, preferred_element_type=F32)
             + jnp.dot((q * q_dec).astype(BF16), r.astype(BF16), preferred_element_type=F32))
        kd_t = (k * k_dec).T.astype(BF16)
        r_new = r * chunk_dec + jnp.dot(kd_t, vb, preferred_element_type=F32)
        r_ref[h] = r_new
        o_ref[:, cols] = _ret_gate(o, g_ref[:, cols])

        @pl.when(c == pl.num_programs(1) - 1)
        def _():
            st_ref[0, h] = r_new


def _ret_log_decay():
    return jnp.log1p(-jnp.exp2(-5.0 - jnp.arange(RET_HEADS, dtype=F32)))


def _ret_prompt(z, batch, seq_len):
    nc = seq_len // RET_CHUNK
    w = RET_HEADS * RET_DK
    col = lambda sec: pl.BlockSpec((RET_CHUNK, w), lambda b, c: (b * nc + c, sec))
    return pl.pallas_call(
        _ret_prompt_kernel,
        grid=(batch, nc),
        in_specs=[pl.BlockSpec(memory_space=pltpu.SMEM), col(0), col(1), col(2), col(3)],
        out_specs=[
            pl.BlockSpec((RET_CHUNK, w), lambda b, c: (b * nc + c, 0)),
            pl.BlockSpec((1, RET_HEADS, RET_DK, RET_DV), lambda b, c: (b, 0, 0, 0)),
        ],
        scratch_shapes=[pltpu.VMEM((RET_HEADS, RET_DK, RET_DV), F32)],
        out_shape=[
            jax.ShapeDtypeStruct((batch * seq_len, w), F32),
            jax.ShapeDtypeStruct((batch, RET_HEADS, RET_DK, RET_DV), F32),
        ],
        compiler_params=_cparams(("parallel", "arbitrary")),
        name="ret_prompt",
    )(_ret_log_decay(), z, z, z, z)


def _ret_sample_kernel(logg_ref, q_ref, kc_ref, kr_ref, v_ref, g_ref, s0_ref, o_ref, st_ref):
    h = pl.program_id(1)
    dec = jnp.exp(logg_ref[h] * jnp.ones((1, 1), F32))
    q = q_ref[0, 0]
    v = v_ref[0, 0]
    r0 = s0_ref[0, 0]
    qk = jnp.sum(q * kr_ref[0, 0], axis=-1, keepdims=True)
    q8 = jnp.broadcast_to(q * dec, (SUBLANES, RET_DK)).astype(BF16)
    qr = jnp.dot(q8, r0.astype(BF16), preferred_element_type=F32)[0:1]
    o = qk * v + qr
    st_ref[0, 0] = r0 * dec + kc_ref[0, 0] * v
    o_ref[0, 0] = _ret_gate(o, g_ref[0, 0])


def _ret_sample(q, k, v, g, state):
    n = q.shape[0]
    row = lambda a: a.reshape(n, RET_HEADS, 1, RET_DK)
    rspec = pl.BlockSpec((1, 1, 1, RET_DK), lambda b, h: (b, h, 0, 0))
    mspec = pl.BlockSpec((1, 1, RET_DK, RET_DV), lambda b, h: (b, h, 0, 0))
    o, st = pl.pallas_call(
        _ret_sample_kernel,
        grid=(n, RET_HEADS),
        in_specs=[pl.BlockSpec(memory_space=pltpu.SMEM),
                  rspec, pl.BlockSpec((1, 1, RET_DK, 1), lambda b, h: (b, h, 0, 0)),
                  rspec, rspec, rspec, mspec],
        out_specs=[rspec, mspec],
        out_shape=[jax.ShapeDtypeStruct((n, RET_HEADS, 1, RET_DV), F32),
                   jax.ShapeDtypeStruct(state.shape, F32)],
        compiler_params=_cparams(("parallel", "parallel")),
        name="ret_sample",
    )(_ret_log_decay(), row(q), k.reshape(n, RET_HEADS, RET_DK, 1), row(k), row(v), row(g), state)
    return o.reshape(n, RET_HEADS * RET_DV), st


MOBA_HEAD_GROUP = 4


def _moba_prompt_kernel(q_ref, k_ref, v_ref, o_ref, kb_ref, vt_ref, km_ref, sel_ref, acc_ref):
    qi = pl.program_id(2)
    blk = MOBA_BLOCK
    hd = MOBA_HEAD_DIM
    heads = range(MOBA_HEAD_GROUP)
    nb = k_ref.shape[0] // blk
    scale = hd ** -0.5
    grp = MOBA_KV_GROUP
    span = grp * blk

    @pl.when(qi == 0)
    def _():
        for hh in heads:
            cols = slice(hh * hd, (hh + 1) * hd)
            kb_ref[hh] = k_ref[:, cols].astype(BF16)
            for n in range(nb):
                rows = slice(n * blk, (n + 1) * blk)
                vt_ref[hh, :, rows] = v_ref[rows, cols].T.astype(BF16)
                km_ref[hh, n:n + 1, :] = jnp.mean(k_ref[rows, cols], axis=0, keepdims=True)

    own = pl.ds(pl.multiple_of(qi * blk, blk), blk)
    key = lax.broadcasted_iota(jnp.int32, (blk, blk), 0)
    qry = lax.broadcasted_iota(jnp.int32, (blk, blk), 1)
    kblock = lax.broadcasted_iota(jnp.int32, (nb, blk), 0)
    past = kblock < qi
    qbs, state = [], []
    for hh in heads:
        qb = q_ref[:, hh * hd:(hh + 1) * hd].astype(BF16)
        qbs.append(qb)
        gate = _nt_dot(km_ref[hh].astype(BF16), qb)
        g = jnp.where(past, gate, -jnp.inf)
        cnt = jnp.zeros((nb, blk), F32)
        for m in range(nb):
            gm = g[m:m + 1, :]
            cnt = cnt + jnp.where(gm > g, 1.0, jnp.where((gm == g) & (kblock > m), 1.0, 0.0))
        sel_ref[hh] = jnp.where(past & (cnt < MOBA_TOPK), 1.0, 0.0)
        s = _nt_dot(kb_ref[hh, own, :], qb) * scale
        s = jnp.where(key <= qry, s, -jnp.inf)
        m0 = jnp.max(s, axis=0, keepdims=True)
        p = jnp.exp(s - m0)
        acc_ref[hh] = jnp.dot(vt_ref[hh, :, own], p.astype(BF16), preferred_element_type=F32)
        state.append((m0, jnp.sum(p, axis=0, keepdims=True)))

    def body(j, carry):
        rows = pl.ds(pl.multiple_of(j * span, span), span)
        out = []
        for hh in heads:
            m_i, l_i = carry[hh]
            sn = _nt_dot(kb_ref[hh, rows, :], qbs[hh]) * scale
            allowed = jnp.concatenate(
                [jnp.broadcast_to(sel_ref[hh, pl.ds(j * grp + t, 1), :], (blk, blk))
                 for t in range(grp)], axis=0)
            sn = jnp.where(allowed > 0.0, sn, -jnp.inf)
            m_new = jnp.maximum(m_i, jnp.max(sn, axis=0, keepdims=True))
            alpha = jnp.exp(m_i - m_new)
            pn = jnp.exp(sn - m_new)
            acc_ref[hh] = alpha * acc_ref[hh] + jnp.dot(vt_ref[hh, :, rows], pn.astype(BF16),
                                                        preferred_element_type=F32)
            out.append((m_new, alpha * l_i + jnp.sum(pn, axis=0, keepdims=True)))
        return tuple(out)

    final = lax.fori_loop(0, (qi + grp - 1) // grp, body, tuple(state))
    for hh in heads:
        o_ref[:, hh * hd:(hh + 1) * hd] = (acc_ref[hh] / final[hh][1]).T


def _moba_prompt(z, batch, seq_len):
    nq = seq_len // MOBA_BLOCK
    hd = MOBA_HEAD_DIM
    hg = MOBA_HEAD_GROUP
    w = hg * hd
    first = (2 * RET_HEADS * RET_DK + 2 * RET_HEADS * RET_DV) // w
    koff = first + MOBA_HEADS // hg
    voff = koff + MOBA_HEADS // hg
    assert nq % MOBA_KV_GROUP == 0 and MOBA_HEADS % hg == 0
    full = lambda off: pl.BlockSpec((seq_len, w), lambda b, h, i: (b, off + h))
    return pl.pallas_call(
        _moba_prompt_kernel,
        grid=(batch, MOBA_HEADS // hg, nq),
        in_specs=[pl.BlockSpec((MOBA_BLOCK, w), lambda b, h, i: (b * nq + i, first + h)),
                  full(koff), full(voff)],
        out_specs=pl.BlockSpec((MOBA_BLOCK, w), lambda b, h, i: (b * nq + i, h)),
        out_shape=jax.ShapeDtypeStruct((batch * seq_len, MOBA_HEADS * hd), F32),
        scratch_shapes=[pltpu.VMEM((hg, seq_len, hd), BF16), pltpu.VMEM((hg, hd, seq_len), BF16),
                        pltpu.VMEM((hg, nq, hd), F32), pltpu.VMEM((hg, nq, MOBA_BLOCK), F32),
                        pltpu.VMEM((hg, hd, MOBA_BLOCK), F32)],
        compiler_params=_cparams(("parallel", "parallel", "arbitrary")),
        name="moba_prompt",
    )(z, z, z)


MOBA_SCAN_PAGES = 8


def _moba_scan_kernel(pt_ref, q_ref, *refs):
    k_refs = refs[:MOBA_SCAN_PAGES]
    sel_ref, gate_ref = refs[MOBA_SCAN_PAGES:]
    p = pl.program_id(1)
    ppb = MOBA_BLOCK // k_refs[0].shape[1]
    blocks_per_step = MOBA_SCAN_PAGES // ppb

    @pl.when(p == 0)
    def _():
        gate_ref[...] = jnp.zeros_like(gate_ref)

    q = _rnd(q_ref[0])
    lane = lax.broadcasted_iota(jnp.int32, gate_ref.shape, 1)
    gate = gate_ref[...]
    for jb in range(blocks_per_step):
        k_sum = sum(jnp.sum(k_refs[jb * ppb + j][0], axis=0) for j in range(ppb))
        g = jnp.sum(q * _rnd(k_sum * (1.0 / MOBA_BLOCK)), axis=1, keepdims=True)
        gate = jnp.where(lane == p * blocks_per_step + jb, g, gate)
    gate_ref[...] = gate

    @pl.when(p == pl.num_programs(1) - 1)
    def _():
        g = jnp.where(lane < pl.num_programs(1) * blocks_per_step, gate, -jnp.inf)
        out = jnp.zeros(gate_ref.shape, jnp.int32)
        for t in range(MOBA_TOPK):
            mx = jnp.max(g, axis=1, keepdims=True)
            idx = jnp.min(jnp.where(g == mx, lane, LANES), axis=1, keepdims=True)
            out = jnp.where(lane == t, idx, out)
            g = jnp.where(lane == idx, -jnp.inf, g)
        sel_ref[0] = out


def _moba_attend_kernel(pt_ref, selp_ref, q_ref, kn_ref, vn_ref, *refs):
    nsrc = (len(refs) - 1) // 2
    k_refs = refs[:nsrc]
    v_refs = refs[nsrc:2 * nsrc]
    o_ref = refs[2 * nsrc]
    h = pl.program_id(1)
    scale = MOBA_HEAD_DIM ** -0.5
    mine2 = lax.broadcasted_iota(jnp.int32, (MOBA_HEADS, MOBA_HEAD_DIM), 0) == h
    mine3 = lax.broadcasted_iota(jnp.int32, (1, MOBA_HEADS, 1), 1) == h
    all2 = lambda a, op: op(op(a, axis=1, keepdims=True), axis=0, keepdims=True)
    all3 = lambda a, op: op(op(a, axis=0, keepdims=True), axis=1, keepdims=True)
    q = _rnd(q_ref[0])
    s_new = all2(jnp.where(mine2, q * _rnd(kn_ref[0]), 0.0), jnp.sum) * scale
    ss = [jnp.where(mine3, jnp.sum(_rnd(kr[0]) * q[None], axis=-1, keepdims=True) * scale, -jnp.inf)
          for kr in k_refs]
    mx = s_new.reshape(1, 1, 1)
    for sj in ss:
        mx = jnp.maximum(mx, all3(sj, jnp.max))
    p_new = jnp.exp(s_new.reshape(1, 1, 1) - mx)
    ps = [jnp.exp(sj - mx) for sj in ss]
    den = p_new
    for pj in ps:
        den = den + all3(pj, jnp.sum)
    inv = 1.0 / den
    acc = jnp.where(mine2, _rnd(p_new * inv).reshape(1, 1) * _rnd(vn_ref[0]), 0.0)
    for pj, vr in zip(ps, v_refs):
        acc = acc + jnp.sum(_rnd(pj * inv) * _rnd(vr[0]), axis=0)
    o_ref[0, 0] = jnp.sum(acc, axis=0, keepdims=True)


def _moba_sample(q, k_new, v_new, cache_k, cache_v, page_table):
    n, npages = page_table.shape
    page = cache_k.shape[1]
    hd = MOBA_HEAD_DIM
    ppb = MOBA_BLOCK // page
    pp = MOBA_SCAN_PAGES
    assert npages % pp == 0 and npages // ppb <= LANES
    page_block = (1, page, MOBA_HEADS, hd)
    tok2 = pl.BlockSpec((1, MOBA_HEADS, hd), lambda b, p, pt: (b, 0, 0))
    scan_specs = [pl.BlockSpec(page_block, (lambda b, p, pt, j=j: (pt[b, p * pp + j], 0, 0, 0)))
                  for j in range(pp)]
    sel = pl.pallas_call(
        _moba_scan_kernel,
        grid_spec=pltpu.PrefetchScalarGridSpec(
            num_scalar_prefetch=1,
            grid=(n, npages // pp),
            in_specs=[tok2] + scan_specs,
            out_specs=pl.BlockSpec((1, MOBA_HEADS, LANES), lambda b, p, pt: (b, 0, 0)),
            scratch_shapes=[pltpu.VMEM((MOBA_HEADS, LANES), F32)],
        ),
        out_shape=jax.ShapeDtypeStruct((n, MOBA_HEADS, LANES), jnp.int32),
        compiler_params=_cparams(("parallel", "arbitrary")),
        name="moba_scan",
    )(page_table, q, *([cache_k] * pp))
    selp = (sel[:, :, :MOBA_TOPK, None] * ppb + jnp.arange(ppb, dtype=jnp.int32)).reshape(-1)
    nsrc = MOBA_TOPK * ppb

    def page_spec(j):
        def index(b, h, pt, sp):
            return (pt[b, sp[(b * MOBA_HEADS + h) * nsrc + j]], 0, 0, 0)
        return pl.BlockSpec(page_block, index)

    tok = pl.BlockSpec((1, MOBA_HEADS, hd), lambda b, h, pt, sp: (b, 0, 0))
    pages = [page_spec(j) for j in range(nsrc)]
    out = pl.pallas_call(
        _moba_attend_kernel,
        grid_spec=pltpu.PrefetchScalarGridSpec(
            num_scalar_prefetch=2,
            grid=(n, MOBA_HEADS),
            in_specs=[tok, tok, tok] + pages + pages,
            out_specs=pl.BlockSpec((1, 1, 1, hd), lambda b, h, pt, sp: (b, h, 0, 0)),
        ),
        out_shape=jax.ShapeDtypeStruct((n, MOBA_HEADS, 1, hd), F32),
        compiler_params=_cparams(("parallel", "parallel")),
        name="moba_attend",
    )(page_table, selp, q, k_new, v_new, *([cache_k] * nsrc), *([cache_v] * nsrc))
    return out.reshape(n, MOBA_HEADS * hd)


def _outproj_kernel(x_ref, ret_ref, moba_ref, wo_ref, nf_ref, wr_ref, br_ref,
                    h_ref, hn_ref, idx_ref, gate_ref):
    rw = ret_ref.shape[1]
    h = (x_ref[...]
         + jnp.dot(ret_ref[...].astype(BF16), wo_ref[:rw, :], preferred_element_type=F32)
         + jnp.dot(moba_ref[...].astype(BF16), wo_ref[rw:, :], preferred_element_type=F32))
    h_ref[...] = h
    hn = _rms(h) * nf_ref[...]
    for p in range(ROW_PIECES):
        hn_ref[pl.ds(p, hn.shape[0], stride=ROW_PIECES), :] = hn[:, p * LANES:(p + 1) * LANES]
    logits = jnp.dot(hn.astype(BF16), wr_ref[...], preferred_element_type=F32) + br_ref[...]
    lane = lax.broadcasted_iota(jnp.int32, logits.shape, 1)
    vals, idxs = [], []
    for _ in range(TOP_K):
        mx = jnp.max(logits, axis=1, keepdims=True)
        ix = jnp.min(jnp.where(logits == mx, lane, LANES), axis=1, keepdims=True)
        vals.append(mx)
        idxs.append(ix)
        logits = jnp.where(lane == ix, -jnp.inf, logits)
    es = [jnp.exp(v - vals[0]) for v in vals]
    den = es[0] + es[1] + es[2] + es[3]
    idx_out = jnp.zeros(lane.shape, jnp.int32)
    gate_out = jnp.zeros(lane.shape, F32)
    for t in range(TOP_K):
        idx_out = jnp.where(lane == t, idxs[t], idx_out)
        gate_out = jnp.where(lane == t, es[t] / den, gate_out)
    idx_ref[...] = idx_out
    gate_ref[...] = gate_out


def _outproj(x2d, ret, moba, wo_bf, norm_ffn, wr_pad, br_pad, tm):
    t, d = x2d.shape
    rw, mw = ret.shape[1], moba.shape[1]
    const = lambda shape: pl.BlockSpec(shape, lambda i: (0, 0))
    rows = lambda w: pl.BlockSpec((tm, w), lambda i: (i, 0))
    return pl.pallas_call(
        _outproj_kernel,
        grid=(t // tm,),
        in_specs=[rows(d), rows(rw), rows(mw), const((rw + mw, d)), const((1, d)),
                  const((d, LANES)), const((1, LANES))],
        out_specs=[rows(d), pl.BlockSpec((tm * ROW_PIECES, LANES), lambda i: (i, 0)), rows(LANES),
                   rows(LANES)],
        out_shape=[jax.ShapeDtypeStruct((t, d), F32),
                   jax.ShapeDtypeStruct((t * ROW_PIECES, d // ROW_PIECES), F32),
                   jax.ShapeDtypeStruct((t, LANES), jnp.int32),
                   jax.ShapeDtypeStruct((t, LANES), F32)],
        compiler_params=_cparams(("parallel",)),
        name="outproj",
    )(x2d, ret, moba, wo_bf, norm_ffn, wr_pad, br_pad)


def _dispatch_kernel(dest_ref, pad_lo_ref, pad_hi_ref, hp_ref, hs_ref, xs_hbm, zero_ref, sem, zsem,
                     *, n_sample):
    i = pl.program_id(0)
    last = pl.num_programs(0) - 1
    tb = hp_ref.shape[0]

    def row_copy(src_ref, first, r, k):
        return pltpu.make_async_copy(src_ref.at[r], xs_hbm.at[dest_ref[(first + r) * TOP_K + k]], sem)

    @pl.when(i < last)
    def _():
        def issue(r, c):
            for k in range(TOP_K):
                row_copy(hp_ref, i * tb, r, k).start()
            return c

        lax.fori_loop(0, tb, issue, 0)

    @pl.when(i == last)
    def _():
        for r in range(n_sample):
            for k in range(TOP_K):
                row_copy(hs_ref, last * tb, r, k).start()

    @pl.when(i == 0)
    def _():
        zero_ref[...] = jnp.zeros_like(zero_ref)

        def pad_copy(s):
            return pltpu.make_async_copy(zero_ref.at[0], xs_hbm.at[s], zsem)

        def per_expert(e, c):
            def pad_row(s, c2):
                pad_copy(s).start()
                return c2
            lax.fori_loop(pad_lo_ref[e], pad_hi_ref[e], pad_row, 0)
            return c

        lax.fori_loop(0, N_EXPERTS, per_expert, 0)

        rb = zero_ref.shape[0]
        tail0 = pad_hi_ref[N_EXPERTS - 1]
        n_tail = (xs_hbm.shape[0] - tail0) // rb

        def tail_copy(c):
            rows = pl.ds(pl.multiple_of(tail0 + c * rb, rb), rb)
            return pltpu.make_async_copy(zero_ref, xs_hbm.at[rows], zsem)

        def tail_start(c, c2):
            tail_copy(c).start()
            return c2

        lax.fori_loop(0, n_tail, tail_start, 0)

        def per_expert_wait(e, c):
            def pad_wait(s, c2):
                pad_copy(s).wait()
                return c2
            lax.fori_loop(pad_lo_ref[e], pad_hi_ref[e], pad_wait, 0)
            return c

        lax.fori_loop(0, N_EXPERTS, per_expert_wait, 0)

        def tail_wait(c, c2):
            tail_copy(c).wait()
            return c2

        lax.fori_loop(0, n_tail, tail_wait, 0)

    @pl.when(i < last)
    def _():
        for k in range(TOP_K):
            pltpu.make_async_copy(hp_ref, xs_hbm.at[pl.ds(0, tb)], sem).wait()

    @pl.when(i == last)
    def _():
        for r in range(n_sample):
            for k in range(TOP_K):
                row_copy(hs_ref, last * tb, r, k).wait()


def _dispatch(hn_prompt, hn_sample, n_sample, dest_flat, pad_lo, pad_hi, n_slots):
    tb = COMBINE_TOKENS
    n_prompt = hn_prompt.shape[0]
    assert n_prompt % tb == 0
    row = hn_prompt.shape[1:]
    return pl.pallas_call(
        functools.partial(_dispatch_kernel, n_sample=n_sample),
        grid_spec=pltpu.PrefetchScalarGridSpec(
            num_scalar_prefetch=3,
            grid=(n_prompt // tb + 1,),
            in_specs=[pl.BlockSpec((tb,) + row,
                                   lambda i, d, lo, hi: (jnp.minimum(i, n_prompt // tb - 1), 0, 0)),
                      pl.BlockSpec(hn_sample.shape, lambda i, d, lo, hi: (0, 0, 0))],
            out_specs=pl.BlockSpec(memory_space=pl.ANY),
            scratch_shapes=[pltpu.VMEM((MOE_ROW_BLOCK,) + row, F32),
                            pltpu.SemaphoreType.DMA(()), pltpu.SemaphoreType.DMA(())],
        ),
        out_shape=jax.ShapeDtypeStruct((n_slots,) + row, F32),
        compiler_params=_cparams(("arbitrary",)),
        name="moe_dispatch",
    )(dest_flat, pad_lo, pad_hi, hn_prompt, hn_sample)


def _moe_kernel(item_e, item_row0, item_nch, tail_ref, xs_hbm, wg_ref, wu_ref, bg_ref, bu_ref,
                wd_ref, bd_ref, out_hbm, x_ref, acc_ref, stage_ref, wgb_ref, wub_ref, wdb_ref,
                sem_in, sem_out):
    it = pl.program_id(0)
    f = pl.program_id(1)
    nf = pl.num_programs(1)
    rb = MOE_ROW_BLOCK
    big = MOE_MATMUL_ROWS
    pieces = ROW_PIECES
    nch = item_nch[it]
    row0 = item_row0[it]
    n_big = nch // (big // rb)
    has_tail = nch % (big // rb) != 0
    tail_row = pl.multiple_of(n_big * big, rb)

    def hbm_rows(start, size):
        return pl.ds(pl.multiple_of((row0 + start) * pieces, rb * pieces), size * pieces)

    def in_copy(start, size, slot):
        return pltpu.make_async_copy(xs_hbm.at[hbm_rows(start, size)],
                                     stage_ref.at[slot, pl.ds(0, size * pieces)], sem_in.at[slot])

    def out_copy(start, size, slot):
        return pltpu.make_async_copy(stage_ref.at[slot, pl.ds(0, size * pieces)],
                                     out_hbm.at[hbm_rows(start, size)], sem_out.at[slot])

    def stage_to_x(start, size, slot):
        rows = pl.ds(start, size)
        for p in range(pieces):
            piece = stage_ref[slot, pl.ds(p, size, stride=pieces), :]
            x_ref[rows, p * LANES:(p + 1) * LANES] = piece.astype(BF16)
        acc_ref[rows, :] = jnp.broadcast_to(bd_ref[0], (size, acc_ref.shape[1]))

    def acc_to_stage(start, size, slot):
        rows = pl.ds(start, size)
        for p in range(pieces):
            stage_ref[slot, pl.ds(p, size, stride=pieces), :] = acc_ref[rows, p * LANES:(p + 1) * LANES]

    @pl.when(nch > 0)
    def _():
        wgb_ref[...] = wg_ref[0].astype(BF16)
        wub_ref[...] = wu_ref[0].astype(BF16)
        wdb_ref[...] = wd_ref[0].astype(BF16)
        bg = bg_ref[0]
        bu = bu_ref[0]

        def compute(start, size):
            rows = pl.ds(start, size)
            x = x_ref[rows, :]
            g = jnp.dot(x, wgb_ref[...], preferred_element_type=F32) + bg
            u = jnp.dot(x, wub_ref[...], preferred_element_type=F32) + bu
            g = jnp.minimum(g, SWIGLU_LIMIT)
            u = jnp.clip(u, -SWIGLU_LIMIT, SWIGLU_LIMIT)
            act = (u + 1.0) * g * jax.nn.sigmoid(SWIGLU_ALPHA * g)
            acc_ref[rows, :] += jnp.dot(act.astype(BF16), wdb_ref[...], preferred_element_type=F32)

        @pl.when(f == 0)
        def _():
            @pl.when(n_big > 0)
            def _():
                in_copy(0, big, 0).start()

            def step(r, c):
                slot = r % 2
                start = pl.multiple_of(r * big, big)

                @pl.when(r + 1 < n_big)
                def _():
                    in_copy(start + big, big, 1 - slot).start()

                in_copy(start, big, slot).wait()
                stage_to_x(start, big, slot)
                compute(start, big)
                return c

            lax.fori_loop(0, n_big, step, 0)

            @pl.when(has_tail)
            def _():
                cp = in_copy(tail_row, rb, 0)
                cp.start()
                cp.wait()
                stage_to_x(tail_row, rb, 0)
                compute(tail_row, rb)

        @pl.when((f > 0) & (f < nf - 1))
        def _():
            def step(r, c):
                compute(pl.multiple_of(r * big, big), big)
                return c

            lax.fori_loop(0, n_big, step, 0)

            @pl.when(has_tail)
            def _():
                compute(tail_row, rb)

        @pl.when(f == nf - 1)
        def _():
            def step(r, c):
                slot = r % 2
                start = pl.multiple_of(r * big, big)
                compute(start, big)

                @pl.when(r >= 2)
                def _():
                    out_copy(start - 2 * big, big, slot).wait()

                acc_to_stage(start, big, slot)
                out_copy(start, big, slot).start()
                return c

            lax.fori_loop(0, n_big, step, 0)

            @pl.when(n_big >= 2)
            def _():
                out_copy(0, big, n_big % 2).wait()

            @pl.when(n_big >= 1)
            def _():
                out_copy(0, big, (n_big - 1) % 2).wait()

            @pl.when(has_tail)
            def _():
                compute(tail_row, rb)
                acc_to_stage(tail_row, rb, 0)
                cp = out_copy(tail_row, rb, 0)
                cp.start()
                cp.wait()

    @pl.when((it == pl.num_programs(0) - 1) & (f == nf - 1))
    def _():
        tail0 = tail_ref[0]
        n_tail = (out_hbm.shape[0] // pieces - tail0) // rb
        stage_ref[0] = jnp.zeros(stage_ref.shape[1:], F32)

        def tail_copy(c):
            rows = pl.ds(pl.multiple_of((tail0 + c * rb) * pieces, rb * pieces), rb * pieces)
            return pltpu.make_async_copy(stage_ref.at[0, pl.ds(0, rb * pieces)], out_hbm.at[rows],
                                         sem_out.at[0])

        def tail_start(c, c2):
            tail_copy(c).start()
            return c2

        def tail_wait(c, c2):
            tail_copy(c).wait()
            return c2

        lax.fori_loop(0, n_tail, tail_start, 0)
        lax.fori_loop(0, n_tail, tail_wait, 0)


def _moe(xs3, items, w_gu, b_gu, w_down, b_down):
    item_e, item_row0, item_nch, tail0 = items
    n_items = item_e.shape[0]
    n_slots = xs3.shape[0]
    d = w_gu.shape[1]
    ff = w_down.shape[1]
    tf = MOE_FF_TILE
    nf = ff // tf
    assert MOE_MATMUL_ROWS == 2 * MOE_ROW_BLOCK and MOE_ITEM_ROWS % MOE_MATMUL_ROWS == 0 and nf >= 2
    xs2 = xs3.reshape(n_slots * ROW_PIECES, LANES)

    def ftile(it, f, nch):
        return jnp.where(nch[it] > 0, f, nf - 1)

    wg_spec = pl.BlockSpec((1, d, tf), lambda it, f, e, r0, nch, t0: (e[it], 0, ftile(it, f, nch)))
    wu_spec = pl.BlockSpec((1, d, tf), lambda it, f, e, r0, nch, t0: (e[it], 0, nf + ftile(it, f, nch)))
    bg_spec = pl.BlockSpec((1, 1, tf), lambda it, f, e, r0, nch, t0: (e[it], 0, ftile(it, f, nch)))
    bu_spec = pl.BlockSpec((1, 1, tf), lambda it, f, e, r0, nch, t0: (e[it], 0, nf + ftile(it, f, nch)))
    wd_spec = pl.BlockSpec((1, tf, d), lambda it, f, e, r0, nch, t0: (e[it], ftile(it, f, nch), 0))
    bd_spec = pl.BlockSpec((1, 1, d), lambda it, f, e, r0, nch, t0: (e[it], 0, 0))
    b_gu3 = b_gu.reshape(N_EXPERTS, 1, 2 * ff)
    b_down3 = b_down.reshape(N_EXPERTS, 1, d)
    out2 = pl.pallas_call(
        _moe_kernel,
        grid_spec=pltpu.PrefetchScalarGridSpec(
            num_scalar_prefetch=4,
            grid=(n_items, nf),
            in_specs=[pl.BlockSpec(memory_space=pl.ANY), wg_spec, wu_spec, bg_spec, bu_spec,
                      wd_spec, bd_spec],
            out_specs=pl.BlockSpec(memory_space=pl.ANY),
            scratch_shapes=[
                pltpu.VMEM((MOE_ITEM_ROWS, d), BF16),
                pltpu.VMEM((MOE_ITEM_ROWS, d), F32),
                pltpu.VMEM((2, MOE_MATMUL_ROWS * ROW_PIECES, LANES), F32),
                pltpu.VMEM((d, tf), BF16), pltpu.VMEM((d, tf), BF16), pltpu.VMEM((tf, d), BF16),
                pltpu.SemaphoreType.DMA((2,)), pltpu.SemaphoreType.DMA((2,)),
            ],
        ),
        out_shape=jax.ShapeDtypeStruct(xs2.shape, F32),
        compiler_params=_cparams(("arbitrary", "arbitrary"), MOE_VMEM_LIMIT_BYTES),
        name="moe_experts",
    )(item_e, item_row0, item_nch, tail0, xs2, w_gu, w_gu, b_gu3, b_gu3, w_down, b_down3)
    return out2


def _combine_kernel(dest_ref, h_ref, gate_ref, out_hbm, o_ref, buf_ref, sem, *, tok_off):
    i = pl.program_id(0)
    tb = h_ref.shape[0]
    pieces = ROW_PIECES

    def row_copy(step, r, k, slot):
        d = dest_ref[(tok_off + step * tb + r) * TOP_K + k]
        src = out_hbm.at[pl.ds(pl.multiple_of(d * pieces, pieces), pieces)]
        dst = buf_ref.at[slot, pl.ds(pl.multiple_of((k * tb + r) * pieces, pieces), pieces)]
        return pltpu.make_async_copy(src, dst, sem.at[slot])

    def issue(step, slot):
        def body(r, c):
            for k in range(TOP_K):
                row_copy(step, r, k, slot).start()
            return c
        lax.fori_loop(0, tb, body, 0)

    @pl.when(i == 0)
    def _():
        issue(0, 0)

    @pl.when(i + 1 < pl.num_programs(0))
    def _():
        issue(i + 1, (i + 1) % 2)

    slot = i % 2
    pltpu.make_async_copy(out_hbm.at[pl.ds(0, TOP_K * tb * pieces)], buf_ref.at[slot],
                          sem.at[slot]).wait()

    for p in range(pieces):
        f = jnp.zeros((tb, LANES), F32)
        for k in range(TOP_K):
            rows = buf_ref[slot, pl.ds(k * tb * pieces + p, tb, stride=pieces), :]
            f = f + gate_ref[:, k:k + 1] * rows
        o_ref[:, p * LANES:(p + 1) * LANES] = h_ref[:, p * LANES:(p + 1) * LANES] + f


def _combine(dest_flat, h, gates, out2, tok_off, tb):
    t, d = h.shape
    return pl.pallas_call(
        functools.partial(_combine_kernel, tok_off=tok_off),
        grid_spec=pltpu.PrefetchScalarGridSpec(
            num_scalar_prefetch=1,
            grid=(t // tb,),
            in_specs=[pl.BlockSpec((tb, d), lambda i, ds: (i, 0)),
                      pl.BlockSpec((tb, LANES), lambda i, ds: (i, 0)),
                      pl.BlockSpec(memory_space=pl.ANY)],
            out_specs=pl.BlockSpec((tb, d), lambda i, ds: (i, 0)),
            scratch_shapes=[pltpu.VMEM((2, TOP_K * tb * ROW_PIECES, LANES), F32),
                            pltpu.SemaphoreType.DMA((2,))],
        ),
        out_shape=jax.ShapeDtypeStruct((t, d), F32),
        compiler_params=_cparams(("arbitrary",)),
        name="moe_combine",
    )(dest_flat, h, gates, out2)


def _ple_kernel(h_ref, p_ref, n_ref, wg_ref, wp_ref, y_ref):
    h = h_ref[...]
    hn = (_rms(h) * n_ref[...]).astype(BF16)
    gate = jax.nn.sigmoid(jnp.dot(hn, wg_ref[...], preferred_element_type=F32))
    proj = jnp.dot(p_ref[...].astype(BF16), wp_ref[...], preferred_element_type=F32)
    y_ref[...] = h + gate * proj


def _ple(h, p, norm, wg_bf, wp_bf, tm):
    t, d = h.shape
    pd = p.shape[1]
    const = lambda shape: pl.BlockSpec(shape, lambda i: (0, 0))
    return pl.pallas_call(
        _ple_kernel,
        grid=(t // tm,),
        in_specs=[pl.BlockSpec((tm, d), lambda i: (i, 0)), pl.BlockSpec((tm, pd), lambda i: (i, 0)),
                  const((1, d)), const((d, d)), const((pd, d))],
        out_specs=pl.BlockSpec((tm, d), lambda i: (i, 0)),
        out_shape=jax.ShapeDtypeStruct((t, d), F32),
        compiler_params=_cparams(("parallel",)),
        name="ple",
    )(h, p, norm, wg_bf, wp_bf)


def _route(top_idx, n_items):
    rb = MOE_ROW_BLOCK
    flat = top_idx.reshape(-1)
    onehot = (flat[:, None] == jnp.arange(N_EXPERTS, dtype=jnp.int32)[None, :]).astype(jnp.int32)
    csum = jnp.cumsum(onehot, axis=0)
    rank = jnp.sum(onehot * (csum - 1), axis=1)
    counts = csum[-1]
    padded = (counts + rb - 1) // rb * rb
    seg_end = jnp.cumsum(padded)
    seg_start = seg_end - padded
    dest = (jnp.sum(onehot * seg_start[None, :], axis=1) + rank).astype(jnp.int32)
    pad_lo = (seg_start + counts).astype(jnp.int32)
    pad_hi = seg_end.astype(jnp.int32)
    per = (padded + MOE_ITEM_ROWS - 1) // MOE_ITEM_ROWS
    item_end = jnp.cumsum(per)
    item_start = item_end - per
    ids = jnp.arange(n_items, dtype=jnp.int32)
    e = jnp.minimum(jnp.searchsorted(item_end, ids, side="right"), N_EXPERTS - 1).astype(jnp.int32)
    valid = ids < item_end[-1]
    piece = ids - item_start[e]
    row0 = seg_start[e] + piece * MOE_ITEM_ROWS
    rows = jnp.clip(padded[e] - piece * MOE_ITEM_ROWS, 0, MOE_ITEM_ROWS)
    nch = jnp.where(valid, rows // rb, 0).astype(jnp.int32)
    last_e = e[jnp.maximum(item_end[-1] - 1, 0)]
    item_e = jnp.where(valid, e, last_e).astype(jnp.int32)
    item_row0 = jnp.where(valid, row0, 0).astype(jnp.int32)
    return dest, pad_lo, pad_hi, (item_e, item_row0, nch, pad_hi[-1:])


def _pad_rows(a, rows):
    return jnp.concatenate([a, jnp.zeros((rows - a.shape[0],) + a.shape[1:], a.dtype)], axis=0)


def kernel(x_prompt, x_sample, cache_k, cache_v, state_ret, page_table, p_prompt, p_sample,
           norm_mix, w_in, q_norm, k_norm, w_o, norm_ffn, w_router, b_router, w_gu, b_gu,
           w_down, b_down, norm_ple, w_ple_gate, w_ple_proj):
    depth = norm_mix.shape[0]
    assert depth == 1
    batch, seq_len, d = x_prompt.shape
    n_dec, dec_seq, _ = x_sample.shape
    assert dec_seq == 1 and n_dec <= SAMPLE_ROWS
    past_len = page_table.shape[1] * cache_k.shape[2]
    ret_w = RET_HEADS * RET_DK
    moba_w = MOBA_HEADS * MOBA_HEAD_DIM
    off = [0, ret_w, 2 * ret_w, 3 * ret_w, 4 * ret_w, 4 * ret_w + moba_w, 4 * ret_w + 2 * moba_w]

    w_in_bf = w_in[0].astype(BF16)
    w_o_bf = w_o[0].astype(BF16)
    wg_ple_bf = w_ple_gate[0].astype(BF16)
    wp_ple_bf = w_ple_proj[0].astype(BF16)
    wr_pad = jnp.concatenate([w_router[0], jnp.zeros((d, LANES - N_EXPERTS), F32)],
                             axis=1).astype(BF16)
    br_pad = jnp.concatenate([b_router[0], jnp.full((LANES - N_EXPERTS,), -jnp.inf, F32)])[None, :]
    qn, kn = q_norm, k_norm

    t_p = batch * seq_len
    xp = x_prompt.reshape(t_p, d)
    tm = 1024
    tables_p = _rope_tables(jnp.arange(seq_len, dtype=jnp.int32))
    z_p, k_p, v_p = _inproj(xp, norm_mix, w_in_bf, tables_p, qn, kn, seq_len, tm)
    ret_p, state_p = _ret_prompt(z_p, batch, seq_len)
    moba_p = _moba_prompt(z_p, batch, seq_len)

    xs_rows = _pad_rows(x_sample.reshape(n_dec, d), SAMPLE_ROWS)
    tables_s = _rope_tables(jnp.full((SAMPLE_ROWS,), past_len, jnp.int32))
    z_s, k_s, v_s = _inproj(xs_rows, norm_mix, w_in_bf, tables_s, qn, kn, SAMPLE_ROWS, SAMPLE_ROWS)
    zs = z_s[:n_dec]
    ret_s, state_s = _ret_sample(zs[:, off[0]:off[1]], zs[:, off[1]:off[2]], zs[:, off[2]:off[3]],
                                 zs[:, off[3]:off[4]], state_ret[0])
    heads = lambda a: a.reshape(n_dec, MOBA_HEADS, MOBA_HEAD_DIM)
    mk_s, mv_s = k_s[:n_dec], v_s[:n_dec]
    pool_shape = cache_k.shape[1:]
    moba_s = _moba_sample(heads(zs[:, off[4]:off[5]]), heads(mk_s), heads(mv_s),
                          cache_k.reshape(pool_shape), cache_v.reshape(pool_shape), page_table)

    h_p, hn_p, idx_p, gate_p = _outproj(xp, ret_p, moba_p, w_o_bf, norm_ffn, wr_pad, br_pad, 256)
    h_s, hn_s, idx_s, gate_s = _outproj(xs_rows, _pad_rows(ret_s, SAMPLE_ROWS),
                                        _pad_rows(moba_s, SAMPLE_ROWS), w_o_bf, norm_ffn,
                                        wr_pad, br_pad, SAMPLE_ROWS)

    n_tok = t_p + n_dec
    top_idx = jnp.concatenate([idx_p[:, :TOP_K], idx_s[:n_dec, :TOP_K]], axis=0)
    n_assign = n_tok * TOP_K
    n_chunks = n_assign // MOE_ROW_BLOCK + N_EXPERTS
    n_slots = n_chunks * MOE_ROW_BLOCK
    n_items = N_EXPERTS + pl.cdiv(n_assign, MOE_ITEM_ROWS)
    dest, pad_lo, pad_hi, items = _route(top_idx, n_items)
    rows3 = lambda a: a.reshape(-1, ROW_PIECES, d // ROW_PIECES)
    xs3 = _dispatch(rows3(hn_p), rows3(hn_s), n_dec, dest, pad_lo, pad_hi, n_slots)
    out2 = _moe(xs3, items, w_gu[0], b_gu[0], w_down[0], b_down[0])
    dest_pad = jnp.concatenate([dest, jnp.zeros(((SAMPLE_ROWS - n_dec) * TOP_K,), jnp.int32)])
    h2_p = _combine(dest_pad, h_p, gate_p, out2, 0, COMBINE_TOKENS)
    h2_s = _combine(dest_pad, h_s, gate_s, out2, t_p, SAMPLE_ROWS)

    y_p = _ple(h2_p, p_prompt[0].reshape(t_p, -1), norm_ple, wg_ple_bf, wp_ple_bf, 512)
    y_s = _ple(h2_s, _pad_rows(p_sample[0].reshape(n_dec, -1), SAMPLE_ROWS), norm_ple,
               wg_ple_bf, wp_ple_bf, SAMPLE_ROWS)

    kv = lambda a, n, l: a.reshape(1, n, l, MOBA_HEADS, MOBA_HEAD_DIM)
    return (y_p.reshape(batch, seq_len, d), y_s[:n_dec].reshape(n_dec, 1, d),
            kv(k_p, batch, seq_len), kv(v_p, batch, seq_len),
            state_p[None], kv(mk_s, n_dec, 1), kv(mv_s, n_dec, 1), state_s[None])
```

```python
import functools
import math

import jax
import jax.numpy as jnp
from jax import lax
from jax.experimental import pallas as pl
from jax.experimental.pallas import tpu as pltpu

F32 = jnp.float32
BF16 = jnp.bfloat16

RET_HEADS = 4
RET_DK = 256
RET_DV = 256
RET_ROPE_THETA = 10000.0
MOBA_HEADS = 8
MOBA_HEAD_DIM = 128
MOBA_BLOCK = 256
MOBA_TOPK = 3
ROPE_THETA = 500000.0
ROPE_DIM = MOBA_HEAD_DIM // 4
N_EXPERTS = 32
TOP_K = 4
SWIGLU_LIMIT = 7.0
SWIGLU_ALPHA = 1.702
EPS = 1e-6

LANES = 128
SUBLANES = 8
VMEM_LIMIT_BYTES = 56 * 1024 * 1024

SAMPLE_ROWS = 16
RET_CHUNK = 256
MOE_ROW_BLOCK = 128
MOE_MATMUL_ROWS = 256
MOE_ITEM_ROWS = 1536
MOE_FF_TILE = 512
MOE_VMEM_LIMIT_BYTES = 60 * 1024 * 1024
ROW_PIECES = 16
COMBINE_TOKENS = 256
MOBA_KV_GROUP = 4


def _cparams(semantics, vmem=VMEM_LIMIT_BYTES):
    return pltpu.CompilerParams(dimension_semantics=semantics, vmem_limit_bytes=vmem)


def _nt_dot(a, b, **kw):
    return lax.dot_general(a, b, (((1,), (1,)), ((), ())), preferred_element_type=F32, **kw)


def _rms(x):
    return x * lax.rsqrt(jnp.mean(x * x, axis=-1, keepdims=True) + EPS)


def _rnd(a):
    return a.astype(BF16).astype(F32)


_IN_TN = 512


def _inproj_kernel(x_ref, g_ref, w_ref, cr_ref, sr_ref, cm_ref, sa_ref, sb_ref, qn_ref, kn_ref,
                   z_ref, k_ref, v_ref, xn_ref):
    j = pl.program_id(1)
    n_ret = RET_HEADS * RET_DK // _IN_TN
    n_moba = MOBA_HEADS * MOBA_HEAD_DIM // _IN_TN
    moba0 = 4 * n_ret

    @pl.when(j == 0)
    def _():
        xn_ref[...] = (_rms(x_ref[...]) * g_ref[...]).astype(BF16)

    acc = jnp.dot(xn_ref[...], w_ref[...], preferred_element_type=F32)

    @pl.when(j < 2 * n_ret)
    def _():
        half = RET_DK // 2
        c = cr_ref[...]
        s = sr_ref[...]
        scale = jnp.where(j < n_ret, 1.0, RET_DK ** -0.5).astype(F32)
        for hh in range(_IN_TN // RET_DK):
            x1 = acc[:, hh * RET_DK:hh * RET_DK + half]
            x2 = acc[:, hh * RET_DK + half:(hh + 1) * RET_DK]
            z_ref[:, hh * RET_DK:hh * RET_DK + half] = (x1 * c - x2 * s) * scale
            z_ref[:, hh * RET_DK + half:(hh + 1) * RET_DK] = (x2 * c + x1 * s) * scale

    @pl.when(((j >= 2 * n_ret) & (j < moba0)) | (j >= moba0 + 2 * n_moba))
    def _():
        z_ref[...] = acc

    @pl.when(j >= moba0 + 2 * n_moba)
    def _():
        v_ref[...] = acc

    @pl.when((j >= moba0) & (j < moba0 + 2 * n_moba))
    def _():
        gain = jnp.where(j < moba0 + n_moba, qn_ref[...], kn_ref[...])
        for hh in range(_IN_TN // MOBA_HEAD_DIM):
            t = _rms(acc[:, hh * MOBA_HEAD_DIM:(hh + 1) * MOBA_HEAD_DIM]) * gain
            up = pltpu.roll(t, MOBA_HEAD_DIM - ROPE_DIM // 2, 1)
            dn = pltpu.roll(t, ROPE_DIM // 2, 1)
            z_ref[:, hh * MOBA_HEAD_DIM:(hh + 1) * MOBA_HEAD_DIM] = (
                t * cm_ref[...] + up * sa_ref[...] + dn * sb_ref[...])

    @pl.when((j >= moba0 + n_moba) & (j < moba0 + 2 * n_moba))
    def _():
        k_ref[...] = z_ref[...]


def _rope_tables(pos):
    posf = pos.astype(F32)[:, None]
    half = RET_DK // 2
    inv = RET_ROPE_THETA ** (-2.0 * jnp.arange(half, dtype=F32) / RET_DK)
    ang = posf * inv[None, :]
    cr, sr = jnp.cos(ang), jnp.sin(ang)
    mh = ROPE_DIM // 2
    inv_m = ROPE_THETA ** (-2.0 * jnp.arange(mh, dtype=F32) / ROPE_DIM)
    ang_m = posf * inv_m[None, :]
    cos_m, sin_m = jnp.cos(ang_m), jnp.sin(ang_m)
    n = pos.shape[0]
    rest = MOBA_HEAD_DIM - ROPE_DIM
    cm = jnp.concatenate([cos_m, cos_m, jnp.ones((n, rest), F32)], axis=1)
    sa = jnp.concatenate([-sin_m, jnp.zeros((n, MOBA_HEAD_DIM - mh), F32)], axis=1)
    sb = jnp.concatenate([jnp.zeros((n, mh), F32), sin_m, jnp.zeros((n, rest), F32)], axis=1)
    return cr, sr, cm, sa, sb


def _inproj(x2d, norm, w_bf, tables, qn, kn, seq_len, tm):
    t, d = x2d.shape
    n_out = w_bf.shape[1]
    tpos = seq_len // tm
    moba_w = MOBA_HEADS * MOBA_HEAD_DIM
    n_moba = moba_w // _IN_TN
    k0 = (n_out - 2 * moba_w) // _IN_TN
    cr, sr, cm, sa, sb = tables
    tab = pl.BlockSpec((tm, LANES), lambda i, j: (i % tpos, 0))
    vec = pl.BlockSpec((1, LANES), lambda i, j: (0, 0))
    return pl.pallas_call(
        _inproj_kernel,
        grid=(t // tm, n_out // _IN_TN),
        in_specs=[
            pl.BlockSpec((tm, d), lambda i, j: (i, 0)),
            pl.BlockSpec((1, d), lambda i, j: (0, 0)),
            pl.BlockSpec((d, _IN_TN), lambda i, j: (0, j)),
            tab, tab, tab, tab, tab, vec, vec,
        ],
        out_specs=[pl.BlockSpec((tm, _IN_TN), lambda i, j: (i, j)),
                   pl.BlockSpec((tm, _IN_TN), lambda i, j: (i, jnp.clip(j - k0, 0, n_moba - 1))),
                   pl.BlockSpec((tm, _IN_TN), lambda i, j: (i, jnp.clip(j - k0 - n_moba, 0, n_moba - 1)))],
        out_shape=[jax.ShapeDtypeStruct((t, n_out), F32),
                   jax.ShapeDtypeStruct((t, moba_w), F32), jax.ShapeDtypeStruct((t, moba_w), F32)],
        scratch_shapes=[pltpu.VMEM((tm, d), BF16)],
        compiler_params=_cparams(("arbitrary", "arbitrary")),
        name="inproj",
    )(x2d, norm, w_bf, cr, sr, cm, sa, sb, qn, kn)


def _ret_gate(o, g):
    return _rms(o) * (g * jax.nn.sigmoid(g))


def _ret_prompt_kernel(logg_ref, q_ref, k_ref, v_ref, g_ref, o_ref, st_ref, r_ref):
    c = pl.program_id(1)
    n = q_ref.shape[0]

    @pl.when(c == 0)
    def _():
        r_ref[...] = jnp.zeros_like(r_ref)

    i = lax.broadcasted_iota(jnp.int32, (n, 1), 0).astype(F32)
    jj = lax.broadcasted_iota(jnp.int32, (1, n), 1).astype(F32)
    diff = i - jj
    for h in range(RET_HEADS):
        lg = logg_ref[h]
        inner = jnp.where(diff >= 0, jnp.exp(lg * jnp.maximum(diff, 0.0)), 0.0)
        q_dec = jnp.exp(lg * (i + 1.0))
        k_dec = jnp.exp(lg * (n - 1.0 - i))
        chunk_dec = jnp.exp(lg * jnp.full((1, 1), float(n), F32))
        cols = slice(h * RET_DK, (h + 1) * RET_DK)
        q = q_ref[:, cols]
        k = k_ref[:, cols]
        vb = v_ref[:, cols].astype(BF16)
        r = r_ref[h]
        s = _nt_dot(q.astype(BF16), k.astype(BF16)) * inner
        o = (jnp.dot(s.astype(BF16), vb, preferred_element_type=F32)
             + jnp.dot((q * q_dec).astype(BF16), r.astype(BF16), preferred_element_type=F32))
        kd_t = (k * k_dec).T.astype(BF16)
        r_new = r * chunk_dec + jnp.dot(kd_t, vb, preferred_element_type=F32)
        r_ref[h] = r_new
        o_ref[:, cols] = _ret_gate(o, g_ref[:, cols])

        @pl.when(c == pl.num_programs(1) - 1)
        def _():
            st_ref[0, h] = r_new


def _ret_log_decay():
    return jnp.log1p(-jnp.exp2(-5.0 - jnp.arange(RET_HEADS, dtype=F32)))


def _ret_prompt(z, batch, seq_len):
    nc = seq_len // RET_CHUNK
    w = RET_HEADS * RET_DK
    col = lambda sec: pl.BlockSpec((RET_CHUNK, w), lambda b, c: (b * nc + c, sec))
    return pl.pallas_call(
        _ret_prompt_kernel,
        grid=(batch, nc),
        in_specs=[pl.BlockSpec(memory_space=pltpu.SMEM), col(0), col(1), col(2), col(3)],
        out_specs=[
            pl.BlockSpec((RET_CHUNK, w), lambda b, c: (b * nc + c, 0)),
            pl.BlockSpec((1, RET_HEADS, RET_DK, RET_DV), lambda b, c: (b, 0, 0, 0)),
        ],
        scratch_shapes=[pltpu.VMEM((RET_HEADS, RET_DK, RET_DV), F32)],
        out_shape=[
            jax.ShapeDtypeStruct((batch * seq_len, w), F32),
            jax.ShapeDtypeStruct((batch, RET_HEADS, RET_DK, RET_DV), F32),
        ],
        compiler_params=_cparams(("parallel", "arbitrary")),
        name="ret_prompt",
    )(_ret_log_decay(), z, z, z, z)


def _ret_sample_kernel(logg_ref, q_ref, kc_ref, kr_ref, v_ref, g_ref, s0_ref, o_ref, st_ref):
    h = pl.program_id(1)
    dec = jnp.exp(logg_ref[h] * jnp.ones((1, 1), F32))
    q = q_ref[0, 0]
    v = v_ref[0, 0]
    r0 = s0_ref[0, 0]
    qk = jnp.sum(q * kr_ref[0, 0], axis=-1, keepdims=True)
    q8 = jnp.broadcast_to(q * dec, (SUBLANES, RET_DK)).astype(BF16)
    qr = jnp.dot(q8, r0.astype(BF16), preferred_element_type=F32)[0:1]
    o = qk * v + qr
    st_ref[0, 0] = r0 * dec + kc_ref[0, 0] * v
    o_ref[0, 0] = _ret_gate(o, g_ref[0, 0])


def _ret_sample(q, k, v, g, state):
    n = q.shape[0]
    row = lambda a: a.reshape(n, RET_HEADS, 1, RET_DK)
    rspec = pl.BlockSpec((1, 1, 1, RET_DK), lambda b, h: (b, h, 0, 0))
    mspec = pl.BlockSpec((1, 1, RET_DK, RET_DV), lambda b, h: (b, h, 0, 0))
    o, st = pl.pallas_call(
        _ret_sample_kernel,
        grid=(n, RET_HEADS),
        in_specs=[pl.BlockSpec(memory_space=pltpu.SMEM),
                  rspec, pl.BlockSpec((1, 1, RET_DK, 1), lambda b, h: (b, h, 0, 0)),
                  rspec, rspec, rspec, mspec],
        out_specs=[rspec, mspec],
        out_shape=[jax.ShapeDtypeStruct((n, RET_HEADS, 1, RET_DV), F32),
                   jax.ShapeDtypeStruct(state.shape, F32)],
        compiler_params=_cparams(("parallel", "parallel")),
        name="ret_sample",
    )(_ret_log_decay(), row(q), k.reshape(n, RET_HEADS, RET_DK, 1), row(k), row(v), row(g), state)
    return o.reshape(n, RET_HEADS * RET_DV), st


MOBA_HEAD_GROUP = 4


def _moba_prompt_kernel(q_ref, k_ref, v_ref, o_ref, kb_ref, vt_ref, km_ref, sel_ref, acc_ref):
    qi = pl.program_id(2)
    blk = MOBA_BLOCK
    hd = MOBA_HEAD_DIM
    heads = range(MOBA_HEAD_GROUP)
    nb = k_ref.shape[0] // blk
    scale = hd ** -0.5
    grp = MOBA_KV_GROUP
    span = grp * blk

    @pl.when(qi == 0)
    def _():
        for hh in heads:
            cols = slice(hh * hd, (hh + 1) * hd)
            kb_ref[hh] = k_ref[:, cols].astype(BF16)
            for n in range(nb):
                rows = slice(n * blk, (n + 1) * blk)
                vt_ref[hh, :, rows] = v_ref[rows, cols].T.astype(BF16)
                km_ref[hh, n:n + 1, :] = jnp.mean(k_ref[rows, cols], axis=0, keepdims=True)

    own = pl.ds(pl.multiple_of(qi * blk, blk), blk)
    key = lax.broadcasted_iota(jnp.int32, (blk, blk), 0)
    qry = lax.broadcasted_iota(jnp.int32, (blk, blk), 1)
    kblock = lax.broadcasted_iota(jnp.int32, (nb, blk), 0)
    past = kblock < qi
    qbs, state = [], []
    for hh in heads:
        qb = q_ref[:, hh * hd:(hh + 1) * hd].astype(BF16)
        qbs.append(qb)
        gate = _nt_dot(km_ref[hh].astype(BF16), qb)
        g = jnp.where(past, gate, -jnp.inf)
        cnt = jnp.zeros((nb, blk), F32)
        for m in range(nb):
            gm = g[m:m + 1, :]
            cnt = cnt + jnp.where(gm > g, 1.0, jnp.where((gm == g) & (kblock > m), 1.0, 0.0))
        sel_ref[hh] = jnp.where(past & (cnt < MOBA_TOPK), 1.0, 0.0)
        s = _nt_dot(kb_ref[hh, own, :], qb) * scale
        s = jnp.where(key <= qry, s, -jnp.inf)
        m0 = jnp.max(s, axis=0, keepdims=True)
        p = jnp.exp(s - m0)
        acc_ref[hh] = jnp.dot(vt_ref[hh, :, own], p.astype(BF16), preferred_element_type=F32)
        state.append((m0, jnp.sum(p, axis=0, keepdims=True)))

    def body(j, carry):
        rows = pl.ds(pl.multiple_of(j * span, span), span)
        out = []
        for hh in heads:
            m_i, l_i = carry[hh]
            sn = _nt_dot(kb_ref[hh, rows, :], qbs[hh]) * scale
            allowed = jnp.concatenate(
                [jnp.broadcast_to(sel_ref[hh, pl.ds(j * grp + t, 1), :], (blk, blk))
                 for t in range(grp)], axis=0)
            sn = jnp.where(allowed > 0.0, sn, -jnp.inf)
            m_new = jnp.maximum(m_i, jnp.max(sn, axis=0, keepdims=True))
            alpha = jnp.exp(m_i - m_new)
            pn = jnp.exp(sn - m_new)
            acc_ref[hh] = alpha * acc_ref[hh] + jnp.dot(vt_ref[hh, :, rows], pn.astype(BF16),
                                                        preferred_element_type=F32)
            out.append((m_new, alpha * l_i + jnp.sum(pn, axis=0, keepdims=True)))
        return tuple(out)

    final = lax.fori_loop(0, (qi + grp - 1) // grp, body, tuple(state))
    for hh in heads:
        o_ref[:, hh * hd:(hh + 1) * hd] = (acc_ref[hh] / final[hh][1]).T


def _moba_prompt(z, batch, seq_len):
    nq = seq_len // MOBA_BLOCK
    hd = MOBA_HEAD_DIM
    hg = MOBA_HEAD_GROUP
    w = hg * hd
    first = (2 * RET_HEADS * RET_DK + 2 * RET_HEADS * RET_DV) // w
    koff = first + MOBA_HEADS // hg
    voff = koff + MOBA_HEADS // hg
    assert nq % MOBA_KV_GROUP == 0 and MOBA_HEADS % hg == 0
    full = lambda off: pl.BlockSpec((seq_len, w), lambda b, h, i: (b, off + h))
    return pl.pallas_call(
        _moba_prompt_kernel,
        grid=(batch, MOBA_HEADS // hg, nq),
        in_specs=[pl.BlockSpec((MOBA_BLOCK, w), lambda b, h, i: (b * nq + i, first + h)),
                  full(koff), full(voff)],
        out_specs=pl.BlockSpec((MOBA_BLOCK, w), lambda b, h, i: (b * nq + i, h)),
        out_shape=jax.ShapeDtypeStruct((batch * seq_len, MOBA_HEADS * hd), F32),
        scratch_shapes=[pltpu.VMEM((hg, seq_len, hd), BF16), pltpu.VMEM((hg, hd, seq_len), BF16),
                        pltpu.VMEM((hg, nq, hd), F32), pltpu.VMEM((hg, nq, MOBA_BLOCK), F32),
                        pltpu.VMEM((hg, hd, MOBA_BLOCK), F32)],
        compiler_params=_cparams(("parallel", "parallel", "arbitrary")),
        name="moba_prompt",
    )(z, z, z)


MOBA_SCAN_PAGES = 8


def _moba_scan_kernel(pt_ref, q_ref, *refs):
    k_refs = refs[:MOBA_SCAN_PAGES]
    sel_ref, gate_ref = refs[MOBA_SCAN_PAGES:]
    p = pl.program_id(1)
    ppb = MOBA_BLOCK // k_refs[0].shape[1]
    blocks_per_step = MOBA_SCAN_PAGES // ppb

    @pl.when(p == 0)
    def _():
        gate_ref[...] = jnp.zeros_like(gate_ref)

    q = _rnd(q_ref[0])
    lane = lax.broadcasted_iota(jnp.int32, gate_ref.shape, 1)
    gate = gate_ref[...]
    for jb in range(blocks_per_step):
        k_sum = sum(jnp.sum(k_refs[jb * ppb + j][0], axis=0) for j in range(ppb))
        g = jnp.sum(q * _rnd(k_sum * (1.0 / MOBA_BLOCK)), axis=1, keepdims=True)
        gate = jnp.where(lane == p * blocks_per_step + jb, g, gate)
    gate_ref[...] = gate

    @pl.when(p == pl.num_programs(1) - 1)
    def _():
        g = jnp.where(lane < pl.num_programs(1) * blocks_per_step, gate, -jnp.inf)
        out = jnp.zeros(gate_ref.shape, jnp.int32)
        for t in range(MOBA_TOPK):
            mx = jnp.max(g, axis=1, keepdims=True)
            idx = jnp.min(jnp.where(g == mx, lane, LANES), axis=1, keepdims=True)
            out = jnp.where(lane == t, idx, out)
            g = jnp.where(lane == idx, -jnp.inf, g)
        sel_ref[0] = out


def _moba_attend_kernel(pt_ref, selp_ref, q_ref, kn_ref, vn_ref, *refs):
    nsrc = (len(refs) - 1) // 2
    k_refs = refs[:nsrc]
    v_refs = refs[nsrc:2 * nsrc]
    o_ref = refs[2 * nsrc]
    h = pl.program_id(1)
    scale = MOBA_HEAD_DIM ** -0.5
    mine2 = lax.broadcasted_iota(jnp.int32, (MOBA_HEADS, MOBA_HEAD_DIM), 0) == h
    mine3 = lax.broadcasted_iota(jnp.int32, (1, MOBA_HEADS, 1), 1) == h
    all2 = lambda a, op: op(op(a, axis=1, keepdims=True), axis=0, keepdims=True)
    all3 = lambda a, op: op(op(a, axis=0, keepdims=True), axis=1, keepdims=True)
    q = _rnd(q_ref[0])
    s_new = all2(jnp.where(mine2, q * _rnd(kn_ref[0]), 0.0), jnp.sum) * scale
    ss = [jnp.where(mine3, jnp.sum(_rnd(kr[0]) * q[None], axis=-1, keepdims=True) * scale, -jnp.inf)
          for kr in k_refs]
    mx = s_new.reshape(1, 1, 1)
    for sj in ss:
        mx = jnp.maximum(mx, all3(sj, jnp.max))
    p_new = jnp.exp(s_new.reshape(1, 1, 1) - mx)
    ps = [jnp.exp(sj - mx) for sj in ss]
    den = p_new
    for pj in ps:
        den = den + all3(pj, jnp.sum)
    inv = 1.0 / den
    acc = jnp.where(mine2, _rnd(p_new * inv).reshape(1, 1) * _rnd(vn_ref[0]), 0.0)
    for pj, vr in zip(ps, v_refs):
        acc = acc + jnp.sum(_rnd(pj * inv) * _rnd(vr[0]), axis=0)
    o_ref[0, 0] = jnp.sum(acc, axis=0, keepdims=True)


def _moba_sample(q, k_new, v_new, cache_k, cache_v, page_table):
    n, npages = page_table.shape
    page = cache_k.shape[1]
    hd = MOBA_HEAD_DIM
    ppb = MOBA_BLOCK // page
    pp = MOBA_SCAN_PAGES
    assert npages % pp == 0 and npages // ppb <= LANES
    page_block = (1, page, MOBA_HEADS, hd)
    tok2 = pl.BlockSpec((1, MOBA_HEADS, hd), lambda b, p, pt: (b, 0, 0))
    scan_specs = [pl.BlockSpec(page_block, (lambda b, p, pt, j=j: (pt[b, p * pp + j], 0, 0, 0)))
                  for j in range(pp)]
    sel = pl.pallas_call(
        _moba_scan_kernel,
        grid_spec=pltpu.PrefetchScalarGridSpec(
            num_scalar_prefetch=1,
            grid=(n, npages // pp),
            in_specs=[tok2] + scan_specs,
            out_specs=pl.BlockSpec((1, MOBA_HEADS, LANES), lambda b, p, pt: (b, 0, 0)),
            scratch_shapes=[pltpu.VMEM((MOBA_HEADS, LANES), F32)],
        ),
        out_shape=jax.ShapeDtypeStruct((n, MOBA_HEADS, LANES), jnp.int32),
        compiler_params=_cparams(("parallel", "arbitrary")),
        name="moba_scan",
    )(page_table, q, *([cache_k] * pp))
    selp = (sel[:, :, :MOBA_TOPK, None] * ppb + jnp.arange(ppb, dtype=jnp.int32)).reshape(-1)
    nsrc = MOBA_TOPK * ppb

    def page_spec(j):
        def index(b, h, pt, sp):
            return (pt[b, sp[(b * MOBA_HEADS + h) * nsrc + j]], 0, 0, 0)
        return pl.BlockSpec(page_block, index)

    tok = pl.BlockSpec((1, MOBA_HEADS, hd), lambda b, h, pt, sp: (b, 0, 0))
    pages = [page_spec(j) for j in range(nsrc)]
    out = pl.pallas_call(
        _moba_attend_kernel,
        grid_spec=pltpu.PrefetchScalarGridSpec(
            num_scalar_prefetch=2,
            grid=(n, MOBA_HEADS),
            in_specs=[tok, tok, tok] + pages + pages,
            out_specs=pl.BlockSpec((1, 1, 1, hd), lambda b, h, pt, sp: (b, h, 0, 0)),
        ),
        out_shape=jax.ShapeDtypeStruct((n, MOBA_HEADS, 1, hd), F32),
        compiler_params=_cparams(("parallel", "parallel")),
        name="moba_attend",
    )(page_table, selp, q, k_new, v_new, *([cache_k] * nsrc), *([cache_v] * nsrc))
    return out.reshape(n, MOBA_HEADS * hd)


def _outproj_kernel(x_ref, ret_ref, moba_ref, wo_ref, nf_ref, wr_ref, br_ref,
                    h_ref, hn_ref, idx_ref, gate_ref):
    rw = ret_ref.shape[1]
    h = (x_ref[...]
         + jnp.dot(ret_ref[...].astype(BF16), wo_ref[:rw, :], preferred_element_type=F32)
         + jnp.dot(moba_ref[...].astype(BF16), wo_ref[rw:, :], preferred_element_type=F32))
    h_ref[...] = h
    hn = _rms(h) * nf_ref[...]
    for p in range(ROW_PIECES):
        hn_ref[pl.ds(p, hn.shape[0], stride=ROW_PIECES), :] = hn[:, p * LANES:(p + 1) * LANES]
    logits = jnp.dot(hn.astype(BF16), wr_ref[...], preferred_element_type=F32) + br_ref[...]
    lane = lax.broadcasted_iota(jnp.int32, logits.shape, 1)
    vals, idxs = [], []
    for _ in range(TOP_K):
        mx = jnp.max(logits, axis=1, keepdims=True)
        ix = jnp.min(jnp.where(logits == mx, lane, LANES), axis=1, keepdims=True)
        vals.append(mx)
        idxs.append(ix)
        logits = jnp.where(lane == ix, -jnp.inf, logits)
    es = [jnp.exp(v - vals[0]) for v in vals]
    den = es[0] + es[1] + es[2] + es[3]
    idx_out = jnp.zeros(lane.shape, jnp.int32)
    gate_out = jnp.zeros(lane.shape, F32)
    for t in range(TOP_K):
        idx_out = jnp.where(lane == t, idxs[t], idx_out)
        gate_out = jnp.where(lane == t, es[t] / den, gate_out)
    idx_ref[...] = idx_out
    gate_ref[...] = gate_out


def _outproj(x2d, ret, moba, wo_bf, norm_ffn, wr_pad, br_pad, tm):
    t, d = x2d.shape
    rw, mw = ret.shape[1], moba.shape[1]
    const = lambda shape: pl.BlockSpec(shape, lambda i: (0, 0))
    rows = lambda w: pl.BlockSpec((tm, w), lambda i: (i, 0))
    return pl.pallas_call(
        _outproj_kernel,
        grid=(t // tm,),
        in_specs=[rows(d), rows(rw), rows(mw), const((rw + mw, d)), const((1, d)),
                  const((d, LANES)), const((1, LANES))],
        out_specs=[rows(d), pl.BlockSpec((tm * ROW_PIECES, LANES), lambda i: (i, 0)), rows(LANES),
                   rows(LANES)],
        out_shape=[jax.ShapeDtypeStruct((t, d), F32),
                   jax.ShapeDtypeStruct((t * ROW_PIECES, d // ROW_PIECES), F32),
                   jax.ShapeDtypeStruct((t, LANES), jnp.int32),
                   jax.ShapeDtypeStruct((t, LANES), F32)],
        compiler_params=_cparams(("parallel",)),
        name="outproj",
    )(x2d, ret, moba, wo_bf, norm_ffn, wr_pad, br_pad)


def _dispatch_kernel(dest_ref, pad_lo_ref, pad_hi_ref, hp_ref, hs_ref, xs_hbm, zero_ref, sem, zsem,
                     *, n_sample):
    i = pl.program_id(0)
    last = pl.num_programs(0) - 1
    tb = hp_ref.shape[0]

    def row_copy(src_ref, first, r, k):
        return pltpu.make_async_copy(src_ref.at[r], xs_hbm.at[dest_ref[(first + r) * TOP_K + k]], sem)

    @pl.when(i < last)
    def _():
        def issue(r, c):
            for k in range(TOP_K):
                row_copy(hp_ref, i * tb, r, k).start(priority=k % 2)
            return c

        lax.fori_loop(0, tb, issue, 0)

    @pl.when(i == last)
    def _():
        for r in range(n_sample):
            for k in range(TOP_K):
                row_copy(hs_ref, last * tb, r, k).start()

    @pl.when(i == 0)
    def _():
        zero_ref[...] = jnp.zeros_like(zero_ref)

        def pad_copy(s):
            return pltpu.make_async_copy(zero_ref.at[0], xs_hbm.at[s], zsem)

        def per_expert(e, c):
            def pad_row(s, c2):
                pad_copy(s).start()
                return c2
            lax.fori_loop(pad_lo_ref[e], pad_hi_ref[e], pad_row, 0)
            return c

        lax.fori_loop(0, N_EXPERTS, per_expert, 0)

        rb = zero_ref.shape[0]
        tail0 = pad_hi_ref[N_EXPERTS - 1]
        n_tail = (xs_hbm.shape[0] - tail0) // rb

        def tail_copy(c):
            rows = pl.ds(pl.multiple_of(tail0 + c * rb, rb), rb)
            return pltpu.make_async_copy(zero_ref, xs_hbm.at[rows], zsem)

        def tail_start(c, c2):
            tail_copy(c).start()
            return c2

        lax.fori_loop(0, n_tail, tail_start, 0)

        def per_expert_wait(e, c):
            def pad_wait(s, c2):
                pad_copy(s).wait()
                return c2
            lax.fori_loop(pad_lo_ref[e], pad_hi_ref[e], pad_wait, 0)
            return c

        lax.fori_loop(0, N_EXPERTS, per_expert_wait, 0)

        def tail_wait(c, c2):
            tail_copy(c).wait()
            return c2

        lax.fori_loop(0, n_tail, tail_wait, 0)

    @pl.when(i < last)
    def _():
        for k in range(TOP_K):
            pltpu.make_async_copy(hp_ref, xs_hbm.at[pl.ds(0, tb)], sem).wait()

    @pl.when(i == last)
    def _():
        for r in range(n_sample):
            for k in range(TOP_K):
                row_copy(hs_ref, last * tb, r, k).wait()


def _dispatch(hn_prompt, hn_sample, n_sample, dest_flat, pad_lo, pad_hi, n_slots):
    tb = COMBINE_TOKENS
    n_prompt = hn_prompt.shape[0]
    assert n_prompt % tb == 0
    row = hn_prompt.shape[1:]
    return pl.pallas_call(
        functools.partial(_dispatch_kernel, n_sample=n_sample),
        grid_spec=pltpu.PrefetchScalarGridSpec(
            num_scalar_prefetch=3,
            grid=(n_prompt // tb + 1,),
            in_specs=[pl.BlockSpec((tb,) + row,
                                   lambda i, d, lo, hi: (jnp.minimum(i, n_prompt // tb - 1), 0, 0)),
                      pl.BlockSpec(hn_sample.shape, lambda i, d, lo, hi: (0, 0, 0))],
            out_specs=pl.BlockSpec(memory_space=pl.ANY),
            scratch_shapes=[pltpu.VMEM((MOE_ROW_BLOCK,) + row, F32),
                            pltpu.SemaphoreType.DMA(()), pltpu.SemaphoreType.DMA(())],
        ),
        out_shape=jax.ShapeDtypeStruct((n_slots,) + row, F32),
        compiler_params=_cparams(("arbitrary",)),
        name="moe_dispatch",
    )(dest_flat, pad_lo, pad_hi, hn_prompt, hn_sample)


def _moe_kernel(item_e, item_row0, item_nch, tail_ref, xs_hbm, wg_ref, wu_ref, bg_ref, bu_ref,
                wd_ref, bd_ref, out_hbm, x_ref, acc_ref, stage_ref, wgb_ref, wub_ref, wdb_ref,
                sem_in, sem_out):
    it = pl.program_id(0)
    f = pl.program_id(1)
    nf = pl.num_programs(1)
    rb = MOE_ROW_BLOCK
    big = MOE_MATMUL_ROWS
    pieces = ROW_PIECES
    nch = item_nch[it]
    row0 = item_row0[it]
    n_big = nch // (big // rb)
    has_tail = nch % (big // rb) != 0
    tail_row = pl.multiple_of(n_big * big, rb)

    def hbm_rows(start, size):
        return pl.ds(pl.multiple_of((row0 + start) * pieces, rb * pieces), size * pieces)

    def in_copy(start, size, slot):
        return pltpu.make_async_copy(xs_hbm.at[hbm_rows(start, size)],
                                     stage_ref.at[slot, pl.ds(0, size * pieces)], sem_in.at[slot])

    def out_copy(start, size, slot):
        return pltpu.make_async_copy(stage_ref.at[slot, pl.ds(0, size * pieces)],
                                     out_hbm.at[hbm_rows(start, size)], sem_out.at[slot])

    def stage_to_x(start, size, slot):
        rows = pl.ds(start, size)
        for p in range(pieces):
            piece = stage_ref[slot, pl.ds(p, size, stride=pieces), :]
            x_ref[rows, p * LANES:(p + 1) * LANES] = piece.astype(BF16)
        acc_ref[rows, :] = jnp.broadcast_to(bd_ref[0], (size, acc_ref.shape[1]))

    def acc_to_stage(start, size, slot):
        rows = pl.ds(start, size)
        for p in range(pieces):
            stage_ref[slot, pl.ds(p, size, stride=pieces), :] = acc_ref[rows, p * LANES:(p + 1) * LANES]

    @pl.when(nch > 0)
    def _():
        wgb_ref[...] = wg_ref[0].astype(BF16)
        wub_ref[...] = wu_ref[0].astype(BF16)
        wdb_ref[...] = wd_ref[0].astype(BF16)
        bg = bg_ref[0]
        bu = bu_ref[0]

        def compute(start, size):
            rows = pl.ds(start, size)
            x = x_ref[rows, :]
            g = jnp.dot(x, wgb_ref[...], preferred_element_type=F32) + bg
            u = jnp.dot(x, wub_ref[...], preferred_element_type=F32) + bu
            g = jnp.minimum(g, SWIGLU_LIMIT)
            u = jnp.clip(u, -SWIGLU_LIMIT, SWIGLU_LIMIT)
            act = (u + 1.0) * g * jax.nn.sigmoid(SWIGLU_ALPHA * g)
            acc_ref[rows, :] += jnp.dot(act.astype(BF16), wdb_ref[...], preferred_element_type=F32)

        @pl.when(f == 0)
        def _():
            @pl.when(n_big > 0)
            def _():
                in_copy(0, big, 0).start()

            def step(r, c):
                slot = r % 2
                start = pl.multiple_of(r * big, big)

                @pl.when(r + 1 < n_big)
                def _():
                    in_copy(start + big, big, 1 - slot).start()

                in_copy(start, big, slot).wait()
                stage_to_x(start, big, slot)
                compute(start, big)
                return c

            lax.fori_loop(0, n_big, step, 0)

            @pl.when(has_tail)
            def _():
                cp = in_copy(tail_row, rb, 0)
                cp.start()
                cp.wait()
                stage_to_x(tail_row, rb, 0)
                compute(tail_row, rb)

        @pl.when((f > 0) & (f < nf - 1))
        def _():
            def step(r, c):
                compute(pl.multiple_of(r * big, big), big)
                return c

            lax.fori_loop(0, n_big, step, 0)

            @pl.when(has_tail)
            def _():
                compute(tail_row, rb)

        @pl.when(f == nf - 1)
        def _():
            def step(r, c):
                slot = r % 2
                start = pl.multiple_of(r * big, big)
                compute(start, big)

                @pl.when(r >= 2)
                def _():
                    out_copy(start - 2 * big, big, slot).wait()

                acc_to_stage(start, big, slot)
                out_copy(start, big, slot).start()
                return c

            lax.fori_loop(0, n_big, step, 0)

            @pl.when(n_big >= 2)
            def _():
                out_copy(0, big, n_big % 2).wait()

            @pl.when(n_big >= 1)
            def _():
                out_copy(0, big, (n_big - 1) % 2).wait()

            @pl.when(has_tail)
            def _():
                compute(tail_row, rb)
                acc_to_stage(tail_row, rb, 0)
                cp = out_copy(tail_row, rb, 0)
                cp.start()
                cp.wait()

    @pl.when((it == pl.num_programs(0) - 1) & (f == nf - 1))
    def _():
        tail0 = tail_ref[0]
        n_tail = (out_hbm.shape[0] // pieces - tail0) // rb
        stage_ref[0] = jnp.zeros(stage_ref.shape[1:], F32)

        def tail_copy(c):
            rows = pl.ds(pl.multiple_of((tail0 + c * rb) * pieces, rb * pieces), rb * pieces)
            return pltpu.make_async_copy(stage_ref.at[0, pl.ds(0, rb * pieces)], out_hbm.at[rows],
                                         sem_out.at[0])

        def tail_start(c, c2):
            tail_copy(c).start()
            return c2

        def tail_wait(c, c2):
            tail_copy(c).wait()
            return c2

        lax.fori_loop(0, n_tail, tail_start, 0)
        lax.fori_loop(0, n_tail, tail_wait, 0)


def _moe(xs3, items, w_gu, b_gu, w_down, b_down):
    item_e, item_row0, item_nch, tail0 = items
    n_items = item_e.shape[0]
    n_slots = xs3.shape[0]
    d = w_gu.shape[1]
    ff = w_down.shape[1]
    tf = MOE_FF_TILE
    nf = ff // tf
    assert MOE_MATMUL_ROWS == 2 * MOE_ROW_BLOCK and MOE_ITEM_ROWS % MOE_MATMUL_ROWS == 0 and nf >= 2
    xs2 = xs3.reshape(n_slots * ROW_PIECES, LANES)

    def ftile(it, f, nch):
        return jnp.where(nch[it] > 0, f, nf - 1)

    wg_spec = pl.BlockSpec((1, d, tf), lambda it, f, e, r0, nch, t0: (e[it], 0, ftile(it, f, nch)))
    wu_spec = pl.BlockSpec((1, d, tf), lambda it, f, e, r0, nch, t0: (e[it], 0, nf + ftile(it, f, nch)))
    bg_spec = pl.BlockSpec((1, 1, tf), lambda it, f, e, r0, nch, t0: (e[it], 0, ftile(it, f, nch)))
    bu_spec = pl.BlockSpec((1, 1, tf), lambda it, f, e, r0, nch, t0: (e[it], 0, nf + ftile(it, f, nch)))
    wd_spec = pl.BlockSpec((1, tf, d), lambda it, f, e, r0, nch, t0: (e[it], ftile(it, f, nch), 0))
    bd_spec = pl.BlockSpec((1, 1, d), lambda it, f, e, r0, nch, t0: (e[it], 0, 0))
    b_gu3 = b_gu.reshape(N_EXPERTS, 1, 2 * ff)
    b_down3 = b_down.reshape(N_EXPERTS, 1, d)
    out2 = pl.pallas_call(
        _moe_kernel,
        grid_spec=pltpu.PrefetchScalarGridSpec(
            num_scalar_prefetch=4,
            grid=(n_items, nf),
            in_specs=[pl.BlockSpec(memory_space=pl.ANY), wg_spec, wu_spec, bg_spec, bu_spec,
                      wd_spec, bd_spec],
            out_specs=pl.BlockSpec(memory_space=pl.ANY),
            scratch_shapes=[
                pltpu.VMEM((MOE_ITEM_ROWS, d), BF16),
                pltpu.VMEM((MOE_ITEM_ROWS, d), F32),
                pltpu.VMEM((2, MOE_MATMUL_ROWS * ROW_PIECES, LANES), F32),
                pltpu.VMEM((d, tf), BF16), pltpu.VMEM((d, tf), BF16), pltpu.VMEM((tf, d), BF16),
                pltpu.SemaphoreType.DMA((2,)), pltpu.SemaphoreType.DMA((2,)),
            ],
        ),
        out_shape=jax.ShapeDtypeStruct(xs2.shape, F32),
        compiler_params=_cparams(("arbitrary", "arbitrary"), MOE_VMEM_LIMIT_BYTES),
        name="moe_experts",
    )(item_e, item_row0, item_nch, tail0, xs2, w_gu, w_gu, b_gu3, b_gu3, w_down, b_down3)
    return out2


def _combine_kernel(dest_ref, h_ref, gate_ref, out_hbm, o_ref, buf_ref, sem, *, tok_off):
    i = pl.program_id(0)
    tb = h_ref.shape[0]
    pieces = ROW_PIECES

    def row_copy(step, r, k, slot):
        d = dest_ref[(tok_off + step * tb + r) * TOP_K + k]
        src = out_hbm.at[pl.ds(pl.multiple_of(d * pieces, pieces), pieces)]
        dst = buf_ref.at[slot, pl.ds(pl.multiple_of((k * tb + r) * pieces, pieces), pieces)]
        return pltpu.make_async_copy(src, dst, sem.at[slot])

    def issue(step, slot):
        def body(r, c):
            for k in range(TOP_K):
                row_copy(step, r, k, slot).start(priority=k % 2)
            return c
        lax.fori_loop(0, tb, body, 0)

    @pl.when(i == 0)
    def _():
        issue(0, 0)

    @pl.when(i + 1 < pl.num_programs(0))
    def _():
        issue(i + 1, (i + 1) % 2)

    slot = i % 2
    pltpu.make_async_copy(out_hbm.at[pl.ds(0, TOP_K * tb * pieces)], buf_ref.at[slot],
                          sem.at[slot]).wait()

    for p in range(pieces):
        f = jnp.zeros((tb, LANES), F32)
        for k in range(TOP_K):
            rows = buf_ref[slot, pl.ds(k * tb * pieces + p, tb, stride=pieces), :]
            f = f + gate_ref[:, k:k + 1] * rows
        o_ref[:, p * LANES:(p + 1) * LANES] = h_ref[:, p * LANES:(p + 1) * LANES] + f


def _combine(dest_flat, h, gates, out2, tok_off, tb):
    t, d = h.shape
    return pl.pallas_call(
        functools.partial(_combine_kernel, tok_off=tok_off),
        grid_spec=pltpu.PrefetchScalarGridSpec(
            num_scalar_prefetch=1,
            grid=(t // tb,),
            in_specs=[pl.BlockSpec((tb, d), lambda i, ds: (i, 0)),
                      pl.BlockSpec((tb, LANES), lambda i, ds: (i, 0)),
                      pl.BlockSpec(memory_space=pl.ANY)],
            out_specs=pl.BlockSpec((tb, d), lambda i, ds: (i, 0)),
            scratch_shapes=[pltpu.VMEM((2, TOP_K * tb * ROW_PIECES, LANES), F32),
                            pltpu.SemaphoreType.DMA((2,))],
        ),
        out_shape=jax.ShapeDtypeStruct((t, d), F32),
        compiler_params=_cparams(("arbitrary",)),
        name="moe_combine",
    )(dest_flat, h, gates, out2)


def _ple_kernel(h_ref, p_ref, n_ref, wg_ref, wp_ref, y_ref):
    h = h_ref[...]
    hn = (_rms(h) * n_ref[...]).astype(BF16)
    gate = jax.nn.sigmoid(jnp.dot(hn, wg_ref[...], preferred_element_type=F32))
    proj = jnp.dot(p_ref[...].astype(BF16), wp_ref[...], preferred_element_type=F32)
    y_ref[...] = h + gate * proj


def _ple(h, p, norm, wg_bf, wp_bf, tm):
    t, d = h.shape
    pd = p.shape[1]
    const = lambda shape: pl.BlockSpec(shape, lambda i: (0, 0))
    return pl.pallas_call(
        _ple_kernel,
        grid=(t // tm,),
        in_specs=[pl.BlockSpec((tm, d), lambda i: (i, 0)), pl.BlockSpec((tm, pd), lambda i: (i, 0)),
                  const((1, d)), const((d, d)), const((pd, d))],
        out_specs=pl.BlockSpec((tm, d), lambda i: (i, 0)),
        out_shape=jax.ShapeDtypeStruct((t, d), F32),
        compiler_params=_cparams(("parallel",)),
        name="ple",
    )(h, p, norm, wg_bf, wp_bf)


def _route(top_idx, n_items):
    rb = MOE_ROW_BLOCK
    flat = top_idx.reshape(-1)
    onehot = (flat[:, None] == jnp.arange(N_EXPERTS, dtype=jnp.int32)[None, :]).astype(jnp.int32)
    csum = jnp.cumsum(onehot, axis=0)
    rank = jnp.sum(onehot * (csum - 1), axis=1)
    counts = csum[-1]
    padded = (counts + rb - 1) // rb * rb
    seg_end = jnp.cumsum(padded)
    seg_start = seg_end - padded
    dest = (jnp.sum(onehot * seg_start[None, :], axis=1) + rank).astype(jnp.int32)
    pad_lo = (seg_start + counts).astype(jnp.int32)
    pad_hi = seg_end.astype(jnp.int32)
    per = (padded + MOE_ITEM_ROWS - 1) // MOE_ITEM_ROWS
    item_end = jnp.cumsum(per)
    item_start = item_end - per
    ids = jnp.arange(n_items, dtype=jnp.int32)
    e = jnp.minimum(jnp.searchsorted(item_end, ids, side="right"), N_EXPERTS - 1).astype(jnp.int32)
    valid = ids < item_end[-1]
    piece = ids - item_start[e]
    row0 = seg_start[e] + piece * MOE_ITEM_ROWS
    rows = jnp.clip(padded[e] - piece * MOE_ITEM_ROWS, 0, MOE_ITEM_ROWS)
    nch = jnp.where(valid, rows // rb, 0).astype(jnp.int32)
    last_e = e[jnp.maximum(item_end[-1] - 1, 0)]
    item_e = jnp.where(valid, e, last_e).astype(jnp.int32)
    item_row0 = jnp.where(valid, row0, 0).astype(jnp.int32)
    return dest, pad_lo, pad_hi, (item_e, item_row0, nch, pad_hi[-1:])


def _pad_rows(a, rows):
    return jnp.concatenate([a, jnp.zeros((rows - a.shape[0],) + a.shape[1:], a.dtype)], axis=0)


def kernel(x_prompt, x_sample, cache_k, cache_v, state_ret, page_table, p_prompt, p_sample,
           norm_mix, w_in, q_norm, k_norm, w_o, norm_ffn, w_router, b_router, w_gu, b_gu,
           w_down, b_down, norm_ple, w_ple_gate, w_ple_proj):
    depth = norm_mix.shape[0]
    assert depth == 1
    batch, seq_len, d = x_prompt.shape
    n_dec, dec_seq, _ = x_sample.shape
    assert dec_seq == 1 and n_dec <= SAMPLE_ROWS
    past_len = page_table.shape[1] * cache_k.shape[2]
    ret_w = RET_HEADS * RET_DK
    moba_w = MOBA_HEADS * MOBA_HEAD_DIM
    off = [0, ret_w, 2 * ret_w, 3 * ret_w, 4 * ret_w, 4 * ret_w + moba_w, 4 * ret_w + 2 * moba_w]

    w_in_bf = w_in[0].astype(BF16)
    w_o_bf = w_o[0].astype(BF16)
    wg_ple_bf = w_ple_gate[0].astype(BF16)
    wp_ple_bf = w_ple_proj[0].astype(BF16)
    wr_pad = jnp.concatenate([w_router[0], jnp.zeros((d, LANES - N_EXPERTS), F32)],
                             axis=1).astype(BF16)
    br_pad = jnp.concatenate([b_router[0], jnp.full((LANES - N_EXPERTS,), -jnp.inf, F32)])[None, :]
    qn, kn = q_norm, k_norm

    t_p = batch * seq_len
    xp = x_prompt.reshape(t_p, d)
    tm = 1024
    tables_p = _rope_tables(jnp.arange(seq_len, dtype=jnp.int32))
    z_p, k_p, v_p = _inproj(xp, norm_mix, w_in_bf, tables_p, qn, kn, seq_len, tm)
    ret_p, state_p = _ret_prompt(z_p, batch, seq_len)
    moba_p = _moba_prompt(z_p, batch, seq_len)

    xs_rows = _pad_rows(x_sample.reshape(n_dec, d), SAMPLE_ROWS)
    tables_s = _rope_tables(jnp.full((SAMPLE_ROWS,), past_len, jnp.int32))
    z_s, k_s, v_s = _inproj(xs_rows, norm_mix, w_in_bf, tables_s, qn, kn, SAMPLE_ROWS, SAMPLE_ROWS)
    zs = z_s[:n_dec]
    ret_s, state_s = _ret_sample(zs[:, off[0]:off[1]], zs[:, off[1]:off[2]], zs[:, off[2]:off[3]],
                                 zs[:, off[3]:off[4]], state_ret[0])
    heads = lambda a: a.reshape(n_dec, MOBA_HEADS, MOBA_HEAD_DIM)
    mk_s, mv_s = k_s[:n_dec], v_s[:n_dec]
    pool_shape = cache_k.shape[1:]
    moba_s = _moba_sample(heads(zs[:, off[4]:off[5]]), heads(mk_s), heads(mv_s),
                          cache_k.reshape(pool_shape), cache_v.reshape(pool_shape), page_table)

    h_p, hn_p, idx_p, gate_p = _outproj(xp, ret_p, moba_p, w_o_bf, norm_ffn, wr_pad, br_pad, 256)
    h_s, hn_s, idx_s, gate_s = _outproj(xs_rows, _pad_rows(ret_s, SAMPLE_ROWS),
                                        _pad_rows(moba_s, SAMPLE_ROWS), w_o_bf, norm_ffn,
                                        wr_pad, br_pad, SAMPLE_ROWS)

    n_tok = t_p + n_dec
    top_idx = jnp.concatenate([idx_p[:, :TOP_K], idx_s[:n_dec, :TOP_K]], axis=0)
    n_assign = n_tok * TOP_K
    n_chunks = n_assign // MOE_ROW_BLOCK + N_EXPERTS
    n_slots = n_chunks * MOE_ROW_BLOCK
    n_items = N_EXPERTS + pl.cdiv(n_assign, MOE_ITEM_ROWS)
    dest, pad_lo, pad_hi, items = _route(top_idx, n_items)
    rows3 = lambda a: a.reshape(-1, ROW_PIECES, d // ROW_PIECES)
    xs3 = _dispatch(rows3(hn_p), rows3(hn_s), n_dec, dest, pad_lo, pad_hi, n_slots)
    out2 = _moe(xs3, items, w_gu[0], b_gu[0], w_down[0], b_down[0])
    dest_pad = jnp.concatenate([dest, jnp.zeros(((SAMPLE_ROWS - n_dec) * TOP_K,), jnp.int32)])
    h2_p = _combine(dest_pad, h_p, gate_p, out2, 0, COMBINE_TOKENS)
    h2_s = _combine(dest_pad, h_s, gate_s, out2, t_p, SAMPLE_ROWS)

    y_p = _ple(h2_p, p_prompt[0].reshape(t_p, -1), norm_ple, wg_ple_bf, wp_ple_bf, 512)
    y_s = _ple(h2_s, _pad_rows(p_sample[0].reshape(n_dec, -1), SAMPLE_ROWS), norm_ple,
               wg_ple_bf, wp_ple_bf, SAMPLE_ROWS)

    kv = lambda a, n, l: a.reshape(1, n, l, MOBA_HEADS, MOBA_HEAD_DIM)
    return (y_p.reshape(batch, seq_len, d), y_s[:n_dec].reshape(n_dec, 1, d),
            kv(k_p, batch, seq_len), kv(v_p, batch, seq_len),
            state_p[None], kv(mk_s, n_dec, 1), kv(mv_s, n_dec, 1), state_s[None])
```

```python
import functools
import math

import jax
import jax.numpy as jnp
from jax import lax
from jax.experimental import pallas as pl
from jax.experimental.pallas import tpu as pltpu

F32 = jnp.float32
BF16 = jnp.bfloat16

RET_HEADS = 4
RET_DK = 256
RET_DV = 256
RET_ROPE_THETA = 10000.0
MOBA_HEADS = 8
MOBA_HEAD_DIM = 128
MOBA_BLOCK = 256
MOBA_TOPK = 3
ROPE_THETA = 500000.0
ROPE_DIM = MOBA_HEAD_DIM // 4
N_EXPERTS = 32
TOP_K = 4
SWIGLU_LIMIT = 7.0
SWIGLU_ALPHA = 1.702
EPS = 1e-6

LANES = 128
SUBLANES = 8
VMEM_LIMIT_BYTES = 56 * 1024 * 1024

SAMPLE_ROWS = 16
RET_CHUNK = 256
MOE_ROW_BLOCK = 128
MOE_MATMUL_ROWS = 256
MOE_ITEM_ROWS = 1536
MOE_FF_TILE = 512
MOE_VMEM_LIMIT_BYTES = 60 * 1024 * 1024
ROW_PIECES = 16
COMBINE_TOKENS = 256
COMBINE_BLOCK = 128
COMBINE_ROW_PITCH = 24
MOBA_KV_GROUP = 4


def _cparams(semantics, vmem=VMEM_LIMIT_BYTES):
    return pltpu.CompilerParams(dimension_semantics=semantics, vmem_limit_bytes=vmem)


def _nt_dot(a, b, **kw):
    return lax.dot_general(a, b, (((1,), (1,)), ((), ())), preferred_element_type=F32, **kw)


def _rms(x):
    return x * lax.rsqrt(jnp.mean(x * x, axis=-1, keepdims=True) + EPS)


def _rnd(a):
    return a.astype(BF16).astype(F32)


_IN_TN = 512


def _inproj_kernel(x_ref, g_ref, w_ref, cr_ref, sr_ref, cm_ref, sa_ref, sb_ref, qn_ref, kn_ref,
                   z_ref, k_ref, v_ref, xn_ref):
    j = pl.program_id(1)
    n_ret = RET_HEADS * RET_DK // _IN_TN
    n_moba = MOBA_HEADS * MOBA_HEAD_DIM // _IN_TN
    moba0 = 4 * n_ret

    @pl.when(j == 0)
    def _():
        xn_ref[...] = (_rms(x_ref[...]) * g_ref[...]).astype(BF16)

    acc = jnp.dot(xn_ref[...], w_ref[...], preferred_element_type=F32)

    @pl.when(j < 2 * n_ret)
    def _():
        half = RET_DK // 2
        c = cr_ref[...]
        s = sr_ref[...]
        scale = jnp.where(j < n_ret, 1.0, RET_DK ** -0.5).astype(F32)
        for hh in range(_IN_TN // RET_DK):
            x1 = acc[:, hh * RET_DK:hh * RET_DK + half]
            x2 = acc[:, hh * RET_DK + half:(hh + 1) * RET_DK]
            z_ref[:, hh * RET_DK:hh * RET_DK + half] = (x1 * c - x2 * s) * scale
            z_ref[:, hh * RET_DK + half:(hh + 1) * RET_DK] = (x2 * c + x1 * s) * scale

    @pl.when(((j >= 2 * n_ret) & (j < moba0)) | (j >= moba0 + 2 * n_moba))
    def _():
        z_ref[...] = acc

    @pl.when(j >= moba0 + 2 * n_moba)
    def _():
        v_ref[...] = acc

    @pl.when((j >= moba0) & (j < moba0 + 2 * n_moba))
    def _():
        gain = jnp.where(j < moba0 + n_moba, qn_ref[...], kn_ref[...])
        for hh in range(_IN_TN // MOBA_HEAD_DIM):
            t = _rms(acc[:, hh * MOBA_HEAD_DIM:(hh + 1) * MOBA_HEAD_DIM]) * gain
            up = pltpu.roll(t, MOBA_HEAD_DIM - ROPE_DIM // 2, 1)
            dn = pltpu.roll(t, ROPE_DIM // 2, 1)
            z_ref[:, hh * MOBA_HEAD_DIM:(hh + 1) * MOBA_HEAD_DIM] = (
                t * cm_ref[...] + up * sa_ref[...] + dn * sb_ref[...])

    @pl.when((j >= moba0 + n_moba) & (j < moba0 + 2 * n_moba))
    def _():
        k_ref[...] = z_ref[...]


def _rope_tables(pos):
    posf = pos.astype(F32)[:, None]
    half = RET_DK // 2
    inv = RET_ROPE_THETA ** (-2.0 * jnp.arange(half, dtype=F32) / RET_DK)
    ang = posf * inv[None, :]
    cr, sr = jnp.cos(ang), jnp.sin(ang)
    mh = ROPE_DIM // 2
    inv_m = ROPE_THETA ** (-2.0 * jnp.arange(mh, dtype=F32) / ROPE_DIM)
    ang_m = posf * inv_m[None, :]
    cos_m, sin_m = jnp.cos(ang_m), jnp.sin(ang_m)
    n = pos.shape[0]
    rest = MOBA_HEAD_DIM - ROPE_DIM
    cm = jnp.concatenate([cos_m, cos_m, jnp.ones((n, rest), F32)], axis=1)
    sa = jnp.concatenate([-sin_m, jnp.zeros((n, MOBA_HEAD_DIM - mh), F32)], axis=1)
    sb = jnp.concatenate([jnp.zeros((n, mh), F32), sin_m, jnp.zeros((n, rest), F32)], axis=1)
    return cr, sr, cm, sa, sb


def _inproj(x2d, norm, w_bf, tables, qn, kn, seq_len, tm):
    t, d = x2d.shape
    n_out = w_bf.shape[1]
    tpos = seq_len // tm
    moba_w = MOBA_HEADS * MOBA_HEAD_DIM
    n_moba = moba_w // _IN_TN
    k0 = (n_out - 2 * moba_w) // _IN_TN
    cr, sr, cm, sa, sb = tables
    tab = pl.BlockSpec((tm, LANES), lambda i, j: (i % tpos, 0))
    vec = pl.BlockSpec((1, LANES), lambda i, j: (0, 0))
    return pl.pallas_call(
        _inproj_kernel,
        grid=(t // tm, n_out // _IN_TN),
        in_specs=[
            pl.BlockSpec((tm, d), lambda i, j: (i, 0)),
            pl.BlockSpec((1, d), lambda i, j: (0, 0)),
            pl.BlockSpec((d, _IN_TN), lambda i, j: (0, j)),
            tab, tab, tab, tab, tab, vec, vec,
        ],
        out_specs=[pl.BlockSpec((tm, _IN_TN), lambda i, j: (i, j)),
                   pl.BlockSpec((tm, _IN_TN), lambda i, j: (i, jnp.clip(j - k0, 0, n_moba - 1))),
                   pl.BlockSpec((tm, _IN_TN), lambda i, j: (i, jnp.clip(j - k0 - n_moba, 0, n_moba - 1)))],
        out_shape=[jax.ShapeDtypeStruct((t, n_out), F32),
                   jax.ShapeDtypeStruct((t, moba_w), F32), jax.ShapeDtypeStruct((t, moba_w), F32)],
        scratch_shapes=[pltpu.VMEM((tm, d), BF16)],
        compiler_params=_cparams(("arbitrary", "arbitrary")),
        name="inproj",
    )(x2d, norm, w_bf, cr, sr, cm, sa, sb, qn, kn)


def _ret_gate(o, g):
    return _rms(o) * (g * jax.nn.sigmoid(g))


def _ret_prompt_kernel(logg_ref, q_ref, k_ref, v_ref, g_ref, o_ref, st_ref, r_ref):
    c = pl.program_id(1)
    n = q_ref.shape[0]

    @pl.when(c == 0)
    def _():
        r_ref[...] = jnp.zeros_like(r_ref)

    i = lax.broadcasted_iota(jnp.int32, (n, 1), 0).astype(F32)
    jj = lax.broadcasted_iota(jnp.int32, (1, n), 1).astype(F32)
    diff = i - jj
    for h in range(RET_HEADS):
        lg = logg_ref[h]
        inner = jnp.where(diff >= 0, jnp.exp(lg * jnp.maximum(diff, 0.0)), 0.0)
        q_dec = jnp.exp(lg * (i + 1.0))
        k_dec = jnp.exp(lg * (n - 1.0 - i))
        chunk_dec = jnp.exp(lg * jnp.full((1, 1), float(n), F32))
        cols = slice(h * RET_DK, (h + 1) * RET_DK)
        q = q_ref[:, cols]
        k = k_ref[:, cols]
        vb = v_ref[:, cols].astype(BF16)
        r = r_ref[h]
        s = _nt_dot(q.astype(BF16), k.astype(BF16)) * inner
        o = (jnp.dot(s.astype(BF16), vb, preferred_element_type=F32)
             + jnp.dot((q * q_dec).astype(BF16), r.astype(BF16), preferred_element_type=F32))
        kd_t = (k * k_dec).T.astype(BF16)
        r_new = r * chunk_dec + jnp.dot(kd_t, vb, preferred_element_type=F32)
        r_ref[h] = r_new
        o_ref[:, cols] = _ret_gate(o, g_ref[:, cols])

        @pl.when(c == pl.num_programs(1) - 1)
        def _():
            st_ref[0, h] = r_new


def _ret_log_decay():
    return jnp.log1p(-jnp.exp2(-5.0 - jnp.arange(RET_HEADS, dtype=F32)))


def _ret_prompt(z, batch, seq_len):
    nc = seq_len // RET_CHUNK
    w = RET_HEADS * RET_DK
    col = lambda sec: pl.BlockSpec((RET_CHUNK, w), lambda b, c: (b * nc + c, sec))
    return pl.pallas_call(
        _ret_prompt_kernel,
        grid=(batch, nc),
        in_specs=[pl.BlockSpec(memory_space=pltpu.SMEM), col(0), col(1), col(2), col(3)],
        out_specs=[
            pl.BlockSpec((RET_CHUNK, w), lambda b, c: (b * nc + c, 0)),
            pl.BlockSpec((1, RET_HEADS, RET_DK, RET_DV), lambda b, c: (b, 0, 0, 0)),
        ],
        scratch_shapes=[pltpu.VMEM((RET_HEADS, RET_DK, RET_DV), F32)],
        out_shape=[
            jax.ShapeDtypeStruct((batch * seq_len, w), F32),
            jax.ShapeDtypeStruct((batch, RET_HEADS, RET_DK, RET_DV), F32),
        ],
        compiler_params=_cparams(("parallel", "arbitrary")),
        name="ret_prompt",
    )(_ret_log_decay(), z, z, z, z)


def _ret_sample_kernel(logg_ref, q_ref, kc_ref, kr_ref, v_ref, g_ref, s0_ref, o_ref, st_ref):
    h = pl.program_id(1)
    dec = jnp.exp(logg_ref[h] * jnp.ones((1, 1), F32))
    q = q_ref[0, 0]
    v = v_ref[0, 0]
    r0 = s0_ref[0, 0]
    qk = jnp.sum(q * kr_ref[0, 0], axis=-1, keepdims=True)
    q8 = jnp.broadcast_to(q * dec, (SUBLANES, RET_DK)).astype(BF16)
    qr = jnp.dot(q8, r0.astype(BF16), preferred_element_type=F32)[0:1]
    o = qk * v + qr
    st_ref[0, 0] = r0 * dec + kc_ref[0, 0] * v
    o_ref[0, 0] = _ret_gate(o, g_ref[0, 0])


def _ret_sample(q, k, v, g, state):
    n = q.shape[0]
    row = lambda a: a.reshape(n, RET_HEADS, 1, RET_DK)
    rspec = pl.BlockSpec((1, 1, 1, RET_DK), lambda b, h: (b, h, 0, 0))
    mspec = pl.BlockSpec((1, 1, RET_DK, RET_DV), lambda b, h: (b, h, 0, 0))
    o, st = pl.pallas_call(
        _ret_sample_kernel,
        grid=(n, RET_HEADS),
        in_specs=[pl.BlockSpec(memory_space=pltpu.SMEM),
                  rspec, pl.BlockSpec((1, 1, RET_DK, 1), lambda b, h: (b, h, 0, 0)),
                  rspec, rspec, rspec, mspec],
        out_specs=[rspec, mspec],
        out_shape=[jax.ShapeDtypeStruct((n, RET_HEADS, 1, RET_DV), F32),
                   jax.ShapeDtypeStruct(state.shape, F32)],
        compiler_params=_cparams(("parallel", "parallel")),
        name="ret_sample",
    )(_ret_log_decay(), row(q), k.reshape(n, RET_HEADS, RET_DK, 1), row(k), row(v), row(g), state)
    return o.reshape(n, RET_HEADS * RET_DV), st


MOBA_HEAD_GROUP = 4


def _moba_prompt_kernel(q_ref, k_ref, v_ref, o_ref, kb_ref, vt_ref, km_ref, sel_ref, acc_ref):
    qi = pl.program_id(2)
    blk = MOBA_BLOCK
    hd = MOBA_HEAD_DIM
    heads = range(MOBA_HEAD_GROUP)
    nb = k_ref.shape[0] // blk
    scale = hd ** -0.5
    grp = MOBA_KV_GROUP
    span = grp * blk

    @pl.when(qi == 0)
    def _():
        for hh in heads:
            cols = slice(hh * hd, (hh + 1) * hd)
            kb_ref[hh] = k_ref[:, cols].astype(BF16)
            for n in range(nb):
                rows = slice(n * blk, (n + 1) * blk)
                vt_ref[hh, :, rows] = v_ref[rows, cols].T.astype(BF16)
                km_ref[hh, n:n + 1, :] = jnp.mean(k_ref[rows, cols], axis=0, keepdims=True)

    own = pl.ds(pl.multiple_of(qi * blk, blk), blk)
    key = lax.broadcasted_iota(jnp.int32, (blk, blk), 0)
    qry = lax.broadcasted_iota(jnp.int32, (blk, blk), 1)
    kblock = lax.broadcasted_iota(jnp.int32, (nb, blk), 0)
    past = kblock < qi
    qbs, state = [], []
    for hh in heads:
        qb = q_ref[:, hh * hd:(hh + 1) * hd].astype(BF16)
        qbs.append(qb)
        gate = _nt_dot(km_ref[hh].astype(BF16), qb)
        g = jnp.where(past, gate, -jnp.inf)
        cnt = jnp.zeros((nb, blk), F32)
        for m in range(nb):
            gm = g[m:m + 1, :]
            cnt = cnt + jnp.where(gm > g, 1.0, jnp.where((gm == g) & (kblock > m), 1.0, 0.0))
        sel_ref[hh] = jnp.where(past & (cnt < MOBA_TOPK), 1.0, 0.0)
        s = _nt_dot(kb_ref[hh, own, :], qb) * scale
        s = jnp.where(key <= qry, s, -jnp.inf)
        m0 = jnp.max(s, axis=0, keepdims=True)
        p = jnp.exp(s - m0)
        acc_ref[hh] = jnp.dot(vt_ref[hh, :, own], p.astype(BF16), preferred_element_type=F32)
        state.append((m0, jnp.sum(p, axis=0, keepdims=True)))

    def body(j, carry):
        rows = pl.ds(pl.multiple_of(j * span, span), span)
        out = []
        for hh in heads:
            m_i, l_i = carry[hh]
            sn = _nt_dot(kb_ref[hh, rows, :], qbs[hh]) * scale
            allowed = jnp.concatenate(
                [jnp.broadcast_to(sel_ref[hh, pl.ds(j * grp + t, 1), :], (blk, blk))
                 for t in range(grp)], axis=0)
            sn = jnp.where(allowed > 0.0, sn, -jnp.inf)
            m_new = jnp.maximum(m_i, jnp.max(sn, axis=0, keepdims=True))
            alpha = jnp.exp(m_i - m_new)
            pn = jnp.exp(sn - m_new)
            acc_ref[hh] = alpha * acc_ref[hh] + jnp.dot(vt_ref[hh, :, rows], pn.astype(BF16),
                                                        preferred_element_type=F32)
            out.append((m_new, alpha * l_i + jnp.sum(pn, axis=0, keepdims=True)))
        return tuple(out)

    final = lax.fori_loop(0, (qi + grp - 1) // grp, body, tuple(state))
    for hh in heads:
        o_ref[:, hh * hd:(hh + 1) * hd] = (acc_ref[hh] / final[hh][1]).T


def _moba_prompt(z, batch, seq_len):
    nq = seq_len // MOBA_BLOCK
    hd = MOBA_HEAD_DIM
    hg = MOBA_HEAD_GROUP
    w = hg * hd
    first = (2 * RET_HEADS * RET_DK + 2 * RET_HEADS * RET_DV) // w
    koff = first + MOBA_HEADS // hg
    voff = koff + MOBA_HEADS // hg
    assert nq % MOBA_KV_GROUP == 0 and MOBA_HEADS % hg == 0
    full = lambda off: pl.BlockSpec((seq_len, w), lambda b, h, i: (b, off + h))
    return pl.pallas_call(
        _moba_prompt_kernel,
        grid=(batch, MOBA_HEADS // hg, nq),
        in_specs=[pl.BlockSpec((MOBA_BLOCK, w), lambda b, h, i: (b * nq + i, first + h)),
                  full(koff), full(voff)],
        out_specs=pl.BlockSpec((MOBA_BLOCK, w), lambda b, h, i: (b * nq + i, h)),
        out_shape=jax.ShapeDtypeStruct((batch * seq_len, MOBA_HEADS * hd), F32),
        scratch_shapes=[pltpu.VMEM((hg, seq_len, hd), BF16), pltpu.VMEM((hg, hd, seq_len), BF16),
                        pltpu.VMEM((hg, nq, hd), F32), pltpu.VMEM((hg, nq, MOBA_BLOCK), F32),
                        pltpu.VMEM((hg, hd, MOBA_BLOCK), F32)],
        compiler_params=_cparams(("parallel", "parallel", "arbitrary")),
        name="moba_prompt",
    )(z, z, z)


MOBA_SCAN_PAGES = 8


def _moba_scan_kernel(pt_ref, q_ref, *refs):
    k_refs = refs[:MOBA_SCAN_PAGES]
    sel_ref, gate_ref = refs[MOBA_SCAN_PAGES:]
    p = pl.program_id(1)
    ppb = MOBA_BLOCK // k_refs[0].shape[1]
    blocks_per_step = MOBA_SCAN_PAGES // ppb

    @pl.when(p == 0)
    def _():
        gate_ref[...] = jnp.zeros_like(gate_ref)

    q = _rnd(q_ref[0])
    lane = lax.broadcasted_iota(jnp.int32, gate_ref.shape, 1)
    gate = gate_ref[...]
    for jb in range(blocks_per_step):
        k_sum = sum(jnp.sum(k_refs[jb * ppb + j][0], axis=0) for j in range(ppb))
        g = jnp.sum(q * _rnd(k_sum * (1.0 / MOBA_BLOCK)), axis=1, keepdims=True)
        gate = jnp.where(lane == p * blocks_per_step + jb, g, gate)
    gate_ref[...] = gate

    @pl.when(p == pl.num_programs(1) - 1)
    def _():
        g = jnp.where(lane < pl.num_programs(1) * blocks_per_step, gate, -jnp.inf)
        out = jnp.zeros(gate_ref.shape, jnp.int32)
        for t in range(MOBA_TOPK):
            mx = jnp.max(g, axis=1, keepdims=True)
            idx = jnp.min(jnp.where(g == mx, lane, LANES), axis=1, keepdims=True)
            out = jnp.where(lane == t, idx, out)
            g = jnp.where(lane == idx, -jnp.inf, g)
        sel_ref[0] = out


def _moba_attend_kernel(pt_ref, selp_ref, q_ref, kn_ref, vn_ref, *refs):
    nsrc = (len(refs) - 1) // 2
    k_refs = refs[:nsrc]
    v_refs = refs[nsrc:2 * nsrc]
    o_ref = refs[2 * nsrc]
    h = pl.program_id(1)
    scale = MOBA_HEAD_DIM ** -0.5
    mine2 = lax.broadcasted_iota(jnp.int32, (MOBA_HEADS, MOBA_HEAD_DIM), 0) == h
    mine3 = lax.broadcasted_iota(jnp.int32, (1, MOBA_HEADS, 1), 1) == h
    all2 = lambda a, op: op(op(a, axis=1, keepdims=True), axis=0, keepdims=True)
    all3 = lambda a, op: op(op(a, axis=0, keepdims=True), axis=1, keepdims=True)
    q = _rnd(q_ref[0])
    s_new = all2(jnp.where(mine2, q * _rnd(kn_ref[0]), 0.0), jnp.sum) * scale
    ss = [jnp.where(mine3, jnp.sum(_rnd(kr[0]) * q[None], axis=-1, keepdims=True) * scale, -jnp.inf)
          for kr in k_refs]
    mx = s_new.reshape(1, 1, 1)
    for sj in ss:
        mx = jnp.maximum(mx, all3(sj, jnp.max))
    p_new = jnp.exp(s_new.reshape(1, 1, 1) - mx)
    ps = [jnp.exp(sj - mx) for sj in ss]
    den = p_new
    for pj in ps:
        den = den + all3(pj, jnp.sum)
    inv = 1.0 / den
    acc = jnp.where(mine2, _rnd(p_new * inv).reshape(1, 1) * _rnd(vn_ref[0]), 0.0)
    for pj, vr in zip(ps, v_refs):
        acc = acc + jnp.sum(_rnd(pj * inv) * _rnd(vr[0]), axis=0)
    o_ref[0, 0] = jnp.sum(acc, axis=0, keepdims=True)


def _moba_sample(q, k_new, v_new, cache_k, cache_v, page_table):
    n, npages = page_table.shape
    page = cache_k.shape[1]
    hd = MOBA_HEAD_DIM
    ppb = MOBA_BLOCK // page
    pp = MOBA_SCAN_PAGES
    assert npages % pp == 0 and npages // ppb <= LANES
    page_block = (1, page, MOBA_HEADS, hd)
    tok2 = pl.BlockSpec((1, MOBA_HEADS, hd), lambda b, p, pt: (b, 0, 0))
    scan_specs = [pl.BlockSpec(page_block, (lambda b, p, pt, j=j: (pt[b, p * pp + j], 0, 0, 0)))
                  for j in range(pp)]
    sel = pl.pallas_call(
        _moba_scan_kernel,
        grid_spec=pltpu.PrefetchScalarGridSpec(
            num_scalar_prefetch=1,
            grid=(n, npages // pp),
            in_specs=[tok2] + scan_specs,
            out_specs=pl.BlockSpec((1, MOBA_HEADS, LANES), lambda b, p, pt: (b, 0, 0)),
            scratch_shapes=[pltpu.VMEM((MOBA_HEADS, LANES), F32)],
        ),
        out_shape=jax.ShapeDtypeStruct((n, MOBA_HEADS, LANES), jnp.int32),
        compiler_params=_cparams(("parallel", "arbitrary")),
        name="moba_scan",
    )(page_table, q, *([cache_k] * pp))
    selp = (sel[:, :, :MOBA_TOPK, None] * ppb + jnp.arange(ppb, dtype=jnp.int32)).reshape(-1)
    nsrc = MOBA_TOPK * ppb

    def page_spec(j):
        def index(b, h, pt, sp):
            return (pt[b, sp[(b * MOBA_HEADS + h) * nsrc + j]], 0, 0, 0)
        return pl.BlockSpec(page_block, index)

    tok = pl.BlockSpec((1, MOBA_HEADS, hd), lambda b, h, pt, sp: (b, 0, 0))
    pages = [page_spec(j) for j in range(nsrc)]
    out = pl.pallas_call(
        _moba_attend_kernel,
        grid_spec=pltpu.PrefetchScalarGridSpec(
            num_scalar_prefetch=2,
            grid=(n, MOBA_HEADS),
            in_specs=[tok, tok, tok] + pages + pages,
            out_specs=pl.BlockSpec((1, 1, 1, hd), lambda b, h, pt, sp: (b, h, 0, 0)),
        ),
        out_shape=jax.ShapeDtypeStruct((n, MOBA_HEADS, 1, hd), F32),
        compiler_params=_cparams(("parallel", "parallel")),
        name="moba_attend",
    )(page_table, selp, q, k_new, v_new, *([cache_k] * nsrc), *([cache_v] * nsrc))
    return out.reshape(n, MOBA_HEADS * hd)


def _outproj_kernel(x_ref, ret_ref, moba_ref, wo_ref, nf_ref, wr_ref, br_ref,
                    h_ref, hn_ref, idx_ref, gate_ref):
    rw = ret_ref.shape[1]
    h = (x_ref[...]
         + jnp.dot(ret_ref[...].astype(BF16), wo_ref[:rw, :], preferred_element_type=F32)
         + jnp.dot(moba_ref[...].astype(BF16), wo_ref[rw:, :], preferred_element_type=F32))
    h_ref[...] = h
    hn = _rms(h) * nf_ref[...]
    for p in range(ROW_PIECES):
        hn_ref[pl.ds(p, hn.shape[0], stride=ROW_PIECES), :] = hn[:, p * LANES:(p + 1) * LANES]
    logits = jnp.dot(hn.astype(BF16), wr_ref[...], preferred_element_type=F32) + br_ref[...]
    lane = lax.broadcasted_iota(jnp.int32, logits.shape, 1)
    vals, idxs = [], []
    for _ in range(TOP_K):
        mx = jnp.max(logits, axis=1, keepdims=True)
        ix = jnp.min(jnp.where(logits == mx, lane, LANES), axis=1, keepdims=True)
        vals.append(mx)
        idxs.append(ix)
        logits = jnp.where(lane == ix, -jnp.inf, logits)
    es = [jnp.exp(v - vals[0]) for v in vals]
    den = es[0] + es[1] + es[2] + es[3]
    idx_out = jnp.zeros(lane.shape, jnp.int32)
    gate_out = jnp.zeros(lane.shape, F32)
    for t in range(TOP_K):
        idx_out = jnp.where(lane == t, idxs[t], idx_out)
        gate_out = jnp.where(lane == t, es[t] / den, gate_out)
    idx_ref[...] = idx_out
    gate_ref[...] = gate_out


def _outproj(x2d, ret, moba, wo_bf, norm_ffn, wr_pad, br_pad, tm):
    t, d = x2d.shape
    rw, mw = ret.shape[1], moba.shape[1]
    const = lambda shape: pl.BlockSpec(shape, lambda i: (0, 0))
    rows = lambda w: pl.BlockSpec((tm, w), lambda i: (i, 0))
    return pl.pallas_call(
        _outproj_kernel,
        grid=(t // tm,),
        in_specs=[rows(d), rows(rw), rows(mw), const((rw + mw, d)), const((1, d)),
                  const((d, LANES)), const((1, LANES))],
        out_specs=[rows(d), pl.BlockSpec((tm * ROW_PIECES, LANES), lambda i: (i, 0)), rows(LANES),
                   rows(LANES)],
        out_shape=[jax.ShapeDtypeStruct((t, d), F32),
                   jax.ShapeDtypeStruct((t * ROW_PIECES, d // ROW_PIECES), F32),
                   jax.ShapeDtypeStruct((t, LANES), jnp.int32),
                   jax.ShapeDtypeStruct((t, LANES), F32)],
        compiler_params=_cparams(("parallel",)),
        name="outproj",
    )(x2d, ret, moba, wo_bf, norm_ffn, wr_pad, br_pad)


def _dispatch_kernel(dest_ref, pad_lo_ref, pad_hi_ref, hp_ref, hs_ref, xs_hbm, zero_ref, sem, zsem,
                     *, n_sample):
    i = pl.program_id(0)
    last = pl.num_programs(0) - 1
    tb = hp_ref.shape[0]

    def row_copy(src_ref, first, r, k):
        return pltpu.make_async_copy(src_ref.at[r], xs_hbm.at[dest_ref[(first + r) * TOP_K + k]], sem)

    @pl.when(i < last)
    def _():
        def issue(r, c):
            for k in range(TOP_K):
                row_copy(hp_ref, i * tb, r, k).start(priority=k % 2)
            return c

        lax.fori_loop(0, tb, issue, 0)

    @pl.when(i == last)
    def _():
        for r in range(n_sample):
            for k in range(TOP_K):
                row_copy(hs_ref, last * tb, r, k).start()

    @pl.when(i == 0)
    def _():
        zero_ref[...] = jnp.zeros_like(zero_ref)

        def pad_copy(s):
            return pltpu.make_async_copy(zero_ref.at[0], xs_hbm.at[s], zsem)

        def per_expert(e, c):
            def pad_row(s, c2):
                pad_copy(s).start()
                return c2
            lax.fori_loop(pad_lo_ref[e], pad_hi_ref[e], pad_row, 0)
            return c

        lax.fori_loop(0, N_EXPERTS, per_expert, 0)

        rb = zero_ref.shape[0]
        tail0 = pad_hi_ref[N_EXPERTS - 1]
        n_tail = (xs_hbm.shape[0] - tail0) // rb

        def tail_copy(c):
            rows = pl.ds(pl.multiple_of(tail0 + c * rb, rb), rb)
            return pltpu.make_async_copy(zero_ref, xs_hbm.at[rows], zsem)

        def tail_start(c, c2):
            tail_copy(c).start()
            return c2

        lax.fori_loop(0, n_tail, tail_start, 0)

        def per_expert_wait(e, c):
            def pad_wait(s, c2):
                pad_copy(s).wait()
                return c2
            lax.fori_loop(pad_lo_ref[e], pad_hi_ref[e], pad_wait, 0)
            return c

        lax.fori_loop(0, N_EXPERTS, per_expert_wait, 0)

        def tail_wait(c, c2):
            tail_copy(c).wait()
            return c2

        lax.fori_loop(0, n_tail, tail_wait, 0)

    @pl.when(i < last)
    def _():
        for k in range(TOP_K):
            pltpu.make_async_copy(hp_ref, xs_hbm.at[pl.ds(0, tb)], sem).wait()

    @pl.when(i == last)
    def _():
        for r in range(n_sample):
            for k in range(TOP_K):
                row_copy(hs_ref, last * tb, r, k).wait()


def _dispatch(hn_prompt, hn_sample, n_sample, dest_flat, pad_lo, pad_hi, n_slots):
    tb = COMBINE_TOKENS
    n_prompt = hn_prompt.shape[0]
    assert n_prompt % tb == 0
    row = hn_prompt.shape[1:]
    return pl.pallas_call(
        functools.partial(_dispatch_kernel, n_sample=n_sample),
        grid_spec=pltpu.PrefetchScalarGridSpec(
            num_scalar_prefetch=3,
            grid=(n_prompt // tb + 1,),
            in_specs=[pl.BlockSpec((tb,) + row,
                                   lambda i, d, lo, hi: (jnp.minimum(i, n_prompt // tb - 1), 0, 0)),
                      pl.BlockSpec(hn_sample.shape, lambda i, d, lo, hi: (0, 0, 0))],
            out_specs=pl.BlockSpec(memory_space=pl.ANY),
            scratch_shapes=[pltpu.VMEM((MOE_ROW_BLOCK,) + row, F32),
                            pltpu.SemaphoreType.DMA(()), pltpu.SemaphoreType.DMA(())],
        ),
        out_shape=jax.ShapeDtypeStruct((n_slots,) + row, F32),
        compiler_params=_cparams(("arbitrary",)),
        name="moe_dispatch",
    )(dest_flat, pad_lo, pad_hi, hn_prompt, hn_sample)


def _moe_kernel(item_e, item_row0, item_nch, tail_ref, xs_hbm, wg_ref, wu_ref, bg_ref, bu_ref,
                wd_ref, bd_ref, out_hbm, x_ref, acc_ref, stage_ref, wgb_ref, wub_ref, wdb_ref,
                sem_in, sem_out):
    it = pl.program_id(0)
    f = pl.program_id(1)
    nf = pl.num_programs(1)
    rb = MOE_ROW_BLOCK
    big = MOE_MATMUL_ROWS
    pieces = ROW_PIECES
    nch = item_nch[it]
    row0 = item_row0[it]
    n_big = nch // (big // rb)
    has_tail = nch % (big // rb) != 0
    tail_row = pl.multiple_of(n_big * big, rb)

    def hbm_rows(start, size):
        return pl.ds(pl.multiple_of((row0 + start) * pieces, rb * pieces), size * pieces)

    def in_copy(start, size, slot):
        return pltpu.make_async_copy(xs_hbm.at[hbm_rows(start, size)],
                                     stage_ref.at[slot, pl.ds(0, size * pieces)], sem_in.at[slot])

    def out_copy(start, size, slot):
        return pltpu.make_async_copy(stage_ref.at[slot, pl.ds(0, size * pieces)],
                                     out_hbm.at[hbm_rows(start, size)], sem_out.at[slot])

    def stage_to_x(start, size, slot):
        rows = pl.ds(start, size)
        for p in range(pieces):
            piece = stage_ref[slot, pl.ds(p, size, stride=pieces), :]
            x_ref[rows, p * LANES:(p + 1) * LANES] = piece.astype(BF16)
        acc_ref[rows, :] = jnp.broadcast_to(bd_ref[0], (size, acc_ref.shape[1]))

    def acc_to_stage(start, size, slot):
        rows = pl.ds(start, size)
        for p in range(pieces):
            stage_ref[slot, pl.ds(p, size, stride=pieces), :] = acc_ref[rows, p * LANES:(p + 1) * LANES]

    @pl.when(nch > 0)
    def _():
        wgb_ref[...] = wg_ref[0].astype(BF16)
        wub_ref[...] = wu_ref[0].astype(BF16)
        wdb_ref[...] = wd_ref[0].astype(BF16)
        bg = bg_ref[0]
        bu = bu_ref[0]

        def compute(start, size):
            rows = pl.ds(start, size)
            x = x_ref[rows, :]
            g = jnp.dot(x, wgb_ref[...], preferred_element_type=F32) + bg
            u = jnp.dot(x, wub_ref[...], preferred_element_type=F32) + bu
            g = jnp.minimum(g, SWIGLU_LIMIT)
            u = jnp.clip(u, -SWIGLU_LIMIT, SWIGLU_LIMIT)
            act = (u + 1.0) * g * jax.nn.sigmoid(SWIGLU_ALPHA * g)
            acc_ref[rows, :] += jnp.dot(act.astype(BF16), wdb_ref[...], preferred_element_type=F32)

        @pl.when(f == 0)
        def _():
            @pl.when(n_big > 0)
            def _():
                in_copy(0, big, 0).start()

            def step(r, c):
                slot = r % 2
                start = pl.multiple_of(r * big, big)

                @pl.when(r + 1 < n_big)
                def _():
                    in_copy(start + big, big, 1 - slot).start()

                in_copy(start, big, slot).wait()
                stage_to_x(start, big, slot)
                compute(start, big)
                return c

            lax.fori_loop(0, n_big, step, 0)

            @pl.when(has_tail)
            def _():
                cp = in_copy(tail_row, rb, 0)
                cp.start()
                cp.wait()
                stage_to_x(tail_row, rb, 0)
                compute(tail_row, rb)

        @pl.when((f > 0) & (f < nf - 1))
        def _():
            def step(r, c):
                compute(pl.multiple_of(r * big, big), big)
                return c

            lax.fori_loop(0, n_big, step, 0)

            @pl.when(has_tail)
            def _():
                compute(tail_row, rb)

        @pl.when(f == nf - 1)
        def _():
            def step(r, c):
                slot = r % 2
                start = pl.multiple_of(r * big, big)
                compute(start, big)

                @pl.when(r >= 2)
                def _():
                    out_copy(start - 2 * big, big, slot).wait()

                acc_to_stage(start, big, slot)
                out_copy(start, big, slot).start()
                return c

            lax.fori_loop(0, n_big, step, 0)

            @pl.when(n_big >= 2)
            def _():
                out_copy(0, big, n_big % 2).wait()

            @pl.when(n_big >= 1)
            def _():
                out_copy(0, big, (n_big - 1) % 2).wait()

            @pl.when(has_tail)
            def _():
                compute(tail_row, rb)
                acc_to_stage(tail_row, rb, 0)
                cp = out_copy(tail_row, rb, 0)
                cp.start()
                cp.wait()

    @pl.when((it == pl.num_programs(0) - 1) & (f == nf - 1))
    def _():
        tail0 = tail_ref[0]
        n_tail = (out_hbm.shape[0] // pieces - tail0) // rb
        stage_ref[0] = jnp.zeros(stage_ref.shape[1:], F32)

        def tail_copy(c):
            rows = pl.ds(pl.multiple_of((tail0 + c * rb) * pieces, rb * pieces), rb * pieces)
            return pltpu.make_async_copy(stage_ref.at[0, pl.ds(0, rb * pieces)], out_hbm.at[rows],
                                         sem_out.at[0])

        def tail_start(c, c2):
            tail_copy(c).start()
            return c2

        def tail_wait(c, c2):
            tail_copy(c).wait()
            return c2

        lax.fori_loop(0, n_tail, tail_start, 0)
        lax.fori_loop(0, n_tail, tail_wait, 0)


def _moe(xs3, items, w_gu, b_gu, w_down, b_down):
    item_e, item_row0, item_nch, tail0 = items
    n_items = item_e.shape[0]
    n_slots = xs3.shape[0]
    d = w_gu.shape[1]
    ff = w_down.shape[1]
    tf = MOE_FF_TILE
    nf = ff // tf
    assert MOE_MATMUL_ROWS == 2 * MOE_ROW_BLOCK and MOE_ITEM_ROWS % MOE_MATMUL_ROWS == 0 and nf >= 2
    xs2 = xs3.reshape(n_slots * ROW_PIECES, LANES)

    def ftile(it, f, nch):
        return jnp.where(nch[it] > 0, f, nf - 1)

    wg_spec = pl.BlockSpec((1, d, tf), lambda it, f, e, r0, nch, t0: (e[it], 0, ftile(it, f, nch)))
    wu_spec = pl.BlockSpec((1, d, tf), lambda it, f, e, r0, nch, t0: (e[it], 0, nf + ftile(it, f, nch)))
    bg_spec = pl.BlockSpec((1, 1, tf), lambda it, f, e, r0, nch, t0: (e[it], 0, ftile(it, f, nch)))
    bu_spec = pl.BlockSpec((1, 1, tf), lambda it, f, e, r0, nch, t0: (e[it], 0, nf + ftile(it, f, nch)))
    wd_spec = pl.BlockSpec((1, tf, d), lambda it, f, e, r0, nch, t0: (e[it], ftile(it, f, nch), 0))
    bd_spec = pl.BlockSpec((1, 1, d), lambda it, f, e, r0, nch, t0: (e[it], 0, 0))
    b_gu3 = b_gu.reshape(N_EXPERTS, 1, 2 * ff)
    b_down3 = b_down.reshape(N_EXPERTS, 1, d)
    out2 = pl.pallas_call(
        _moe_kernel,
        grid_spec=pltpu.PrefetchScalarGridSpec(
            num_scalar_prefetch=4,
            grid=(n_items, nf),
            in_specs=[pl.BlockSpec(memory_space=pl.ANY), wg_spec, wu_spec, bg_spec, bu_spec,
                      wd_spec, bd_spec],
            out_specs=pl.BlockSpec(memory_space=pl.ANY),
            scratch_shapes=[
                pltpu.VMEM((MOE_ITEM_ROWS, d), BF16),
                pltpu.VMEM((MOE_ITEM_ROWS, d), F32),
                pltpu.VMEM((2, MOE_MATMUL_ROWS * ROW_PIECES, LANES), F32),
                pltpu.VMEM((d, tf), BF16), pltpu.VMEM((d, tf), BF16), pltpu.VMEM((tf, d), BF16),
                pltpu.SemaphoreType.DMA((2,)), pltpu.SemaphoreType.DMA((2,)),
            ],
        ),
        out_shape=jax.ShapeDtypeStruct(xs2.shape, F32),
        compiler_params=_cparams(("arbitrary", "arbitrary"), MOE_VMEM_LIMIT_BYTES),
        name="moe_experts",
    )(item_e, item_row0, item_nch, tail0, xs2, w_gu, w_gu, b_gu3, b_gu3, w_down, b_down3)
    return out2


def _combine_kernel(dest_ref, h_ref, gate_ref, out_hbm, o_ref, buf_ref, sem, *, tok_off):
    i = pl.program_id(0)
    tb = h_ref.shape[0]
    pieces = ROW_PIECES
    pitch = COMBINE_ROW_PITCH

    def row_copy(step, r, k, slot):
        d = dest_ref[(tok_off + step * tb + r) * TOP_K + k]
        src = out_hbm.at[pl.ds(pl.multiple_of(d * pieces, pieces), pieces)]
        dst = buf_ref.at[slot, pl.ds(pl.multiple_of((k * tb + r) * pitch, SUBLANES), pieces)]
        return pltpu.make_async_copy(src, dst, sem.at[slot])

    def issue(step, slot):
        def body(r, c):
            for k in range(TOP_K):
                row_copy(step, r, k, slot).start(priority=k % 2)
            return c
        lax.fori_loop(0, tb, body, 0)

    @pl.when(i == 0)
    def _():
        issue(0, 0)

    @pl.when(i + 1 < pl.num_programs(0))
    def _():
        issue(i + 1, (i + 1) % 2)

    slot = i % 2
    pltpu.make_async_copy(out_hbm.at[pl.ds(0, TOP_K * tb * pieces)],
                          buf_ref.at[slot, pl.ds(0, TOP_K * tb * pieces)], sem.at[slot]).wait()

    for p in range(pieces):
        f = jnp.zeros((tb, LANES), F32)
        for k in range(TOP_K):
            rows = buf_ref[slot, pl.ds(k * tb * pitch + p, tb, stride=pitch), :]
            f = f + gate_ref[:, k:k + 1] * rows
        o_ref[:, p * LANES:(p + 1) * LANES] = h_ref[:, p * LANES:(p + 1) * LANES] + f


def _combine(dest_flat, h, gates, out2, tok_off, tb):
    t, d = h.shape
    return pl.pallas_call(
        functools.partial(_combine_kernel, tok_off=tok_off),
        grid_spec=pltpu.PrefetchScalarGridSpec(
            num_scalar_prefetch=1,
            grid=(t // tb,),
            in_specs=[pl.BlockSpec((tb, d), lambda i, ds: (i, 0)),
                      pl.BlockSpec((tb, LANES), lambda i, ds: (i, 0)),
                      pl.BlockSpec(memory_space=pl.ANY)],
            out_specs=pl.BlockSpec((tb, d), lambda i, ds: (i, 0)),
            scratch_shapes=[pltpu.VMEM((2, TOP_K * tb * COMBINE_ROW_PITCH, LANES), F32),
                            pltpu.SemaphoreType.DMA((2,))],
        ),
        out_shape=jax.ShapeDtypeStruct((t, d), F32),
        compiler_params=_cparams(("arbitrary",)),
        name="moe_combine",
    )(dest_flat, h, gates, out2)


def _ple_kernel(h_ref, p_ref, n_ref, wg_ref, wp_ref, y_ref):
    h = h_ref[...]
    hn = (_rms(h) * n_ref[...]).astype(BF16)
    gate = jax.nn.sigmoid(jnp.dot(hn, wg_ref[...], preferred_element_type=F32))
    proj = jnp.dot(p_ref[...].astype(BF16), wp_ref[...], preferred_element_type=F32)
    y_ref[...] = h + gate * proj


def _ple(h, p, norm, wg_bf, wp_bf, tm):
    t, d = h.shape
    pd = p.shape[1]
    const = lambda shape: pl.BlockSpec(shape, lambda i: (0, 0))
    return pl.pallas_call(
        _ple_kernel,
        grid=(t // tm,),
        in_specs=[pl.BlockSpec((tm, d), lambda i: (i, 0)), pl.BlockSpec((tm, pd), lambda i: (i, 0)),
                  const((1, d)), const((d, d)), const((pd, d))],
        out_specs=pl.BlockSpec((tm, d), lambda i: (i, 0)),
        out_shape=jax.ShapeDtypeStruct((t, d), F32),
        compiler_params=_cparams(("parallel",)),
        name="ple",
    )(h, p, norm, wg_bf, wp_bf)


def _route(top_idx, n_items):
    rb = MOE_ROW_BLOCK
    flat = top_idx.reshape(-1)
    onehot = (flat[:, None] == jnp.arange(N_EXPERTS, dtype=jnp.int32)[None, :]).astype(jnp.int32)
    csum = jnp.cumsum(onehot, axis=0)
    rank = jnp.sum(onehot * (csum - 1), axis=1)
    counts = csum[-1]
    padded = (counts + rb - 1) // rb * rb
    seg_end = jnp.cumsum(padded)
    seg_start = seg_end - padded
    dest = (jnp.sum(onehot * seg_start[None, :], axis=1) + rank).astype(jnp.int32)
    pad_lo = (seg_start + counts).astype(jnp.int32)
    pad_hi = seg_end.astype(jnp.int32)
    per = (padded + MOE_ITEM_ROWS - 1) // MOE_ITEM_ROWS
    item_end = jnp.cumsum(per)
    item_start = item_end - per
    ids = jnp.arange(n_items, dtype=jnp.int32)
    e = jnp.minimum(jnp.searchsorted(item_end, ids, side="right"), N_EXPERTS - 1).astype(jnp.int32)
    valid = ids < item_end[-1]
    piece = ids - item_start[e]
    row0 = seg_start[e] + piece * MOE_ITEM_ROWS
    rows = jnp.clip(padded[e] - piece * MOE_ITEM_ROWS, 0, MOE_ITEM_ROWS)
    nch = jnp.where(valid, rows // rb, 0).astype(jnp.int32)
    last_e = e[jnp.maximum(item_end[-1] - 1, 0)]
    item_e = jnp.where(valid, e, last_e).astype(jnp.int32)
    item_row0 = jnp.where(valid, row0, 0).astype(jnp.int32)
    return dest, pad_lo, pad_hi, (item_e, item_row0, nch, pad_hi[-1:])


def _pad_rows(a, rows):
    return jnp.concatenate([a, jnp.zeros((rows - a.shape[0],) + a.shape[1:], a.dtype)], axis=0)


def kernel(x_prompt, x_sample, cache_k, cache_v, state_ret, page_table, p_prompt, p_sample,
           norm_mix, w_in, q_norm, k_norm, w_o, norm_ffn, w_router, b_router, w_gu, b_gu,
           w_down, b_down, norm_ple, w_ple_gate, w_ple_proj):
    depth = norm_mix.shape[0]
    assert depth == 1
    batch, seq_len, d = x_prompt.shape
    n_dec, dec_seq, _ = x_sample.shape
    assert dec_seq == 1 and n_dec <= SAMPLE_ROWS
    past_len = page_table.shape[1] * cache_k.shape[2]
    ret_w = RET_HEADS * RET_DK
    moba_w = MOBA_HEADS * MOBA_HEAD_DIM
    off = [0, ret_w, 2 * ret_w, 3 * ret_w, 4 * ret_w, 4 * ret_w + moba_w, 4 * ret_w + 2 * moba_w]

    w_in_bf = w_in[0].astype(BF16)
    w_o_bf = w_o[0].astype(BF16)
    wg_ple_bf = w_ple_gate[0].astype(BF16)
    wp_ple_bf = w_ple_proj[0].astype(BF16)
    wr_pad = jnp.concatenate([w_router[0], jnp.zeros((d, LANES - N_EXPERTS), F32)],
                             axis=1).astype(BF16)
    br_pad = jnp.concatenate([b_router[0], jnp.full((LANES - N_EXPERTS,), -jnp.inf, F32)])[None, :]
    qn, kn = q_norm, k_norm

    t_p = batch * seq_len
    xp = x_prompt.reshape(t_p, d)
    tm = 1024
    tables_p = _rope_tables(jnp.arange(seq_len, dtype=jnp.int32))
    z_p, k_p, v_p = _inproj(xp, norm_mix, w_in_bf, tables_p, qn, kn, seq_len, tm)
    ret_p, state_p = _ret_prompt(z_p, batch, seq_len)
    moba_p = _moba_prompt(z_p, batch, seq_len)

    xs_rows = _pad_rows(x_sample.reshape(n_dec, d), SAMPLE_ROWS)
    tables_s = _rope_tables(jnp.full((SAMPLE_ROWS,), past_len, jnp.int32))
    z_s, k_s, v_s = _inproj(xs_rows, norm_mix, w_in_bf, tables_s, qn, kn, SAMPLE_ROWS, SAMPLE_ROWS)
    zs = z_s[:n_dec]
    ret_s, state_s = _ret_sample(zs[:, off[0]:off[1]], zs[:, off[1]:off[2]], zs[:, off[2]:off[3]],
                                 zs[:, off[3]:off[4]], state_ret[0])
    heads = lambda a: a.reshape(n_dec, MOBA_HEADS, MOBA_HEAD_DIM)
    mk_s, mv_s = k_s[:n_dec], v_s[:n_dec]
    pool_shape = cache_k.shape[1:]
    moba_s = _moba_sample(heads(zs[:, off[4]:off[5]]), heads(mk_s), heads(mv_s),
                          cache_k.reshape(pool_shape), cache_v.reshape(pool_shape), page_table)

    h_p, hn_p, idx_p, gate_p = _outproj(xp, ret_p, moba_p, w_o_bf, norm_ffn, wr_pad, br_pad, 256)
    h_s, hn_s, idx_s, gate_s = _outproj(xs_rows, _pad_rows(ret_s, SAMPLE_ROWS),
                                        _pad_rows(moba_s, SAMPLE_ROWS), w_o_bf, norm_ffn,
                                        wr_pad, br_pad, SAMPLE_ROWS)

    n_tok = t_p + n_dec
    top_idx = jnp.concatenate([idx_p[:, :TOP_K], idx_s[:n_dec, :TOP_K]], axis=0)
    n_assign = n_tok * TOP_K
    n_chunks = n_assign // MOE_ROW_BLOCK + N_EXPERTS
    n_slots = n_chunks * MOE_ROW_BLOCK
    n_items = N_EXPERTS + pl.cdiv(n_assign, MOE_ITEM_ROWS)
    dest, pad_lo, pad_hi, items = _route(top_idx, n_items)
    rows3 = lambda a: a.reshape(-1, ROW_PIECES, d // ROW_PIECES)
    xs3 = _dispatch(rows3(hn_p), rows3(hn_s), n_dec, dest, pad_lo, pad_hi, n_slots)
    out2 = _moe(xs3, items, w_gu[0], b_gu[0], w_down[0], b_down[0])
    dest_pad = jnp.concatenate([dest, jnp.zeros(((SAMPLE_ROWS - n_dec) * TOP_K,), jnp.int32)])
    h2_p = _combine(dest_pad, h_p, gate_p, out2, 0, COMBINE_BLOCK)
    h2_s = _combine(dest_pad, h_s, gate_s, out2, t_p, SAMPLE_ROWS)

    y_p = _ple(h2_p, p_prompt[0].reshape(t_p, -1), norm_ple, wg_ple_bf, wp_ple_bf, 512)
    y_s = _ple(h2_s, _pad_rows(p_sample[0].reshape(n_dec, -1), SAMPLE_ROWS), norm_ple,
               wg_ple_bf, wp_ple_bf, SAMPLE_ROWS)

    kv = lambda a, n, l: a.reshape(1, n, l, MOBA_HEADS, MOBA_HEAD_DIM)
    return (y_p.reshape(batch, seq_len, d), y_s[:n_dec].reshape(n_dec, 1, d),
            kv(k_p, batch, seq_len), kv(v_p, batch, seq_len),
            state_p[None], kv(mk_s, n_dec, 1), kv(mv_s, n_dec, 1), state_s[None])
```
